```python
import math
import jax
import jax.numpy as jnp
from jax import lax
import numpy as np


D_MODEL = 1024
BATCH = 8
SEQ = 2048
DEPTH = 4

GRID_W = 64
CTX_LEN = 256
N_MIXERS = 4
EPS = 1e-6
NEG_INF = -1e30
F32 = jnp.float32

HY_ORDER = 2
HY_SHORT = 3
HY_EMB = 33
HY_FFN = 64
HY_FAST_DECAY = 0.3
HY_SLOW_DECAY = 1.5
HY_TARGET = 1e-2
HY_SHIFT = 0.05

SW_HQ = 16
SW_HKV = 4
SW_REP = SW_HQ // SW_HKV
SW_HD = 64
SW_WINDOW = 128
SW_BLOCK = SW_WINDOW
ROPE_BASE = 10000.0

GD_H = 8
GD_DK = 128
GD_DV = 128
GD_QK = GD_H * GD_DK
GD_V = GD_H * GD_DV
GD_CONV = 3
GD_CHUNK = 64

HG_DK = 128
HG_H = D_MODEL // HG_DK
HG_DV = D_MODEL // HG_H
HG_CHUNK = 64

N_EXPERTS = 16
EXPERT_FF = 1024
EC_CAPACITY = 2

kernel_name = "hybrid_hyena_swa_gdn_hgrn2_ecmoe_diffusion"


def rms_norm(x, w):
    xf = x.astype(F32)
    y = xf * lax.rsqrt(jnp.mean(xf * xf, axis=-1, keepdims=True) + EPS)
    return (y * w.astype(F32)).astype(x.dtype)


def l2_normalize(x):
    xf = x.astype(F32)
    return xf * lax.rsqrt(jnp.sum(xf * xf, axis=-1, keepdims=True) + EPS)


def orient(t, direction):
    return jnp.flip(t, axis=1) if direction == 1 else t


def centred_depthwise_conv(x, w, b=None):
    K, C = w.shape
    y = lax.conv_general_dilated(x, w[:, None, :].astype(x.dtype), window_strides=(1,),
                                 padding=[(K // 2, K // 2)],
                                 dimension_numbers=('NWC', 'WIO', 'NWC'), feature_group_count=C)
    return y if b is None else y + b


def hyena_positional_features(L):
    t = jnp.linspace(0.0, 1.0, L, dtype=F32)[:, None]
    bands = (HY_EMB - 1) // 2
    w = 2.0 * math.pi * jnp.arange(L, dtype=F32)[:, None] / L
    f = jnp.linspace(1e-4, bands - 1, bands, dtype=F32)[None, :]
    z = jnp.concatenate([t, jnp.cos(f * w), -jnp.sin(f * w)], axis=-1)
    return t, z


def hyena_filters(L, w1, b1, w2, b2, w3, freq):
    t, z = hyena_positional_features(L)
    h = jnp.sin(freq[0].astype(F32) * (z @ w1.astype(F32) + b1.astype(F32)))
    h = jnp.sin(freq[1].astype(F32) * (h @ w2.astype(F32) + b2.astype(F32)))
    h = (h @ w3.astype(F32)).reshape(L, HY_ORDER, 2, D_MODEL)
    max_decay = math.log(HY_TARGET) / HY_FAST_DECAY
    min_decay = math.log(HY_TARGET) / HY_SLOW_DECAY
    deltas = jnp.abs(jnp.linspace(min_decay, max_decay, D_MODEL, dtype=F32))
    window = jnp.exp(-t * deltas) + HY_SHIFT
    return h * window[:, None, None, :]


def two_sided_fft_conv(u, h_fwd, h_bwd, bias):
    L = u.shape[1]
    kern = jnp.concatenate([h_fwd, jnp.zeros((1, h_fwd.shape[1]), F32), h_bwd[:0:-1]], axis=0)
    uf = u.astype(F32)
    spec = jnp.fft.rfft(uf, n=2 * L, axis=1) * jnp.fft.rfft(kern, n=2 * L, axis=0)[None]
    y = jnp.fft.irfft(spec, n=2 * L, axis=1)[:, :L]
    return (y + uf * bias.astype(F32)).astype(u.dtype)


def hyena_mixer(h, w_in, b_in, short_w, short_b, w1, b1, w2, b2, w3, freq, filt_bias, w_out, b_out):
    L = h.shape[1]
    u = centred_depthwise_conv(h @ w_in + b_in, short_w, short_b)
    v, x1, x2 = jnp.split(u, 3, axis=-1)
    filt = hyena_filters(L, w1, b1, w2, b2, w3, freq)
    z = x1 * two_sided_fft_conv(v, filt[:, 0, 0], filt[:, 0, 1], filt_bias[0])
    z = x2 * two_sided_fft_conv(z, filt[:, 1, 0], filt[:, 1, 1], filt_bias[1])
    return z @ w_out + b_out


def axial_rope_tables(L):
    rows = L // GRID_W
    row = jnp.repeat(jnp.arange(rows, dtype=F32), GRID_W)
    col = jnp.tile(jnp.arange(GRID_W, dtype=F32), rows)
    nf = SW_HD // 4
    inv = ROPE_BASE ** (-jnp.arange(nf, dtype=F32) / nf)
    ang = jnp.concatenate([row[:, None] * inv, col[:, None] * inv], axis=-1)
    return jnp.cos(ang), jnp.sin(ang)


def apply_rope(x, cos, sin):
    shape = (1, cos.shape[0]) + (1,) * (x.ndim - 3) + (cos.shape[1],)
    cos = cos.reshape(shape).astype(x.dtype)
    sin = sin.reshape(shape).astype(x.dtype)
    x1, x2 = jnp.split(x, 2, axis=-1)
    return jnp.concatenate([x1 * cos - x2 * sin, x2 * cos + x1 * sin], axis=-1)


def sink_softmax(s, sink):
    sk = jnp.broadcast_to(sink.astype(F32).reshape(SW_HKV, SW_REP, 1, 1), s.shape[:-1] + (1,))
    return jax.nn.softmax(jnp.concatenate([s, sk], axis=-1), axis=-1)[..., :-1]


def swa_mixer(h_ctx, h_lat, w_in, sink, w_out, with_ctx_out):
    B, L, _ = h_lat.shape
    scale = SW_HD ** -0.5

    def project(h):
        n = h.shape[1]
        u = h @ w_in
        q = u[..., :SW_HQ * SW_HD].reshape(B, n, SW_HKV, SW_REP, SW_HD)
        k = u[..., SW_HQ * SW_HD:(SW_HQ + SW_HKV) * SW_HD].reshape(B, n, SW_HKV, SW_HD)
        v = u[..., (SW_HQ + SW_HKV) * SW_HD:].reshape(B, n, SW_HKV, SW_HD)
        return q, k, v

    q_c, k_c, v_c = project(h_ctx)
    q_l, k_l, v_l = project(h_lat)
    cos, sin = axial_rope_tables(L)
    q_l = apply_rope(q_l, cos, sin)
    k_l = apply_rope(k_l, cos, sin)
    n_ctx = k_c.shape[1]
    nb = L // SW_BLOCK

    def band(t):
        tp = jnp.pad(t, ((0, 0), (SW_BLOCK, SW_BLOCK), (0, 0), (0, 0)))
        tp = tp.reshape(B, nb + 2, SW_BLOCK, SW_HKV, SW_HD)
        return jnp.concatenate([tp[:, :-2], tp[:, 1:-1], tp[:, 2:]], axis=2).swapaxes(0, 1)

    blk = jnp.arange(nb)[:, None, None]
    qpos = blk * SW_BLOCK + jnp.arange(SW_BLOCK)[None, :, None]
    kpos = (blk - 1) * SW_BLOCK + jnp.arange(3 * SW_BLOCK)[None, None, :]
    band_mask = (jnp.abs(qpos - kpos) <= SW_WINDOW) & (kpos >= 0) & (kpos < L)
    q_blocks = q_l.reshape(B, nb, SW_BLOCK, SW_HKV, SW_REP, SW_HD).swapaxes(0, 1)

    def block_attention(xs):
        q_b, k_b, v_b, m_b = xs
        s_ctx = jnp.einsum('bqgrd,bkgd->bgrqk', q_b, k_c).astype(F32) * scale
        s_loc = jnp.einsum('bqgrd,bkgd->bgrqk', q_b, k_b).astype(F32) * scale
        s_loc = jnp.where(m_b, s_loc, NEG_INF)
        p = sink_softmax(jnp.concatenate([s_ctx, s_loc], axis=-1), sink).astype(v_b.dtype)
        return (jnp.einsum('bgrqk,bkgd->bqgrd', p[..., :n_ctx], v_c)
                + jnp.einsum('bgrqk,bkgd->bqgrd', p[..., n_ctx:], v_b))

    o_l = lax.map(block_attention, (q_blocks, band(k_l), band(v_l), band_mask))
    y_lat = o_l.swapaxes(0, 1).reshape(B, L, SW_HQ * SW_HD) @ w_out
    y_ctx = None
    if with_ctx_out:
        s = jnp.einsum('bqgrd,bkgd->bgrqk', q_c, k_c).astype(F32) * scale
        p = sink_softmax(s, sink).astype(v_c.dtype)
        y_ctx = jnp.einsum('bgrqk,bkgd->bqgrd', p, v_c).reshape(B, n_ctx, SW_HQ * SW_HD) @ w_out
    return y_ctx, y_lat


def to_chunks(t, size):
    B, L, H = t.shape[:3]
    t = t.reshape((B, L // size, size, H) + t.shape[3:])
    return jnp.moveaxis(t, (1, 3), (0, 2))


def from_chunks(t):
    t = jnp.moveaxis(t, (0, 2), (1, 3))
    return t.reshape((t.shape[0], t.shape[1] * t.shape[2]) + t.shape[3:])


def chunk_gated_delta(q, k, v, g, beta, S0):
    dv = v.shape[-1]
    C = GD_CHUNK
    qc, kc, vc = to_chunks(q, C), to_chunks(k, C), to_chunks(v, C)
    gc, bc = to_chunks(g, C), to_chunks(beta, C)
    G = jnp.cumsum(gc, axis=-1)
    incl = jnp.tril(jnp.ones((C, C), bool))
    strict = jnp.tril(jnp.ones((C, C), bool), -1)
    decay = jnp.exp(jnp.where(incl, G[..., :, None] - G[..., None, :], -jnp.inf))
    kb = kc * bc[..., None]
    A = jnp.where(strict, jnp.einsum('nbhid,nbhjd->nbhij', kb, kc) * decay, 0.0)
    rhs = jnp.concatenate([vc * bc[..., None], kb * jnp.exp(G)[..., None]], axis=-1)
    uw = lax.linalg.triangular_solve(jnp.eye(C, dtype=F32) + A, rhs, left_side=True, lower=True,
                                     unit_diagonal=True)
    u, w = uw[..., :dv], uw[..., dv:]

    def step(S, xs):
        q_i, k_i, u_i, w_i, G_i, dec_i = xs
        v_new = u_i - jnp.einsum('bhck,bhkv->bhcv', w_i, S)
        attn = jnp.einsum('bhik,bhjk->bhij', q_i, k_i) * dec_i
        o = (jnp.einsum('bhck,bhkv->bhcv', q_i * jnp.exp(G_i)[..., None], S)
             + jnp.einsum('bhij,bhjv->bhiv', attn, v_new))
        g_last = G_i[..., -1:]
        S = (S * jnp.exp(g_last)[..., None]
             + jnp.einsum('bhck,bhcv->bhkv', k_i * jnp.exp(g_last - G_i)[..., None], v_new))
        return S, o

    S, o = lax.scan(step, S0, (qc, kc, u, w, G, decay))
    return from_chunks(o), S


def chunk_gla(q, k, v, logf, S0):
    C = HG_CHUNK
    qc, kc, vc, gc = to_chunks(q, C), to_chunks(k, C), to_chunks(v, C), to_chunks(logf, C)
    G = jnp.cumsum(gc, axis=-2)
    incl = jnp.tril(jnp.ones((C, C), bool))[:, :, None]

    def step(S, xs):
        q_i, k_i, v_i, G_i = xs
        dec = jnp.exp(jnp.where(incl, G_i[:, :, :, None, :] - G_i[:, :, None, :, :], -jnp.inf))
        attn = jnp.einsum('bhtk,bhtsk->bhts', q_i, dec * k_i[:, :, None, :, :])
        o = (jnp.einsum('bhtk,bhkv->bhtv', q_i * jnp.exp(G_i), S)
             + jnp.einsum('bhts,bhsv->bhtv', attn, v_i))
        g_last = G_i[:, :, -1:, :]
        S = (S * jnp.exp(g_last[:, :, 0, :])[..., None]
             + jnp.einsum('bhsk,bhsv->bhkv', k_i * jnp.exp(g_last - G_i), v_i))
        return S, o

    S, o = lax.scan(step, S0, (qc, kc, vc, G))
    return from_chunks(o), S


def gdn_mixer(h_ctx, h_lat, w_in, conv_w, a_log, dt_bias, norm_w, w_out, with_ctx_out):
    n_qkv = 2 * GD_QK + GD_V

    def project(h):
        B, n, _ = h.shape
        u = h @ w_in
        qkv = jax.nn.silu(centred_depthwise_conv(u[..., :n_qkv], conv_w))
        q = l2_normalize(qkv[..., :GD_QK].reshape(B, n, GD_H, GD_DK)) * GD_DK ** -0.5
        k = l2_normalize(qkv[..., GD_QK:2 * GD_QK].reshape(B, n, GD_H, GD_DK))
        v = qkv[..., 2 * GD_QK:].astype(F32).reshape(B, n, GD_H, GD_DV)
        z = u[..., n_qkv:n_qkv + GD_V]
        ba = u[..., n_qkv + GD_V:].astype(F32).reshape(B, n, 2, 2, GD_H)
        beta = jax.nn.sigmoid(ba[:, :, 0])
        g = -jnp.exp(a_log.astype(F32)) * jax.nn.softplus(ba[:, :, 1] + dt_bias.astype(F32))
        return q, k, v, z, beta, g

    qc, kc, vc, zc, bc, gc = project(h_ctx)
    ql, kl, vl, zl, bl, gl = project(h_lat)
    S0 = jnp.zeros((h_lat.shape[0], GD_H, GD_DK, GD_DV), F32)
    o_c = 0.0
    o_l = 0.0
    for d in range(2):
        oc_d, S_c = chunk_gated_delta(orient(qc, d), orient(kc, d), orient(vc, d),
                                      orient(gc[:, :, d], d), orient(bc[:, :, d], d), S0)
        ol_d, _ = chunk_gated_delta(orient(ql, d), orient(kl, d), orient(vl, d),
                                    orient(gl[:, :, d], d), orient(bl[:, :, d], d), S_c)
        o_c = o_c + orient(oc_d, d)
        o_l = o_l + orient(ol_d, d)

    def readout(o, z):
        B, n = o.shape[:2]
        y = rms_norm(o, norm_w) * jax.nn.silu(z.astype(F32)).reshape(B, n, GD_H, GD_DV)
        return y.reshape(B, n, GD_V).astype(z.dtype) @ w_out

    y_ctx = readout(o_c, zc) if with_ctx_out else None
    return y_ctx, readout(o_l, zl)


def hgrn2_mixer(h_ctx, h_lat, w_in, lb, norm_w, w_out, with_ctx_out):
    lbh = lb.reshape(HG_H, HG_DK)

    def project(h):
        B, n, _ = h.shape
        q, f_fw, f_bw, i, g = jnp.split(h @ w_in, 5, axis=-1)
        q = jax.nn.silu(q).astype(F32).reshape(B, n, HG_H, HG_DK)
        f = jnp.stack([f_fw, f_bw], axis=2).astype(F32).reshape(B, n, 2, HG_H, HG_DK)
        logf = jnp.log(lbh + (1.0 - lbh) * jax.nn.sigmoid(f))
        k = (1.0 - lbh) * jax.nn.sigmoid(-f)
        v = i.astype(F32).reshape(B, n, HG_H, HG_DV)
        return q, k, v, logf, g

    qc, kc, vc, fc, gc = project(h_ctx)
    ql, kl, vl, fl, gl = project(h_lat)
    S0 = jnp.zeros((h_lat.shape[0], HG_H, HG_DK, HG_DV), F32)
    o_c = 0.0
    o_l = 0.0
    for d in range(2):
        oc_d, S_c = chunk_gla(orient(qc, d), orient(kc[:, :, d], d), orient(vc, d),
                              orient(fc[:, :, d], d), S0)
        ol_d, _ = chunk_gla(orient(ql, d), orient(kl[:, :, d], d), orient(vl, d),
                            orient(fl[:, :, d], d), S_c)
        o_c = o_c + orient(oc_d, d)
        o_l = o_l + orient(ol_d, d)

    def readout(o, g):
        B, n = o.shape[:2]
        y = rms_norm(o, norm_w) * jax.nn.silu(g.astype(F32)).reshape(B, n, HG_H, HG_DV)
        return y.reshape(B, n, D_MODEL).astype(g.dtype) @ w_out

    y_ctx = readout(o_c, gc) if with_ctx_out else None
    return y_ctx, readout(o_l, gl)


def expert_choice_ffn(h, router_w, w_gate, w_up, w_down):
    B, n, D = h.shape
    cap = EC_CAPACITY * n // N_EXPERTS
    aff = jax.nn.softmax((h @ router_w).astype(F32), axis=-1)
    gate, idx = lax.top_k(jnp.swapaxes(aff, 1, 2), cap)
    xe = jax.vmap(lambda hb, ib: hb[ib])(h, idx)
    a = jnp.einsum('becd,edf->becf', xe, w_gate)
    u = jnp.einsum('becd,edf->becf', xe, w_up)
    ye = jnp.einsum('becf,efd->becd', jax.nn.silu(a) * u, w_down) * gate[..., None].astype(h.dtype)
    return jax.vmap(lambda ib, yb: jnp.zeros((n, D), h.dtype).at[ib.reshape(-1)].add(yb.reshape(-1, D)))(idx, ye)


def setup_inputs(seed: int = 0) -> dict:
    key = jax.random.key(seed)
    keys = iter(jax.random.split(key, 48))

    def nrm(shape, std):
        return std * jax.random.normal(next(keys), shape, F32)

    def gain(shape):
        return 1.0 + nrm(shape, 0.02)

    D = D_MODEL
    gd_cols = 2 * GD_QK + 2 * GD_V + 4 * GD_H
    gd_a_log = jnp.log(jax.random.uniform(next(keys), (2, GD_H), F32, 1.0, 16.0))
    dt = jnp.exp(jax.random.uniform(next(keys), (2, GD_H), F32, math.log(1e-3), math.log(1e-1)))
    gd_dt_bias = jnp.log(jnp.expm1(dt))
    return {
        'x': nrm((BATCH, SEQ, D), 1.0),
        'c': nrm((BATCH, D), 1.0),
        'ctx': nrm((BATCH, CTX_LEN, D), 1.0),
        'c_ctx': nrm((D,), 1.0),
        'ada_w': nrm((DEPTH, D, 6 * D), 0.5 * D ** -0.5),
        'ada_b': nrm((DEPTH, 6 * D), 0.02),
        'norm1_w': gain((DEPTH, D)),
        'norm2_w': gain((DEPTH, D)),
        'final_norm_w': gain((D,)),
        'hy_w_in': nrm((D, 3 * D), D ** -0.5),
        'hy_b_in': nrm((3 * D,), 0.02),
        'hy_short_w': nrm((HY_SHORT, 3 * D), HY_SHORT ** -0.5),
        'hy_short_b': nrm((3 * D,), 0.02),
        'hy_ffn_w1': nrm((HY_EMB, HY_FFN), HY_EMB ** -0.5),
        'hy_ffn_b1': nrm((HY_FFN,), 0.02),
        'hy_ffn_w2': nrm((HY_FFN, HY_FFN), HY_FFN ** -0.5),
        'hy_ffn_b2': nrm((HY_FFN,), 0.02),
        'hy_ffn_w3': nrm((HY_FFN, HY_ORDER * 2 * D), 0.01),
        'hy_sin_freq': gain((2, HY_FFN)),
        'hy_filter_bias': nrm((HY_ORDER, D), 0.5),
        'hy_w_out': nrm((D, D), D ** -0.5),
        'hy_b_out': nrm((D,), 0.02),
        'sw_w_in': nrm((D, (SW_HQ + 2 * SW_HKV) * SW_HD), D ** -0.5),
        'sw_sink': nrm((SW_HQ,), 0.5),
        'sw_w_out': nrm((SW_HQ * SW_HD, D), (SW_HQ * SW_HD) ** -0.5),
        'gd_w_in': nrm((D, gd_cols), D ** -0.5),
        'gd_conv_w': nrm((GD_CONV, 2 * GD_QK + GD_V), GD_CONV ** -0.5),
        'gd_a_log': gd_a_log,
        'gd_dt_bias': gd_dt_bias,
        'gd_norm_w': gain((GD_DV,)),
        'gd_w_out': nrm((GD_V, D), GD_V ** -0.5),
        'hg_w_in': nrm((D, 5 * D), D ** -0.5),
        'hg_lb': nrm((DEPTH, D), 1.0),
        'hg_norm_w': gain((HG_DV,)),
        'hg_w_out': nrm((D, D), D ** -0.5),
        'moe_router': nrm((DEPTH, D, N_EXPERTS), D ** -0.5),
        'moe_w_gate': nrm((DEPTH, N_EXPERTS, D, EXPERT_FF), D ** -0.5),
        'moe_w_up': nrm((DEPTH, N_EXPERTS, D, EXPERT_FF), D ** -0.5),
        'moe_w_down': nrm((DEPTH, N_EXPERTS, EXPERT_FF, D), EXPERT_FF ** -0.5),
    }


def reference(x, c, ctx, c_ctx, ada_w, ada_b, norm1_w, norm2_w, final_norm_w,
              hy_w_in, hy_b_in, hy_short_w, hy_short_b, hy_ffn_w1, hy_ffn_b1, hy_ffn_w2, hy_ffn_b2,
              hy_ffn_w3, hy_sin_freq, hy_filter_bias, hy_w_out, hy_b_out,
              sw_w_in, sw_sink, sw_w_out,
              gd_w_in, gd_conv_w, gd_a_log, gd_dt_bias, gd_norm_w, gd_w_out,
              hg_w_in, hg_lb, hg_norm_w, hg_w_out,
              moe_router, moe_w_gate, moe_w_up, moe_w_down):
    lat, cx = x, ctx
    lb_all = jnp.cumsum(jax.nn.softmax(hg_lb.astype(F32), axis=0), axis=0)
    for layer in range(DEPTH):
        kind = layer % N_MIXERS
        keep_ctx = layer < DEPTH - 1
        mod_l = [m[:, None, :] for m in jnp.split(jax.nn.silu(c) @ ada_w[layer] + ada_b[layer], 6, axis=-1)]
        mod_c = jnp.split(jax.nn.silu(c_ctx) @ ada_w[layer] + ada_b[layer], 6, axis=-1)
        h_l = rms_norm(lat, norm1_w[layer]) * (1.0 + mod_l[1]) + mod_l[0]
        h_c = rms_norm(cx, norm1_w[layer]) * (1.0 + mod_c[1]) + mod_c[0]
        if kind == 0:
            hy_args = (hy_w_in, hy_b_in, hy_short_w, hy_short_b, hy_ffn_w1, hy_ffn_b1, hy_ffn_w2,
                       hy_ffn_b2, hy_ffn_w3, hy_sin_freq, hy_filter_bias, hy_w_out, hy_b_out)
            y_l = hyena_mixer(h_l, *hy_args)
            y_c = hyena_mixer(h_c, *hy_args) if keep_ctx else None
        elif kind == 1:
            y_c, y_l = swa_mixer(h_c, h_l, sw_w_in, sw_sink, sw_w_out, keep_ctx)
        elif kind == 2:
            y_c, y_l = gdn_mixer(h_c, h_l, gd_w_in, gd_conv_w, gd_a_log, gd_dt_bias, gd_norm_w,
                                 gd_w_out, keep_ctx)
        else:
            lb = lb_all[layer] - lb_all[0]
            y_c, y_l = hgrn2_mixer(h_c, h_l, hg_w_in, lb, hg_norm_w, hg_w_out, keep_ctx)
        lat = lat + mod_l[2] * y_l
        h_l = rms_norm(lat, norm2_w[layer]) * (1.0 + mod_l[4]) + mod_l[3]
        lat = lat + mod_l[5] * expert_choice_ffn(h_l, moe_router[layer], moe_w_gate[layer],
                                                 moe_w_up[layer], moe_w_down[layer])
        if keep_ctx:
            cx = cx + mod_c[2] * y_c
            h_c = rms_norm(cx, norm2_w[layer]) * (1.0 + mod_c[4]) + mod_c[3]
            cx = cx + mod_c[5] * expert_choice_ffn(h_c, moe_router[layer], moe_w_gate[layer],
                                                   moe_w_up[layer], moe_w_down[layer])
    return rms_norm(lat, final_norm_w)
```

```python
import functools
import math

import jax
import jax.numpy as jnp
import numpy as np
from jax import lax
from jax.experimental import pallas as pl
from jax.experimental.pallas import tpu as pltpu

F32 = jnp.float32
BF16 = jnp.bfloat16
HIGHEST = lax.Precision.HIGHEST
EPS = 1e-6
NEG_INF = -1e30
LANES = 128
MOD_ROWS = 16

GRID_W = 64
HY_ORDER = 2
HY_EMB = 33
HY_FAST_DECAY = 0.3
HY_SLOW_DECAY = 1.5
HY_TARGET = 1e-2
HY_SHIFT = 0.05
SW_HQ, SW_HKV, SW_HD, SW_WINDOW = 16, 4, 64, 128
SW_REP = SW_HQ // SW_HKV
ROPE_BASE = 10000.0
GD_H, GD_DK = 8, 128
HG_DK = 128
CHUNK = 64
SUPER = 256
DFT_ROWS = 512
N_EXPERTS = 16
EC_CAPACITY = 2

S = jax.ShapeDtypeStruct


def _cp(sem, vmem_mb=48):
    return pltpu.CompilerParams(dimension_semantics=sem, vmem_limit_bytes=vmem_mb * 2**20)


def _iota(shape, dim):
    return lax.broadcasted_iota(jnp.int32, shape, dim)


def _sigmoid(x):
    return 1.0 / (1.0 + jnp.exp(-x))


def _silu(x):
    return x * _sigmoid(x)


def _softplus(x):
    return jnp.maximum(x, 0.0) + jnp.log(1.0 + jnp.exp(-jnp.abs(x)))


def _mm(a, b):
    return jnp.dot(a.astype(BF16), b.astype(BF16), preferred_element_type=F32)


def _mm_nt(a, b):
    return lax.dot_general(a.astype(BF16), b.astype(BF16), (((1,), (1,)), ((), ())), preferred_element_type=F32)


def _mm_tn(a, b):
    return lax.dot_general(a.astype(BF16), b.astype(BF16), (((0,), (0,)), ((), ())), preferred_element_type=F32)


def _mm_f32(a, b):
    return jnp.dot(a, b, precision=HIGHEST, preferred_element_type=F32)


def _mod_body(c_ref, w_ref, b_ref, o_ref):
    o_ref[0] = _mm_f32(_silu(c_ref[...]), w_ref[0]) + b_ref[0]


def _modulation(c16, ada_w, ada_b):
    depth, d, n = ada_w.shape
    tn = 1024
    return pl.pallas_call(
        _mod_body, out_shape=S((depth, MOD_ROWS, n), F32), grid=(depth, n // tn),
        in_specs=[pl.BlockSpec((MOD_ROWS, d), lambda l, j: (0, 0)),
                  pl.BlockSpec((1, d, tn), lambda l, j: (l, 0, j)),
                  pl.BlockSpec((1, 1, tn), lambda l, j: (l, 0, j))],
        out_specs=pl.BlockSpec((1, MOD_ROWS, tn), lambda l, j: (l, 0, j)),
        compiler_params=_cp(("parallel", "parallel")), name="adaln_mod",
    )(c16, ada_w, ada_b.reshape(depth, 1, n))


def _norm_mod(x, nw, shift_ref, scale_ref, b, row0, n_lat, n_b):
    tm = x.shape[0]
    y = x * lax.rsqrt(jnp.mean(x * x, axis=-1, keepdims=True) + EPS) * nw
    is_ctx = (row0 + _iota((tm, 1), 0)) >= n_lat
    shift = jnp.where(is_ctx, shift_ref[n_b:n_b + 1, :], shift_ref[pl.ds(b, 1), :])
    scale = jnp.where(is_ctx, scale_ref[n_b:n_b + 1, :], scale_ref[pl.ds(b, 1), :])
    return y * (1.0 + scale) + shift


def _row_gate(gate_ref, b, row0, tm, n_lat, n_b):
    is_ctx = (row0 + _iota((tm, 1), 0)) >= n_lat
    return jnp.where(is_ctx, gate_ref[n_b:n_b + 1, :], gate_ref[pl.ds(b, 1), :])


def _in_proj_body(x_ref, nw_ref, sh_ref, sc_ref, w_ref, bias_ref, o_ref, h_scr, *, n_lat, n_b, tm):
    b, i = pl.program_id(0), pl.program_id(1)

    @pl.when(pl.program_id(2) == 0)
    def _():
        h_scr[...] = _norm_mod(x_ref[0], nw_ref[...], sh_ref, sc_ref, b, i * tm, n_lat, n_b).astype(BF16)

    o_ref[0] = jnp.dot(h_scr[...], w_ref[...], preferred_element_type=F32) + bias_ref[...]


def _in_proj(xs, norm_w, mod, shift_idx, scale_idx, w, bias, n_lat, name):
    bsz, t, d = xs.shape
    n = w.shape[1]
    tm = t // 3 if t % 3 == 0 and (t // 3) % 8 == 0 else t
    tn = 512 if n % 512 == 0 else (384 if n % 384 == 0 else 128)
    body = functools.partial(_in_proj_body, n_lat=n_lat, n_b=bsz, tm=tm)
    return pl.pallas_call(
        body, out_shape=S((bsz, t, n), F32), grid=(bsz, t // tm, n // tn),
        in_specs=[pl.BlockSpec((1, tm, d), lambda b, i, j: (b, i, 0)),
                  pl.BlockSpec((1, d), lambda b, i, j: (0, 0)),
                  pl.BlockSpec((MOD_ROWS, d), lambda b, i, j: (0, shift_idx)),
                  pl.BlockSpec((MOD_ROWS, d), lambda b, i, j: (0, scale_idx)),
                  pl.BlockSpec((d, tn), lambda b, i, j: (0, j)),
                  pl.BlockSpec((1, tn), lambda b, i, j: (0, j))],
        out_specs=pl.BlockSpec((1, tm, tn), lambda b, i, j: (b, i, j)),
        scratch_shapes=[pltpu.VMEM((tm, d), BF16)],
        compiler_params=_cp(("parallel", "parallel", "arbitrary")), name=name,
    )(xs, norm_w.reshape(1, d), mod, mod, w, bias.reshape(1, n))


def _out_proj_body(y_ref, w_ref, bias_ref, res_ref, gate_ref, o_ref, *, n_lat, n_b, tm):
    b, i = pl.program_id(0), pl.program_id(1)
    y = jnp.dot(y_ref[0].astype(BF16), w_ref[...], preferred_element_type=F32) + bias_ref[...]
    o_ref[0] = res_ref[0] + _row_gate(gate_ref, b, i * tm, tm, n_lat, n_b) * y


def _out_proj(y, w, bias, res, mod, gate_idx, n_lat, name):
    bsz, t, dy = y.shape
    d = w.shape[1]
    tm = t // 3 if t % 3 == 0 and (t // 3) % 8 == 0 else t
    body = functools.partial(_out_proj_body, n_lat=n_lat, n_b=bsz, tm=tm)
    return pl.pallas_call(
        body, out_shape=S((bsz, t, d), F32), grid=(bsz, t // tm),
        in_specs=[pl.BlockSpec((1, tm, dy), lambda b, i: (b, i, 0)),
                  pl.BlockSpec((dy, d), lambda b, i: (0, 0)),
                  pl.BlockSpec((1, d), lambda b, i: (0, 0)),
                  pl.BlockSpec((1, tm, d), lambda b, i: (b, i, 0)),
                  pl.BlockSpec((MOD_ROWS, d), lambda b, i: (0, gate_idx))],
        out_specs=pl.BlockSpec((1, tm, d), lambda b, i: (b, i, 0)),
        compiler_params=_cp(("parallel", "parallel")), name=name,
    )(y, w, bias.reshape(1, d), res, mod)


def _dft_tables(n):
    k = np.arange(n, dtype=np.int64)
    ang = (np.outer(k, k) % (2 * n)).astype(np.float64) * (math.pi / n)
    c = np.cos(ang)
    s = -np.sin(ang)
    s[0, :] = 1.0 - 2.0 * (k % 2)
    return jnp.asarray(np.stack([c, s]), dtype=BF16)


def _hy_positional(n):
    t = np.linspace(0.0, 1.0, n)[:, None]
    bands = (HY_EMB - 1) // 2
    w = (2.0 * math.pi * np.arange(n) / n)[:, None]
    f = np.linspace(1e-4, bands - 1, bands)[None, :]
    z = np.concatenate([t, np.cos(f * w), -np.sin(f * w)], axis=-1)
    zp = np.zeros((n, LANES), np.float32)
    zp[:, :HY_EMB] = z
    return jnp.asarray(zp), jnp.asarray(t.astype(np.float32))


def _hy_filter_body(z_ref, w1_ref, b1_ref, w2_ref, b2_ref, fr_ref, w3f_ref, w3b_ref, t_ref, dl_ref, cs_ref,
                    p1_ref, p2_ref, pn_ref, *, n):
    h = jnp.sin(fr_ref[0:1, :] * (_mm_f32(z_ref[...], w1_ref[...]) + b1_ref[...]))
    h = jnp.sin(fr_ref[1:2, :] * (_mm_f32(h, w2_ref[...]) + b2_ref[...]))
    win = jnp.exp(-t_ref[...] * dl_ref[...]) + HY_SHIFT
    hf = _mm_f32(h, w3f_ref[...]) * win
    hb = _mm_f32(h, w3b_ref[...]) * win
    row = _iota((n, 1), 0)
    hb = jnp.where(row == 0, 0.0, hb)
    hs, hd = hf + hb, hf - hb
    alt = (1 - 2 * (row & 1)).astype(F32)
    knyq = jnp.sum(alt * hs, axis=0, keepdims=True)
    pn_ref[0] = jnp.broadcast_to(knyq * (0.5 / n), pn_ref.shape[1:])
    hs, hd = hs.astype(BF16), hd.astype(BF16)
    rc = min(n, DFT_ROWS)
    for r0 in range(0, n, rc):
        rows = slice(r0, r0 + rc)
        wgt = jnp.where(row[rows] == 0, 0.5 / n, 1.0 / n)
        p1_ref[0, rows, :] = _mm(cs_ref[0, rows, :], hs) * wgt
        p2_ref[0, rows, :] = jnp.where(row[rows] == 0, 0.0, _mm(cs_ref[1, rows, :], hd) * wgt)


def _hy_filter_spectra(n, w1p, b1, w2, b2, freq, w3, cs, d, td):
    zpad, tcol = _hy_positional(n)
    max_decay = math.log(HY_TARGET) / HY_FAST_DECAY
    min_decay = math.log(HY_TARGET) / HY_SLOW_DECAY
    deltas = jnp.asarray(np.abs(np.linspace(min_decay, max_decay, d)), dtype=F32)[None, :]
    nd = d // td
    ff = w1p.shape[1]
    body = functools.partial(_hy_filter_body, n=n)
    const = lambda o, j: (0, 0)
    return pl.pallas_call(
        body, out_shape=(S((HY_ORDER, n, d), F32), S((HY_ORDER, n, d), F32), S((HY_ORDER, 8, d), F32)),
        grid=(HY_ORDER, nd),
        in_specs=[pl.BlockSpec((n, LANES), const), pl.BlockSpec((LANES, ff), const), pl.BlockSpec((1, ff), const),
                  pl.BlockSpec((ff, ff), const), pl.BlockSpec((1, ff), const), pl.BlockSpec((2, ff), const),
                  pl.BlockSpec((ff, td), lambda o, j: (0, (2 * o) * nd + j)),
                  pl.BlockSpec((ff, td), lambda o, j: (0, (2 * o + 1) * nd + j)),
                  pl.BlockSpec((n, 1), const), pl.BlockSpec((1, td), lambda o, j: (0, j)),
                  pl.BlockSpec((2, n, n), lambda o, j: (0, 0, 0), pipeline_mode=pl.Buffered(1))],
        out_specs=(pl.BlockSpec((1, n, td), lambda o, j: (o, 0, j)),
                   pl.BlockSpec((1, n, td), lambda o, j: (o, 0, j)),
                   pl.BlockSpec((1, 8, td), lambda o, j: (o, 0, j))),
        compiler_params=_cp(("parallel", "parallel")), name=f"hy_filter_{n}",
    )(zpad, w1p, b1.reshape(1, ff), w2, b2.reshape(1, ff), freq, w3, w3, tcol, deltas, cs)


def _short_conv(x, w_ref, b_ref):
    n = x.shape[0]
    row = _iota((n, 1), 0)
    xp = jnp.where(row == 0, 0.0, pltpu.roll(x, 1, 0))
    xn = jnp.where(row == n - 1, 0.0, pltpu.roll(x, n - 1, 0))
    return w_ref[0:1, :] * xp + w_ref[1:2, :] * x + w_ref[2:3, :] * xn + b_ref[...]


def _short_conv_rows(x_ref, w_ref, b_ref, r0, rc, n):
    x = x_ref[0, r0:r0 + rc, :]
    row = _iota((rc, 1), 0)
    prev = x_ref[0, r0 - 1:r0, :] if r0 > 0 else 0.0
    nxt = x_ref[0, r0 + rc:r0 + rc + 1, :] if r0 + rc < n else 0.0
    xp = jnp.where(row == 0, prev, pltpu.roll(x, 1, 0))
    xn = jnp.where(row == rc - 1, nxt, pltpu.roll(x, rc - 1, 0))
    return w_ref[0:1, :] * xp + w_ref[1:2, :] * x + w_ref[2:3, :] * xn + b_ref[...]


def _hy_conv_body(a_ref, g_ref, wa_ref, ba_ref, wg_ref, bg_ref, cs_ref, p1_ref, p2_ref, pn_ref, fb_ref, o_ref,
                  a_s, ab_s, yr_s, yi_s, *, n, conv_a):
    a = a_ref[0]
    if conv_a:
        a = _short_conv(a, wa_ref, ba_ref)
    a_s[...] = a
    ab_s[...] = a.astype(BF16)
    rc = min(n, DFT_ROWS)
    row = _iota((rc, 1), 0)
    alt = (1 - 2 * (row & 1)).astype(F32)
    asum = None
    for r0 in range(0, n, rc):
        rows = slice(r0, r0 + rc)
        re = _mm(cs_ref[0, rows, :], ab_s[...])
        im = _mm(cs_ref[1, rows, :], ab_s[...])
        p1, p2 = p1_ref[0, rows, :], p2_ref[0, rows, :]
        yr_s[rows, :] = (re * p1 - im * p2).astype(BF16)
        yi = re * p2 + im * (jnp.where(row == 0, pn_ref[0, 0:1, :], p1) if r0 == 0 else p1)
        yi_s[rows, :] = yi.astype(BF16)
        part = jnp.sum(alt * yi, axis=0, keepdims=True)
        asum = part if asum is None else asum + part
        if r0 == 0:
            nyq = yi[0:1, :]
    for r0 in range(0, n, rc):
        rows = slice(r0, r0 + rc)
        y = _mm(cs_ref[0, rows, :], yr_s[...]) + _mm(cs_ref[1, rows, :], yi_s[...])
        y = y + alt * nyq
        if r0 == 0:
            y = y - jnp.where(row == 0, asum, 0.0)
        g = _short_conv_rows(g_ref, wg_ref, bg_ref, r0, rc, n)
        o_ref[0, rows, :] = g * (y + a_s[rows, :] * fb_ref[0])


def _hy_conv(a, a_col0, a_row, g, g_col0, g_row, n, conv_w, conv_b, cs, p1, p2, pn, fbias, order, conv_a, d, td, name):
    bsz = a.shape[0]
    nd = d // td
    body = functools.partial(_hy_conv_body, n=n, conv_a=conv_a)
    cw = conv_w
    cb = conv_b.reshape(1, -1)
    return pl.pallas_call(
        body, out_shape=S((bsz, n, d), F32), grid=(nd, bsz),
        in_specs=[pl.BlockSpec((1, n, td), lambda j, b: (b, a_row, a_col0 * nd + j)),
                  pl.BlockSpec((1, n, td), lambda j, b: (b, g_row, g_col0 * nd + j)),
                  pl.BlockSpec((3, td), lambda j, b: (0, a_col0 * nd + j if conv_a else j)),
                  pl.BlockSpec((1, td), lambda j, b: (0, a_col0 * nd + j if conv_a else j)),
                  pl.BlockSpec((3, td), lambda j, b: (0, g_col0 * nd + j)),
                  pl.BlockSpec((1, td), lambda j, b: (0, g_col0 * nd + j)),
                  pl.BlockSpec((2, n, n), lambda j, b: (0, 0, 0), pipeline_mode=pl.Buffered(1)),
                  pl.BlockSpec((1, n, td), lambda j, b: (order, 0, j), pipeline_mode=pl.Buffered(1)),
                  pl.BlockSpec((1, n, td), lambda j, b: (order, 0, j), pipeline_mode=pl.Buffered(1)),
                  pl.BlockSpec((1, 8, td), lambda j, b: (order, 0, j)),
                  pl.BlockSpec((1, 1, td), lambda j, b: (order, 0, j))],
        out_specs=pl.BlockSpec((1, n, td), lambda j, b: (b, 0, j)),
        scratch_shapes=[pltpu.VMEM((n, td), F32), pltpu.VMEM((n, td), BF16), pltpu.VMEM((n, td), BF16),
                        pltpu.VMEM((n, td), BF16)],
        compiler_params=_cp(("parallel", "parallel"), 56), name=name,
    )(a, g, cw, cb, cw, cb, cs, p1, p2, pn, fbias.reshape(HY_ORDER, 1, d))


def _hyena_mixer(u, n_lat, n_ctx, short_w, short_b, w1, b1, w2, b2, w3, freq, fbias):
    d = u.shape[2] // 3
    ff = w1.shape[1]
    w1p = jnp.zeros((LANES, ff), F32).at[:w1.shape[0]].set(w1)
    outs = []
    for n, row in ((n_lat, 0), (n_ctx, n_lat // n_ctx)):
        td = 256 if n > 512 else 512
        cs = _dft_tables(n)
        p1, p2, pn = _hy_filter_spectra(n, w1p, b1, w2, b2, freq, w3, cs, d, td)
        z1 = _hy_conv(u, 0, row, u, 1, row, n, short_w, short_b, cs, p1, p2, pn, fbias, 0, True, d, td, f"hy_conv1_{n}")
        z2 = _hy_conv(z1, 0, 0, u, 2, row, n, short_w, short_b, cs, p1, p2, pn, fbias, 1, False, d, td, f"hy_conv2_{n}")
        outs.append(z2)
    return jnp.concatenate(outs, axis=1)


def _rope_tables(n_lat, width, rot_heads):
    hd = SW_HD
    rows = n_lat // GRID_W
    row = np.repeat(np.arange(rows, dtype=np.float64), GRID_W)
    col = np.tile(np.arange(GRID_W, dtype=np.float64), rows)
    nf = hd // 4
    inv = ROPE_BASE ** (-np.arange(nf, dtype=np.float64) / nf)
    ang = np.concatenate([row[:, None] * inv, col[:, None] * inv], axis=-1)
    cos, sin = np.cos(ang), np.sin(ang)
    zero = np.zeros_like(sin)
    c = np.ones((n_lat, width), np.float32)
    sa = np.zeros((n_lat, width), np.float32)
    sb = np.zeros((n_lat, width), np.float32)
    for h in range(rot_heads):
        c[:, h * hd:(h + 1) * hd] = np.concatenate([cos, cos], axis=-1)
        sa[:, h * hd:(h + 1) * hd] = np.concatenate([-sin, zero], axis=-1)
        sb[:, h * hd:(h + 1) * hd] = np.concatenate([zero, sin], axis=-1)
    return jnp.asarray(c), jnp.asarray(sa), jnp.asarray(sb)


def _rope(x, c, sa, sb):
    w = x.shape[1]
    half = SW_HD // 2
    return x * c + pltpu.roll(x, w - half, 1) * sa + pltpu.roll(x, half, 1) * sb


def _sink_attend(q, kvs, sink):
    ss = []
    m = None
    for k, _, mask in kvs:
        s = _mm_nt(q, k)
        if mask is not None:
            s = jnp.where(mask, s, NEG_INF)
        ss.append(s)
        sm = jnp.max(s, axis=-1, keepdims=True)
        m = sm if m is None else jnp.maximum(m, sm)
    m = jnp.maximum(m, sink)
    den = jnp.exp(sink - m)
    o = None
    for s, (_, v, _) in zip(ss, kvs):
        p = jnp.exp(s - m)
        den = den + jnp.sum(p, axis=-1, keepdims=True)
        pv = _mm(p, v)
        o = pv if o is None else o + pv
    return o / den


def _swa_body(u_ref, cq_ref, saq_ref, sbq_ref, ck_ref, sak_ref, sbk_ref, sink_ref, o_ref, kv_scr,
              *, n_lat, n_ctx, blk):
    g = pl.program_id(1)
    hd, rep = SW_HD, SW_REP
    qw = rep * hd
    scale = hd ** -0.5
    span = 3 * blk
    kv_scr[0:n_lat, :] = _rope(u_ref[0, 0:n_lat, qw:qw + 2 * hd], ck_ref[...], sak_ref[...], sbk_ref[...]).astype(BF16)
    kv_scr[n_lat:, :] = u_ref[0, n_lat:, qw:qw + 2 * hd].astype(BF16)
    kc = kv_scr[n_lat:, 0:hd]
    vc = kv_scr[n_lat:, hd:2 * hd]

    def qblock(i, carry):
        r0 = pl.multiple_of(i * blk, blk)
        rows = pl.ds(r0, blk)
        q = (_rope(u_ref[0, rows, 0:qw], cq_ref[rows, :], saq_ref[rows, :], sbq_ref[rows, :]) * scale).astype(BF16)
        ks = pl.multiple_of(jnp.clip((i - 1) * blk, 0, n_lat - span), blk)
        kl = kv_scr[pl.ds(ks, span), 0:hd]
        vl = kv_scr[pl.ds(ks, span), hd:2 * hd]
        qpos = r0 + _iota((blk, span), 0)
        kpos = ks + _iota((blk, span), 1)
        mask = jnp.abs(qpos - kpos) <= SW_WINDOW
        for r in range(rep):
            o = _sink_attend(q[:, r * hd:(r + 1) * hd], [(kc, vc, None), (kl, vl, mask)], sink_ref[g * rep + r])
            o_ref[0, rows, r * hd:(r + 1) * hd] = o
        return carry

    lax.fori_loop(0, n_lat // blk, qblock, 0)
    qc = (u_ref[0, n_lat:, 0:qw] * scale).astype(BF16)
    for r in range(rep):
        o_ref[0, n_lat:, r * hd:(r + 1) * hd] = _sink_attend(qc[:, r * hd:(r + 1) * hd], [(kc, vc, None)], sink_ref[g * rep + r])


def _swa_mixer(u, n_lat, n_ctx, sink):
    bsz, t, _ = u.shape
    hd, rep = SW_HD, SW_REP
    gw = rep * hd + 2 * hd
    blk = SW_WINDOW
    assert n_lat % blk == 0 and n_lat >= 3 * blk and n_lat % GRID_W == 0
    cq, saq, sbq = _rope_tables(n_lat, rep * hd, rep)
    ck, sak, sbk = _rope_tables(n_lat, 2 * hd, 1)
    body = functools.partial(_swa_body, n_lat=n_lat, n_ctx=n_ctx, blk=blk)
    tab = lambda w: pl.BlockSpec((n_lat, w), lambda b, g: (0, 0))
    return pl.pallas_call(
        body, out_shape=S((bsz, t, SW_HQ * hd), F32), grid=(bsz, SW_HKV),
        in_specs=[pl.BlockSpec((1, t, gw), lambda b, g: (b, 0, g)),
                  tab(rep * hd), tab(rep * hd), tab(rep * hd), tab(2 * hd), tab(2 * hd), tab(2 * hd),
                  pl.BlockSpec(memory_space=pltpu.SMEM)],
        out_specs=pl.BlockSpec((1, t, rep * hd), lambda b, g: (b, 0, g)),
        scratch_shapes=[pltpu.VMEM((t, 2 * hd), BF16)],
        compiler_params=_cp(("parallel", "parallel")), name="swa_attention",
    )(u, cq, saq, sbq, ck, sak, sbk, sink)


def _seq_conv(x, w_ref, n_lat):
    t = x.shape[0]
    row = _iota((t, 1), 0)
    first = (row == 0) | (row == n_lat)
    last = (row == n_lat - 1) | (row == t - 1)
    xp = jnp.where(first, 0.0, pltpu.roll(x, 1, 0))
    xn = jnp.where(last, 0.0, pltpu.roll(x, t - 1, 0))
    return w_ref[0:1, :] * xp + w_ref[1:2, :] * x + w_ref[2:3, :] * xn


def _chunk_scan(x, reverse):
    t = x.shape[0]
    pos = _iota((t, 1), 0) & (CHUNK - 1)
    s = 1
    while s < CHUNK:
        if reverse:
            x = x + jnp.where(pos < CHUNK - s, pltpu.roll(x, t - s, 0), 0.0)
        else:
            x = x + jnp.where(pos >= s, pltpu.roll(x, s, 0), 0.0)
        s *= 2
    return x


def _chunk_order(s, n_lat_chunks, n_ctx_chunks, direction):
    if direction == 0:
        return jnp.where(s < n_ctx_chunks, n_lat_chunks + s, s - n_ctx_chunks)
    return n_lat_chunks + n_ctx_chunks - 1 - s


def _store_diag_blocks(dst, r0, mat):
    for a in range(SUPER // CHUNK):
        dst[pl.ds(r0 + a * CHUNK, CHUNK), 0:CHUNK] = mat[a * CHUNK:(a + 1) * CHUNK, a * CHUNK:(a + 1) * CHUNK].astype(dst.dtype)


def _gated_rms(o, nw, z):
    return o * lax.rsqrt(jnp.mean(o * o, axis=-1, keepdims=True) + EPS) * nw * _silu(z)


def _unit_tri_inverse(a, ii, jj):
    eye = (ii == jj).astype(F32)
    a8 = jnp.where((ii >> 3) == (jj >> 3), a, 0.0)
    a8_2 = _mm(a8, a8)
    a8_4 = _mm(a8_2, a8_2)
    x = _mm(_mm(eye - a8, eye + a8_2), eye + a8_4)
    sh = 3
    while (1 << sh) < CHUNK:
        e = jnp.where(((ii >> (sh + 1)) == (jj >> (sh + 1))) & ((ii >> sh) != (jj >> sh)), a, 0.0)
        x = x - _mm(_mm(x, e), x)
        sh += 1
    return x


def _gdn_body(q_ref, k_ref, v_ref, z_ref, ba_ref, cwq_ref, cwk_ref, cwv_ref, par_ref, nw_ref, o_ref,
              kn_s, qn_s, qe_s, kk_s, egl_s, kb_s, kbe_s, vb_s, gc_s, gt_s, u_s, w_s, at_s, oacc,
              *, n_lat, n_ctx):
    t = n_lat + n_ctx
    h = pl.program_id(1)
    lane = _iota((1, LANES), 1)
    q = _silu(_seq_conv(q_ref[0], cwq_ref, n_lat))
    k = _silu(_seq_conv(k_ref[0], cwk_ref, n_lat))
    v = _silu(_seq_conv(v_ref[0], cwv_ref, n_lat))
    qn = q * lax.rsqrt(jnp.sum(q * q, axis=-1, keepdims=True) + EPS) * (GD_DK ** -0.5)
    kn = k * lax.rsqrt(jnp.sum(k * k, axis=-1, keepdims=True) + EPS)
    kn_s[...] = kn.astype(BF16)
    qn_s[...] = qn.astype(BF16)
    ba = ba_ref[0]
    par = par_ref[...]

    def pick(x, c):
        return jnp.sum(jnp.where(lane == c, x, 0.0), axis=-1, keepdims=True)

    gpack = jnp.zeros((t, LANES), F32)
    for d in range(2):
        beta = _sigmoid(pick(ba, d * GD_H + h))
        a_log = pick(par[d:d + 1, :], h)
        dt_b = pick(par[2 + d:3 + d, :], h)
        g = -jnp.exp(a_log) * _softplus(pick(ba, 2 * GD_H + d * GD_H + h) + dt_b)
        gb = jnp.broadcast_to(g, (t, LANES))
        pre = _chunk_scan(gb, False)
        suf = _chunk_scan(gb, True)
        gc = suf if d else pre
        glast = pre + suf - gb
        gc_s[d] = gc
        gpack = jnp.where(lane == d, gc, gpack)
        eg = jnp.exp(gc)
        kb = kn * beta
        qe_s[d] = (qn * eg).astype(BF16)
        kk_s[d] = (kn * jnp.exp(glast - gc)).astype(BF16)
        egl_s[d] = jnp.exp(glast)
        kb_s[d] = kb.astype(BF16)
        kbe_s[d] = (kb * eg).astype(BF16)
        vb_s[d] = (v * beta).astype(BF16)
    gt_s[...] = gpack.T[0:8, :]
    oacc[...] = jnp.zeros_like(oacc)

    ii = _iota((SUPER, SUPER), 0)
    jj = _iota((SUPER, SUPER), 1)
    same = (ii >> 6) == (jj >> 6)

    def intra(sc, carry):
        r0 = pl.multiple_of(sc * SUPER, SUPER)
        rows = pl.ds(r0, SUPER)
        kc = kn_s[rows, :]
        qc = qn_s[rows, :]
        for d in range(2):
            incl = same & ((jj >= ii) if d else (jj <= ii))
            dif = jnp.broadcast_to(gc_s[d, rows, 0:1], (SUPER, SUPER)) - gt_s[d:d + 1, rows]
            dec = jnp.where(incl, jnp.exp(jnp.where(incl, dif, 0.0)), 0.0)
            a = jnp.where(ii == jj, 0.0, _mm_nt(kb_s[d, rows, :], kc) * dec)
            inv = _unit_tri_inverse(a, ii, jj)
            uw = _mm(inv, jnp.concatenate([vb_s[d, rows, :], kbe_s[d, rows, :]], axis=1))
            u_s[d, rows, :] = uw[:, 0:LANES]
            w_s[d, rows, :] = uw[:, LANES:].astype(BF16)
            _store_diag_blocks(at_s.at[d], r0, _mm_nt(qc, kc) * dec)
        return carry

    lax.fori_loop(0, t // SUPER, intra, 0)

    nl, nc = n_lat // CHUNK, n_ctx // CHUNK

    def step(s, states):
        new = []
        for d in range(2):
            st = states[d]
            c = _chunk_order(s, nl, nc, d)
            rows = pl.ds(pl.multiple_of(c * CHUNK, CHUNK), CHUNK)
            v_new = u_s[d, rows, :] - _mm(w_s[d, rows, :], st)
            o = _mm(qe_s[d, rows, :], st) + _mm(at_s[d, rows, 0:CHUNK], v_new)
            oacc[rows, :] += o
            new.append(st * egl_s[d, rows, :][0:1, :] + _mm_tn(kk_s[d, rows, :], v_new))
        return tuple(new)

    zero = jnp.zeros((GD_DK, LANES), F32)
    lax.fori_loop(0, nl + nc, step, (zero, zero))
    o_ref[0] = _gated_rms(oacc[...], nw_ref[...], z_ref[0])


def _gdn_mixer(u, n_lat, n_ctx, conv_w, a_log, dt_bias, norm_w):
    bsz, t, _ = u.shape
    assert t % SUPER == 0 and n_lat % SUPER == 0
    par = jnp.zeros((8, LANES), F32).at[0:2, :GD_H].set(a_log).at[2:4, :GD_H].set(dt_bias)
    body = functools.partial(_gdn_body, n_lat=n_lat, n_ctx=n_ctx)
    sec = lambda s: pl.BlockSpec((1, t, LANES), lambda b, h: (b, 0, s * GD_H + h))
    cw = lambda s: pl.BlockSpec((3, LANES), lambda b, h: (0, s * GD_H + h))
    both = lambda dt, w=LANES: pltpu.VMEM((2, t, w), dt)
    return pl.pallas_call(
        body, out_shape=S((bsz, t, GD_H * LANES), F32), grid=(bsz, GD_H),
        in_specs=[sec(0), sec(1), sec(2), sec(3),
                  pl.BlockSpec((1, t, LANES), lambda b, h: (b, 0, 4 * GD_H)),
                  cw(0), cw(1), cw(2),
                  pl.BlockSpec((8, LANES), lambda b, h: (0, 0)),
                  pl.BlockSpec((1, LANES), lambda b, h: (0, 0))],
        out_specs=pl.BlockSpec((1, t, LANES), lambda b, h: (b, 0, h)),
        scratch_shapes=[pltpu.VMEM((t, LANES), BF16), pltpu.VMEM((t, LANES), BF16),
                        both(BF16), both(BF16), both(F32), both(BF16), both(BF16), both(BF16), both(F32),
                        pltpu.VMEM((8, t), F32), both(F32), both(BF16), both(BF16), pltpu.VMEM((t, LANES), F32)],
        compiler_params=_cp(("parallel", "parallel"), 56), name="gdn_mixer",
    )(u, u, u, u, u, conv_w, conv_w, conv_w, par, norm_w.reshape(1, LANES))


def _hgrn_body(q_ref, ff_ref, fb_ref, i_ref, g_ref, lbp_ref, nw_ref, o_ref,
               qe_s, ke_s, kk_s, egl_s, v_s, at_s, oacc, *, n_lat, n_ctx, layer):
    t = n_lat + n_ctx
    e = jnp.exp(lbp_ref[...] - jnp.max(lbp_ref[...], axis=0, keepdims=True))
    lb = jnp.sum(e[1:layer + 1, :], axis=0, keepdims=True) / jnp.sum(e, axis=0, keepdims=True)
    q = _silu(q_ref[0])
    v_s[...] = i_ref[0].astype(BF16)
    for d, f_ref in enumerate((ff_ref, fb_ref)):
        sig = _sigmoid(f_ref[0])
        logf = jnp.log(lb + (1.0 - lb) * sig)
        kin = (1.0 - lb) * _sigmoid(-f_ref[0])
        pre = _chunk_scan(logf, False)
        suf = _chunk_scan(logf, True)
        gc = suf if d else pre
        glast = pre + suf - logf
        qe_s[d] = (q * jnp.exp(gc)).astype(BF16)
        ke_s[d] = (kin * jnp.exp(-gc)).astype(BF16)
        kk_s[d] = (kin * jnp.exp(glast - gc)).astype(BF16)
        egl_s[d] = jnp.exp(glast)
    oacc[...] = jnp.zeros_like(oacc)

    ii = _iota((SUPER, SUPER), 0)
    jj = _iota((SUPER, SUPER), 1)
    same = (ii >> 6) == (jj >> 6)

    def intra(sc, carry):
        r0 = pl.multiple_of(sc * SUPER, SUPER)
        rows = pl.ds(r0, SUPER)
        for d in range(2):
            incl = same & ((jj >= ii) if d else (jj <= ii))
            _store_diag_blocks(at_s.at[d], r0, jnp.where(incl, _mm_nt(qe_s[d, rows, :], ke_s[d, rows, :]), 0.0))
        return carry

    lax.fori_loop(0, t // SUPER, intra, 0)
    nl, nc = n_lat // CHUNK, n_ctx // CHUNK

    def step(s, states):
        new = []
        for d in range(2):
            st = states[d]
            c = _chunk_order(s, nl, nc, d)
            rows = pl.ds(pl.multiple_of(c * CHUNK, CHUNK), CHUNK)
            vc = v_s[rows, :]
            oacc[rows, :] += _mm_nt(qe_s[d, rows, :], st) + _mm(at_s[d, rows, 0:CHUNK], vc)
            new.append(st * egl_s[d, rows, :][0:1, :] + _mm_tn(vc, kk_s[d, rows, :]))
        return tuple(new)

    zero = jnp.zeros((LANES, HG_DK), F32)
    lax.fori_loop(0, nl + nc, step, (zero, zero))
    o_ref[0] = _gated_rms(oacc[...], nw_ref[...], g_ref[0])


def _hgrn_mixer(u, n_lat, n_ctx, hg_lb, norm_w, layer):
    bsz, t, n5 = u.shape
    d = n5 // 5
    nh = d // HG_DK
    depth = hg_lb.shape[0]
    assert t % SUPER == 0 and n_lat % SUPER == 0
    body = functools.partial(_hgrn_body, n_lat=n_lat, n_ctx=n_ctx, layer=layer)
    sec = lambda s: pl.BlockSpec((1, t, LANES), lambda b, h: (b, 0, s * nh + h))
    both = lambda dt: pltpu.VMEM((2, t, LANES), dt)
    return pl.pallas_call(
        body, out_shape=S((bsz, t, d), F32), grid=(bsz, nh),
        in_specs=[sec(0), sec(1), sec(2), sec(3), sec(4),
                  pl.BlockSpec((depth, LANES), lambda b, h: (0, h)),
                  pl.BlockSpec((1, LANES), lambda b, h: (0, 0))],
        out_specs=pl.BlockSpec((1, t, LANES), lambda b, h: (b, 0, h)),
        scratch_shapes=[both(BF16), both(BF16), both(BF16), both(F32), pltpu.VMEM((t, LANES), BF16), both(BF16),
                        pltpu.VMEM((t, LANES), F32)],
        compiler_params=_cp(("parallel", "parallel")), name="hgrn2_mixer",
    )(u, u, u, u, u, hg_lb, norm_w.reshape(1, LANES))


def _norm_router_body(x_ref, nw_ref, sh_ref, sc_ref, rw_ref, h_ref, lg_ref, *, n_lat, n_b, tm):
    b, i = pl.program_id(0), pl.program_id(1)
    h = _norm_mod(x_ref[0], nw_ref[...], sh_ref, sc_ref, b, i * tm, n_lat, n_b)
    h_ref[0] = h.astype(BF16)
    lg_ref[0] = _mm_f32(h, rw_ref[...])


def _norm_router(xs, norm_w, mod, router_p, n_lat):
    bsz, t, d = xs.shape
    tm = t // 3 if t % 3 == 0 and (t // 3) % 16 == 0 else t
    body = functools.partial(_norm_router_body, n_lat=n_lat, n_b=bsz, tm=tm)
    return pl.pallas_call(
        body, out_shape=(S((bsz, t, d), BF16), S((bsz, t, LANES), F32)), grid=(bsz, t // tm),
        in_specs=[pl.BlockSpec((1, tm, d), lambda b, i: (b, i, 0)),
                  pl.BlockSpec((1, d), lambda b, i: (0, 0)),
                  pl.BlockSpec((MOD_ROWS, d), lambda b, i: (0, 3)),
                  pl.BlockSpec((MOD_ROWS, d), lambda b, i: (0, 4)),
                  pl.BlockSpec((d, LANES), lambda b, i: (0, 0))],
        out_specs=(pl.BlockSpec((1, tm, d), lambda b, i: (b, i, 0)),
                   pl.BlockSpec((1, tm, LANES), lambda b, i: (b, i, 0))),
        compiler_params=_cp(("parallel", "parallel")), name="moe_norm_router",
    )(xs, norm_w.reshape(1, d), mod, mod, router_p)


def _excl_count_rows(x):
    n = x.shape[0]
    blk = min(n, 256)
    tri = jnp.where(_iota((blk, blk), 0) > _iota((blk, blk), 1), 1.0, 0.0).astype(BF16)
    run = jnp.zeros((1, x.shape[1]), F32)
    outs = []
    for r in range(n // blk):
        xb = x[r * blk:(r + 1) * blk]
        outs.append(jnp.dot(tri, xb.astype(BF16), preferred_element_type=F32) + run)
        run = run + jnp.sum(xb, axis=0, keepdims=True)
    return jnp.concatenate(outs, axis=0) if len(outs) > 1 else outs[0]


def _topcap_slots(v, cap, lane_ok):
    capf = float(cap)

    def count_ge(thr):
        return jnp.sum(jnp.where(v >= thr, 1.0, 0.0), axis=0, keepdims=True)

    def bisect(_, c):
        lo, hi = c
        mid = jnp.sqrt(jnp.maximum(lo, 1e-37)) * jnp.sqrt(hi)
        ok = count_ge(mid) >= capf
        return jnp.where(ok, mid, lo), jnp.where(ok, hi, mid)

    shape = (1, v.shape[1])
    lo, hi = lax.fori_loop(0, 34, bisect, (jnp.zeros(shape, F32), jnp.full(shape, 2.0, F32)))
    thr, found, upper = lo, jnp.zeros(shape, F32), hi
    for _ in range(4):
        m = jnp.max(jnp.where(v < upper, v, -1.0), axis=0, keepdims=True)
        ok = jnp.where(count_ge(m) >= capf, 1.0, 0.0) * (1.0 - found)
        thr = jnp.where(ok > 0, m, thr)
        found = jnp.maximum(found, ok)
        upper = jnp.where(found > 0, upper, m)
    gt = jnp.where(v > thr, 1.0, 0.0)
    eq = jnp.where(v == thr, 1.0, 0.0)
    need = capf - jnp.sum(gt, axis=0, keepdims=True)
    sel = jnp.maximum(gt, eq * jnp.where(_excl_count_rows(eq) < need, 1.0, 0.0)) * lane_ok
    return jnp.where(sel > 0, _excl_count_rows(sel), -1.0)


def _route_body(lg_ref, aff_ref, code_ref, codet_ref, *, n_lat, cap_l, cap_c):
    lane_ok = _iota((1, LANES), 1) < N_EXPERTS
    lg = jnp.where(lane_ok, lg_ref[0], NEG_INF)
    e = jnp.exp(lg - jnp.max(lg, axis=-1, keepdims=True))
    aff = e / jnp.sum(e, axis=-1, keepdims=True)
    aff_ref[0] = aff
    okf = lane_ok.astype(F32)
    code = jnp.concatenate([_topcap_slots(aff[0:n_lat], cap_l, okf), _topcap_slots(aff[n_lat:], cap_c, okf)], axis=0)
    code_ref[0] = code
    codet_ref[0] = code.T


def _route(logits, n_lat, cap_l, cap_c):
    bsz, t, _ = logits.shape
    body = functools.partial(_route_body, n_lat=n_lat, cap_l=cap_l, cap_c=cap_c)
    blk = pl.BlockSpec((1, t, LANES), lambda b: (b, 0, 0))
    return pl.pallas_call(
        body, out_shape=(S((bsz, t, LANES), F32), S((bsz, t, LANES), F32), S((bsz, LANES, t), F32)), grid=(bsz,),
        in_specs=[blk], out_specs=(blk, blk, pl.BlockSpec((1, LANES, t), lambda b: (b, 0, 0))),
        compiler_params=_cp(("parallel",)), name="moe_route",
    )(logits)


def _gather_body(codet_ref, h_ref, xl_ref, xc_ref, *, n_lat, n_ctx, cap_l, cap_c, grp):
    hl = h_ref[0, 0:n_lat, :]
    hc = h_ref[0, n_lat:, :]
    il = _iota((cap_l, n_lat), 0).astype(F32)
    ic = _iota((cap_c, n_ctx), 0).astype(F32)
    for e0 in range(0, N_EXPERTS, grp):
        onehot = jnp.concatenate([jnp.where(il == codet_ref[0, e:e + 1, 0:n_lat], 1.0, 0.0).astype(BF16)
                                  for e in range(e0, e0 + grp)], axis=0)
        xe = jnp.dot(onehot, hl, preferred_element_type=F32).astype(BF16)
        for r in range(grp):
            xl_ref[0, e0 + r] = xe[r * cap_l:(r + 1) * cap_l]
    onehot = jnp.concatenate([jnp.where(ic == codet_ref[0, e:e + 1, n_lat:], 1.0, 0.0).astype(BF16)
                              for e in range(N_EXPERTS)], axis=0)
    xc_ref[0] = jnp.dot(onehot, hc, preferred_element_type=F32).astype(BF16)


def _gather(codet, h, n_lat, cap_l, cap_c):
    bsz, t, d = h.shape
    body = functools.partial(_gather_body, n_lat=n_lat, n_ctx=t - n_lat, cap_l=cap_l, cap_c=cap_c, grp=4)
    return pl.pallas_call(
        body, out_shape=(S((bsz, N_EXPERTS, cap_l, d), BF16), S((bsz, N_EXPERTS * cap_c, d), BF16)), grid=(bsz,),
        in_specs=[pl.BlockSpec((1, LANES, t), lambda b: (b, 0, 0)), pl.BlockSpec((1, t, d), lambda b: (b, 0, 0))],
        out_specs=(pl.BlockSpec((1, N_EXPERTS, cap_l, d), lambda b: (b, 0, 0, 0)),
                   pl.BlockSpec((1, N_EXPERTS * cap_c, d), lambda b: (b, 0, 0))),
        compiler_params=_cp(("parallel",)), name="moe_gather",
    )(codet, h)


def _ffn_body(xl_ref, xc_ref, wg_ref, wu_ref, wd_ref, yl_ref, yc_ref, x_scr, acc, *, nb, cap_l, cap_c):
    f = pl.program_id(1)
    d = x_scr.shape[1]

    @pl.when(f == 0)
    def _():
        x_scr[0:nb * cap_l, :] = xl_ref[:, 0].reshape(nb * cap_l, d)
        x_scr[nb * cap_l:, :] = xc_ref[:, 0].reshape(nb * cap_c, d)
        acc[...] = jnp.zeros_like(acc)

    x = x_scr[...]
    a = jnp.dot(x, wg_ref[0, 0].astype(BF16), preferred_element_type=F32)
    u = jnp.dot(x, wu_ref[0, 0].astype(BF16), preferred_element_type=F32)
    acc[...] += _mm(_silu(a) * u, wd_ref[0, 0])

    @pl.when(f == pl.num_programs(1) - 1)
    def _():
        y = acc[...].astype(BF16)
        yl_ref[:, 0] = y[0:nb * cap_l].reshape(nb, cap_l, d)
        yc_ref[:, 0] = y[nb * cap_l:].reshape(nb, cap_c, d)


def _expert_ffn(xl, xc, w_gate, w_up, w_down, layer):
    bsz, ne, cap_l, d = xl.shape
    cap_c = xc.shape[2]
    ffd = w_gate.shape[-1]
    tf = 256
    rows = bsz * (cap_l + cap_c)
    body = functools.partial(_ffn_body, nb=bsz, cap_l=cap_l, cap_c=cap_c)
    return pl.pallas_call(
        body, out_shape=(S(xl.shape, BF16), S(xc.shape, BF16)), grid=(ne, ffd // tf),
        in_specs=[pl.BlockSpec((bsz, 1, cap_l, d), lambda e, f: (0, e, 0, 0)),
                  pl.BlockSpec((bsz, 1, cap_c, d), lambda e, f: (0, e, 0, 0)),
                  pl.BlockSpec((1, 1, d, tf), lambda e, f: (layer, e, 0, f)),
                  pl.BlockSpec((1, 1, d, tf), lambda e, f: (layer, e, 0, f)),
                  pl.BlockSpec((1, 1, tf, d), lambda e, f: (layer, e, f, 0))],
        out_specs=(pl.BlockSpec((bsz, 1, cap_l, d), lambda e, f: (0, e, 0, 0)),
                   pl.BlockSpec((bsz, 1, cap_c, d), lambda e, f: (0, e, 0, 0))),
        scratch_shapes=[pltpu.VMEM((rows, d), BF16), pltpu.VMEM((rows, d), F32)],
        compiler_params=_cp(("parallel", "arbitrary")), name="moe_expert_ffn",
    )(xl, xc, w_gate, w_up, w_down)


def _combine_body(code_ref, aff_ref, yl_ref, yc_ref, res_ref, gate_ref, o_ref, *, n_lat, n_b, cap_l, cap_c, tm):
    b, i = pl.program_id(0), pl.program_id(1)
    code = code_ref[0]
    aff = aff_ref[0]

    def scatter(cap, y):
        slot = _iota((tm, cap), 1).astype(F32)
        q = jnp.concatenate([jnp.where(code[:, e:e + 1] == slot, aff[:, e:e + 1], 0.0) for e in range(N_EXPERTS)], axis=1)
        return jnp.dot(q.astype(BF16), y, preferred_element_type=F32)

    @pl.when(i * tm < n_lat)
    def _():
        o_ref[0] = res_ref[0] + gate_ref[pl.ds(b, 1), :] * scatter(cap_l, yl_ref[0])

    @pl.when(i * tm >= n_lat)
    def _():
        o_ref[0] = res_ref[0] + gate_ref[n_b:n_b + 1, :] * scatter(cap_c, yc_ref[0])


def _combine(code, aff, yl, yc, res, mod, n_lat):
    bsz, t, d = res.shape
    cap_l, cap_c = yl.shape[2], yc.shape[2]
    tm = t - n_lat
    assert n_lat % tm == 0
    body = functools.partial(_combine_body, n_lat=n_lat, n_b=bsz, cap_l=cap_l, cap_c=cap_c, tm=tm)
    tok = lambda w: pl.BlockSpec((1, tm, w), lambda b, i: (b, i, 0))
    return pl.pallas_call(
        body, out_shape=S((bsz, t, d), F32), grid=(bsz, t // tm),
        in_specs=[tok(LANES), tok(LANES),
                  pl.BlockSpec((1, N_EXPERTS * cap_l, d), lambda b, i: (b, 0, 0)),
                  pl.BlockSpec((1, N_EXPERTS * cap_c, d), lambda b, i: (b, 0, 0)),
                  tok(d), pl.BlockSpec((MOD_ROWS, d), lambda b, i: (0, 5))],
        out_specs=tok(d),
        compiler_params=_cp(("parallel", "parallel")), name="moe_combine",
    )(code, aff, yl.reshape(bsz, N_EXPERTS * cap_l, d), yc.reshape(bsz, N_EXPERTS * cap_c, d), res, mod)


def _moe_layer(xs, norm_w, mod, router_p, w_gate, w_up, w_down, layer, n_lat):
    bsz, t, d = xs.shape
    n_ctx = t - n_lat
    cap_l = EC_CAPACITY * n_lat // N_EXPERTS
    cap_c = EC_CAPACITY * n_ctx // N_EXPERTS
    h, logits = _norm_router(xs, norm_w, mod, router_p, n_lat)
    aff, code, codet = _route(logits, n_lat, cap_l, cap_c)
    xl, xc = _gather(codet, h, n_lat, cap_l, cap_c)
    yl, yc = _expert_ffn(xl, xc.reshape(bsz, N_EXPERTS, cap_c, d), w_gate, w_up, w_down, layer)
    return _combine(code, aff, yl, yc, xs, mod, n_lat)


def _final_norm_body(x_ref, w_ref, o_ref):
    x = x_ref[0]
    o_ref[0] = x * lax.rsqrt(jnp.mean(x * x, axis=-1, keepdims=True) + EPS) * w_ref[...]


def _final_norm(xs, w, n_lat):
    bsz, _, d = xs.shape
    tm = 512 if n_lat % 512 == 0 else n_lat
    return pl.pallas_call(
        _final_norm_body, out_shape=S((bsz, n_lat, d), F32), grid=(bsz, n_lat // tm),
        in_specs=[pl.BlockSpec((1, tm, d), lambda b, i: (b, i, 0)), pl.BlockSpec((1, d), lambda b, i: (0, 0))],
        out_specs=pl.BlockSpec((1, tm, d), lambda b, i: (b, i, 0)),
        compiler_params=_cp(("parallel", "parallel")), name="final_norm",
    )(xs, w.reshape(1, d))


def _swa_group_columns():
    hd, rep = SW_HD, SW_REP
    cols = []
    for g in range(SW_HKV):
        cols += list(range(g * rep * hd, (g + 1) * rep * hd))
        cols += list(range(SW_HQ * hd + g * hd, SW_HQ * hd + (g + 1) * hd))
        cols += list(range((SW_HQ + SW_HKV) * hd + g * hd, (SW_HQ + SW_HKV) * hd + (g + 1) * hd))
    return np.asarray(cols, np.int32)


def kernel(x, c, ctx, c_ctx, ada_w, ada_b, norm1_w, norm2_w, final_norm_w, hy_w_in, hy_b_in, hy_short_w, hy_short_b, hy_ffn_w1, hy_ffn_b1, hy_ffn_w2, hy_ffn_b2, hy_ffn_w3, hy_sin_freq, hy_filter_bias, hy_w_out, hy_b_out, sw_w_in, sw_sink, sw_w_out, gd_w_in, gd_conv_w, gd_a_log, gd_dt_bias, gd_norm_w, gd_w_out, hg_w_in, hg_lb, hg_norm_w, hg_w_out, moe_router, moe_w_gate, moe_w_up, moe_w_down):
    bsz, n_lat, d = x.shape
    n_ctx = ctx.shape[1]
    depth = ada_w.shape[0]
    assert bsz < MOD_ROWS and n_lat % n_ctx == 0
    xs = jnp.concatenate([x, ctx], axis=1)
    c16 = jnp.zeros((MOD_ROWS, d), F32).at[:bsz].set(c).at[bsz].set(c_ctx)
    mod = _modulation(c16, ada_w, ada_b)
    zero_bias = jnp.zeros((d,), F32)
    gd_pad = (-gd_w_in.shape[1]) % LANES
    gd_w = jnp.pad(gd_w_in, ((0, 0), (0, gd_pad))).astype(BF16)
    sw_w = sw_w_in[:, _swa_group_columns()].astype(BF16)
    for layer in range(depth):
        m = mod[layer]
        nw = norm1_w[layer]
        kind = layer % 4
        if kind == 0:
            u = _in_proj(xs, nw, m, 0, 1, hy_w_in.astype(BF16), hy_b_in, n_lat, "hy_in_proj")
            y = _hyena_mixer(u, n_lat, n_ctx, hy_short_w, hy_short_b, hy_ffn_w1, hy_ffn_b1, hy_ffn_w2, hy_ffn_b2,
                             hy_ffn_w3, hy_sin_freq, hy_filter_bias)
            xs = _out_proj(y, hy_w_out.astype(BF16), hy_b_out, xs, m, 2, n_lat, "hy_out_proj")
        elif kind == 1:
            u = _in_proj(xs, nw, m, 0, 1, sw_w, jnp.zeros((sw_w.shape[1],), F32), n_lat, "sw_in_proj")
            y = _swa_mixer(u, n_lat, n_ctx, sw_sink)
            xs = _out_proj(y, sw_w_out.astype(BF16), zero_bias, xs, m, 2, n_lat, "sw_out_proj")
        elif kind == 2:
            u = _in_proj(xs, nw, m, 0, 1, gd_w, jnp.zeros((gd_w.shape[1],), F32), n_lat, "gd_in_proj")
            y = _gdn_mixer(u, n_lat, n_ctx, gd_conv_w, gd_a_log, gd_dt_bias, gd_norm_w)
            xs = _out_proj(y, gd_w_out.astype(BF16), zero_bias, xs, m, 2, n_lat, "gd_out_proj")
        else:
            u = _in_proj(xs, nw, m, 0, 1, hg_w_in.astype(BF16), jnp.zeros((hg_w_in.shape[1],), F32), n_lat, "hg_in_proj")
            y = _hgrn_mixer(u, n_lat, n_ctx, hg_lb, hg_norm_w, layer)
            xs = _out_proj(y, hg_w_out.astype(BF16), zero_bias, xs, m, 2, n_lat, "hg_out_proj")
        router_p = jnp.pad(moe_router[layer], ((0, 0), (0, LANES - N_EXPERTS)))
        xs = _moe_layer(xs, norm2_w[layer], m, router_p, moe_w_gate, moe_w_up, moe_w_down, layer, n_lat)
    return _final_norm(xs, final_norm_w, n_lat)
```

```python
import functools
import math

import jax
import jax.numpy as jnp
import numpy as np
from jax import lax
from jax.experimental import pallas as pl
from jax.experimental.pallas import tpu as pltpu

F32 = jnp.float32
BF16 = jnp.bfloat16
HIGHEST = lax.Precision.HIGHEST
EPS = 1e-6
NEG_INF = -1e30
LANES = 128
BF16_ROWS = 16
MOD_ROWS = 16

GRID_W = 64
HY_ORDER = 2
HY_EMB = 33
HY_FAST_DECAY = 0.3
HY_SLOW_DECAY = 1.5
HY_TARGET = 1e-2
HY_SHIFT = 0.05
SW_HQ, SW_HKV, SW_HD, SW_WINDOW = 16, 4, 64, 128
SW_REP = SW_HQ // SW_HKV
ROPE_BASE = 10000.0
GD_H, GD_DK = 8, 128
HG_DK = 128
CHUNK = 64
SUPER = 256
DFT_ROWS = 512
N_EXPERTS = 16
EC_CAPACITY = 2

S = jax.ShapeDtypeStruct


def _cp(sem, vmem_mb=48):
    return pltpu.CompilerParams(dimension_semantics=sem, vmem_limit_bytes=vmem_mb * 2**20)


def _iota(shape, dim):
    return lax.broadcasted_iota(jnp.int32, shape, dim)


def _sigmoid(x):
    return 1.0 / (1.0 + jnp.exp(-x))


def _silu(x):
    return x * _sigmoid(x)


def _softplus(x):
    return jnp.maximum(x, 0.0) + jnp.log(1.0 + jnp.exp(-jnp.abs(x)))


def _mm(a, b):
    return jnp.dot(a.astype(BF16), b.astype(BF16), preferred_element_type=F32)


def _mm_nt(a, b):
    return lax.dot_general(a.astype(BF16), b.astype(BF16), (((1,), (1,)), ((), ())), preferred_element_type=F32)


def _mm_tn(a, b):
    return lax.dot_general(a.astype(BF16), b.astype(BF16), (((0,), (0,)), ((), ())), preferred_element_type=F32)


def _mm_f32(a, b):
    return jnp.dot(a, b, precision=HIGHEST, preferred_element_type=F32)


def _mod_body(c_ref, w_ref, b_ref, o_ref):
    o_ref[0] = _mm_f32(_silu(c_ref[...]), w_ref[0]) + b_ref[0]


def _modulation(c16, ada_w, ada_b):
    depth, d, n = ada_w.shape
    tn = 1024
    return pl.pallas_call(
        _mod_body, out_shape=S((depth, MOD_ROWS, n), F32), grid=(depth, n // tn),
        in_specs=[pl.BlockSpec((MOD_ROWS, d), lambda l, j: (0, 0)),
                  pl.BlockSpec((1, d, tn), lambda l, j: (l, 0, j)),
                  pl.BlockSpec((1, 1, tn), lambda l, j: (l, 0, j))],
        out_specs=pl.BlockSpec((1, MOD_ROWS, tn), lambda l, j: (l, 0, j)),
        compiler_params=_cp(("parallel", "parallel")), name="adaln_mod",
    )(c16, ada_w, ada_b.reshape(depth, 1, n))


def _norm_mod(x, nw, shift_ref, scale_ref, b, row0, n_lat, n_b):
    tm = x.shape[0]
    y = x * lax.rsqrt(jnp.mean(x * x, axis=-1, keepdims=True) + EPS) * nw
    is_ctx = (row0 + _iota((tm, 1), 0)) >= n_lat
    shift = jnp.where(is_ctx, shift_ref[n_b:n_b + 1, :], shift_ref[pl.ds(b, 1), :])
    scale = jnp.where(is_ctx, scale_ref[n_b:n_b + 1, :], scale_ref[pl.ds(b, 1), :])
    return y * (1.0 + scale) + shift


def _row_gate(gate_ref, b, row0, tm, n_lat, n_b):
    is_ctx = (row0 + _iota((tm, 1), 0)) >= n_lat
    return jnp.where(is_ctx, gate_ref[n_b:n_b + 1, :], gate_ref[pl.ds(b, 1), :])


def _in_proj_body(x_ref, nw_ref, sh_ref, sc_ref, w_ref, bias_ref, o_ref, h_scr, *, n_lat, n_b, tm):
    b, i = pl.program_id(0), pl.program_id(1)

    @pl.when(pl.program_id(2) == 0)
    def _():
        h_scr[...] = _norm_mod(x_ref[0], nw_ref[...], sh_ref, sc_ref, b, i * tm, n_lat, n_b).astype(BF16)

    o_ref[0] = (jnp.dot(h_scr[...], w_ref[...], preferred_element_type=F32) + bias_ref[...]).astype(o_ref.dtype)


def _in_proj(xs, norm_w, mod, shift_idx, scale_idx, w, bias, n_lat, name):
    bsz, t, d = xs.shape
    n = w.shape[1]
    tm = t
    tn = 512 if n % 512 == 0 else (384 if n % 384 == 0 else 128)
    body = functools.partial(_in_proj_body, n_lat=n_lat, n_b=bsz, tm=tm)
    return pl.pallas_call(
        body, out_shape=S((bsz, t, n), BF16), grid=(bsz, t // tm, n // tn),
        in_specs=[pl.BlockSpec((1, tm, d), lambda b, i, j: (b, i, 0)),
                  pl.BlockSpec((1, d), lambda b, i, j: (0, 0)),
                  pl.BlockSpec((MOD_ROWS, d), lambda b, i, j: (0, shift_idx)),
                  pl.BlockSpec((MOD_ROWS, d), lambda b, i, j: (0, scale_idx)),
                  pl.BlockSpec((d, tn), lambda b, i, j: (0, j)),
                  pl.BlockSpec((1, tn), lambda b, i, j: (0, j))],
        out_specs=pl.BlockSpec((1, tm, tn), lambda b, i, j: (b, i, j)),
        scratch_shapes=[pltpu.VMEM((tm, d), BF16)],
        compiler_params=_cp(("parallel", "parallel", "arbitrary")), name=name,
    )(xs, norm_w.reshape(1, d), mod, mod, w, bias.reshape(1, n))


def _out_proj_body(y_ref, w_ref, bias_ref, res_ref, gate_ref, o_ref, *, n_lat, n_b, tm):
    b, i = pl.program_id(0), pl.program_id(1)
    y = jnp.dot(y_ref[0].astype(BF16), w_ref[...], preferred_element_type=F32) + bias_ref[...]
    o_ref[0] = res_ref[0] + _row_gate(gate_ref, b, i * tm, tm, n_lat, n_b) * y


def _out_proj(y, w, bias, res, mod, gate_idx, n_lat, name):
    bsz, t, dy = y.shape
    d = w.shape[1]
    tm = t // 3 if t % 3 == 0 and (t // 3) % 8 == 0 else t
    body = functools.partial(_out_proj_body, n_lat=n_lat, n_b=bsz, tm=tm)
    return pl.pallas_call(
        body, out_shape=S((bsz, t, d), F32), grid=(bsz, t // tm),
        in_specs=[pl.BlockSpec((1, tm, dy), lambda b, i: (b, i, 0)),
                  pl.BlockSpec((dy, d), lambda b, i: (0, 0)),
                  pl.BlockSpec((1, d), lambda b, i: (0, 0)),
                  pl.BlockSpec((1, tm, d), lambda b, i: (b, i, 0)),
                  pl.BlockSpec((MOD_ROWS, d), lambda b, i: (0, gate_idx))],
        out_specs=pl.BlockSpec((1, tm, d), lambda b, i: (b, i, 0)),
        compiler_params=_cp(("parallel", "parallel")), name=name,
    )(y, w, bias.reshape(1, d), res, mod)


def _dft_tables(n):
    k = np.arange(n, dtype=np.int64)
    ang = (np.outer(k, k) % (2 * n)).astype(np.float64) * (math.pi / n)
    c = np.cos(ang)
    s = -np.sin(ang)
    s[0, :] = 1.0 - 2.0 * (k % 2)
    return jnp.asarray(np.stack([c, s]), dtype=BF16)


def _hy_positional(n):
    t = np.linspace(0.0, 1.0, n)[:, None]
    bands = (HY_EMB - 1) // 2
    w = (2.0 * math.pi * np.arange(n) / n)[:, None]
    f = np.linspace(1e-4, bands - 1, bands)[None, :]
    z = np.concatenate([t, np.cos(f * w), -np.sin(f * w)], axis=-1)
    zp = np.zeros((n, LANES), np.float32)
    zp[:, :HY_EMB] = z
    return jnp.asarray(zp), jnp.asarray(t.astype(np.float32))


def _hy_filter_body(z_ref, w1_ref, b1_ref, w2_ref, b2_ref, fr_ref, w3f_ref, w3b_ref, t_ref, dl_ref, cs_ref,
                    p1_ref, p2_ref, pn_ref, *, n):
    h = jnp.sin(fr_ref[0:1, :] * (_mm_f32(z_ref[...], w1_ref[...]) + b1_ref[...]))
    h = jnp.sin(fr_ref[1:2, :] * (_mm_f32(h, w2_ref[...]) + b2_ref[...]))
    win = jnp.exp(-t_ref[...] * dl_ref[...]) + HY_SHIFT
    hf = _mm_f32(h, w3f_ref[...]) * win
    hb = _mm_f32(h, w3b_ref[...]) * win
    row = _iota((n, 1), 0)
    hb = jnp.where(row == 0, 0.0, hb)
    hs, hd = hf + hb, hf - hb
    alt = (1 - 2 * (row & 1)).astype(F32)
    knyq = jnp.sum(alt * hs, axis=0, keepdims=True)
    pn_ref[0] = jnp.broadcast_to(knyq * (0.5 / n), pn_ref.shape[1:])
    hs, hd = hs.astype(BF16), hd.astype(BF16)
    rc = min(n, DFT_ROWS)
    for r0 in range(0, n, rc):
        rows = slice(r0, r0 + rc)
        wgt = jnp.where(row[rows] == 0, 0.5 / n, 1.0 / n)
        p1_ref[0, rows, :] = _mm(cs_ref[0, rows, :], hs) * wgt
        p2_ref[0, rows, :] = jnp.where(row[rows] == 0, 0.0, _mm(cs_ref[1, rows, :], hd) * wgt)


def _hy_filter_spectra(n, w1p, b1, w2, b2, freq, w3, cs, d, td):
    zpad, tcol = _hy_positional(n)
    max_decay = math.log(HY_TARGET) / HY_FAST_DECAY
    min_decay = math.log(HY_TARGET) / HY_SLOW_DECAY
    deltas = jnp.asarray(np.abs(np.linspace(min_decay, max_decay, d)), dtype=F32)[None, :]
    nd = d // td
    ff = w1p.shape[1]
    body = functools.partial(_hy_filter_body, n=n)
    const = lambda o, j: (0, 0)
    return pl.pallas_call(
        body, out_shape=(S((HY_ORDER, n, d), F32), S((HY_ORDER, n, d), F32), S((HY_ORDER, 8, d), F32)),
        grid=(HY_ORDER, nd),
        in_specs=[pl.BlockSpec((n, LANES), const), pl.BlockSpec((LANES, ff), const), pl.BlockSpec((1, ff), const),
                  pl.BlockSpec((ff, ff), const), pl.BlockSpec((1, ff), const), pl.BlockSpec((2, ff), const),
                  pl.BlockSpec((ff, td), lambda o, j: (0, (2 * o) * nd + j)),
                  pl.BlockSpec((ff, td), lambda o, j: (0, (2 * o + 1) * nd + j)),
                  pl.BlockSpec((n, 1), const), pl.BlockSpec((1, td), lambda o, j: (0, j)),
                  pl.BlockSpec((2, n, n), lambda o, j: (0, 0, 0), pipeline_mode=pl.Buffered(1))],
        out_specs=(pl.BlockSpec((1, n, td), lambda o, j: (o, 0, j)),
                   pl.BlockSpec((1, n, td), lambda o, j: (o, 0, j)),
                   pl.BlockSpec((1, 8, td), lambda o, j: (o, 0, j))),
        compiler_params=_cp(("parallel", "parallel")), name=f"hy_filter_{n}",
    )(zpad, w1p, b1.reshape(1, ff), w2, b2.reshape(1, ff), freq, w3, w3, tcol, deltas, cs)


def _short_conv(x, w_ref, b_ref):
    n = x.shape[0]
    row = _iota((n, 1), 0)
    xp = jnp.where(row == 0, 0.0, pltpu.roll(x, 1, 0))
    xn = jnp.where(row == n - 1, 0.0, pltpu.roll(x, n - 1, 0))
    return w_ref[0:1, :] * xp + w_ref[1:2, :] * x + w_ref[2:3, :] * xn + b_ref[...]


def _short_conv_rows(x_ref, w_ref, b_ref, r0, rc, n):
    x = x_ref[0, r0:r0 + rc, :].astype(F32)
    row = _iota((rc, 1), 0)
    pk = BF16_ROWS
    prev = x_ref[0, r0 - pk:r0, :].astype(F32)[pk - 1:pk] if r0 > 0 else 0.0
    nxt = x_ref[0, r0 + rc:r0 + rc + pk, :].astype(F32)[0:1] if r0 + rc < n else 0.0
    xp = jnp.where(row == 0, prev, pltpu.roll(x, 1, 0))
    xn = jnp.where(row == rc - 1, nxt, pltpu.roll(x, rc - 1, 0))
    return w_ref[0:1, :] * xp + w_ref[1:2, :] * x + w_ref[2:3, :] * xn + b_ref[...]


def _hy_conv_body(a_ref, g_ref, wa_ref, ba_ref, wg_ref, bg_ref, cs_ref, p1_ref, p2_ref, pn_ref, fb_ref, o_ref,
                  a_s, ab_s, yr_s, yi_s, *, n, conv_a):
    a = a_ref[0].astype(F32)
    if conv_a:
        a = _short_conv(a, wa_ref, ba_ref)
    a_s[...] = a
    ab_s[...] = a.astype(BF16)
    rc = min(n, DFT_ROWS)
    row = _iota((rc, 1), 0)
    alt = (1 - 2 * (row & 1)).astype(F32)
    asum = None
    for r0 in range(0, n, rc):
        rows = slice(r0, r0 + rc)
        re = _mm(cs_ref[0, rows, :], ab_s[...])
        im = _mm(cs_ref[1, rows, :], ab_s[...])
        p1, p2 = p1_ref[0, rows, :], p2_ref[0, rows, :]
        yr_s[rows, :] = (re * p1 - im * p2).astype(BF16)
        yi = re * p2 + im * (jnp.where(row == 0, pn_ref[0, 0:1, :], p1) if r0 == 0 else p1)
        yi_s[rows, :] = yi.astype(BF16)
        part = jnp.sum(alt * yi, axis=0, keepdims=True)
        asum = part if asum is None else asum + part
        if r0 == 0:
            nyq = yi[0:1, :]
    for r0 in range(0, n, rc):
        rows = slice(r0, r0 + rc)
        y = _mm(cs_ref[0, rows, :], yr_s[...]) + _mm(cs_ref[1, rows, :], yi_s[...])
        y = y + alt * nyq
        if r0 == 0:
            y = y - jnp.where(row == 0, asum, 0.0)
        g = _short_conv_rows(g_ref, wg_ref, bg_ref, r0, rc, n)
        o_ref[0, rows, :] = g * (y + a_s[rows, :] * fb_ref[0])


def _hy_conv(a, a_col0, a_row, g, g_col0, g_row, n, conv_w, conv_b, cs, p1, p2, pn, fbias, order, conv_a, d, td, name):
    bsz = a.shape[0]
    nd = d // td
    body = functools.partial(_hy_conv_body, n=n, conv_a=conv_a)
    cw = conv_w
    cb = conv_b.reshape(1, -1)
    return pl.pallas_call(
        body, out_shape=S((bsz, n, d), F32), grid=(nd, bsz),
        in_specs=[pl.BlockSpec((1, n, td), lambda j, b: (b, a_row, a_col0 * nd + j)),
                  pl.BlockSpec((1, n, td), lambda j, b: (b, g_row, g_col0 * nd + j)),
                  pl.BlockSpec((3, td), lambda j, b: (0, a_col0 * nd + j if conv_a else j)),
                  pl.BlockSpec((1, td), lambda j, b: (0, a_col0 * nd + j if conv_a else j)),
                  pl.BlockSpec((3, td), lambda j, b: (0, g_col0 * nd + j)),
                  pl.BlockSpec((1, td), lambda j, b: (0, g_col0 * nd + j)),
                  pl.BlockSpec((2, n, n), lambda j, b: (0, 0, 0), pipeline_mode=pl.Buffered(1)),
                  pl.BlockSpec((1, n, td), lambda j, b: (order, 0, j), pipeline_mode=pl.Buffered(1)),
                  pl.BlockSpec((1, n, td), lambda j, b: (order, 0, j), pipeline_mode=pl.Buffered(1)),
                  pl.BlockSpec((1, 8, td), lambda j, b: (order, 0, j)),
                  pl.BlockSpec((1, 1, td), lambda j, b: (order, 0, j))],
        out_specs=pl.BlockSpec((1, n, td), lambda j, b: (b, 0, j)),
        scratch_shapes=[pltpu.VMEM((n, td), F32), pltpu.VMEM((n, td), BF16), pltpu.VMEM((n, td), BF16),
                        pltpu.VMEM((n, td), BF16)],
        compiler_params=_cp(("parallel", "parallel"), 56), name=name,
    )(a, g, cw, cb, cw, cb, cs, p1, p2, pn, fbias.reshape(HY_ORDER, 1, d))


def _hyena_mixer(u, n_lat, n_ctx, short_w, short_b, w1, b1, w2, b2, w3, freq, fbias):
    d = u.shape[2] // 3
    ff = w1.shape[1]
    w1p = jnp.zeros((LANES, ff), F32).at[:w1.shape[0]].set(w1)
    outs = []
    for n, row in ((n_lat, 0), (n_ctx, n_lat // n_ctx)):
        td = 256 if n > 512 else 512
        cs = _dft_tables(n)
        p1, p2, pn = _hy_filter_spectra(n, w1p, b1, w2, b2, freq, w3, cs, d, td)
        z1 = _hy_conv(u, 0, row, u, 1, row, n, short_w, short_b, cs, p1, p2, pn, fbias, 0, True, d, td, f"hy_conv1_{n}")
        z2 = _hy_conv(z1, 0, 0, u, 2, row, n, short_w, short_b, cs, p1, p2, pn, fbias, 1, False, d, td, f"hy_conv2_{n}")
        outs.append(z2)
    return jnp.concatenate(outs, axis=1)


def _rope_tables(n_lat, width, rot_heads):
    hd = SW_HD
    rows = n_lat // GRID_W
    row = np.repeat(np.arange(rows, dtype=np.float64), GRID_W)
    col = np.tile(np.arange(GRID_W, dtype=np.float64), rows)
    nf = hd // 4
    inv = ROPE_BASE ** (-np.arange(nf, dtype=np.float64) / nf)
    ang = np.concatenate([row[:, None] * inv, col[:, None] * inv], axis=-1)
    cos, sin = np.cos(ang), np.sin(ang)
    zero = np.zeros_like(sin)
    c = np.ones((n_lat, width), np.float32)
    sa = np.zeros((n_lat, width), np.float32)
    sb = np.zeros((n_lat, width), np.float32)
    for h in range(rot_heads):
        c[:, h * hd:(h + 1) * hd] = np.concatenate([cos, cos], axis=-1)
        sa[:, h * hd:(h + 1) * hd] = np.concatenate([-sin, zero], axis=-1)
        sb[:, h * hd:(h + 1) * hd] = np.concatenate([zero, sin], axis=-1)
    return jnp.asarray(c), jnp.asarray(sa), jnp.asarray(sb)


def _rope(x, c, sa, sb):
    w = x.shape[1]
    half = SW_HD // 2
    return x * c + pltpu.roll(x, w - half, 1) * sa + pltpu.roll(x, half, 1) * sb


def _sink_attend(q, kvs, sink):
    ss = []
    m = None
    for k, _, mask in kvs:
        s = _mm_nt(q, k)
        if mask is not None:
            s = jnp.where(mask, s, NEG_INF)
        ss.append(s)
        sm = jnp.max(s, axis=-1, keepdims=True)
        m = sm if m is None else jnp.maximum(m, sm)
    m = jnp.maximum(m, sink)
    den = jnp.exp(sink - m)
    o = None
    for s, (_, v, _) in zip(ss, kvs):
        p = jnp.exp(s - m)
        den = den + jnp.sum(p, axis=-1, keepdims=True)
        pv = _mm(p, v)
        o = pv if o is None else o + pv
    return o / den


def _swa_body(u_ref, cq_ref, saq_ref, sbq_ref, ck_ref, sak_ref, sbk_ref, sink_ref, o_ref, kv_scr,
              *, n_lat, n_ctx, blk):
    g = pl.program_id(1)
    hd, rep = SW_HD, SW_REP
    qw = rep * hd
    scale = hd ** -0.5
    span = 3 * blk
    kv_scr[0:n_lat, :] = _rope(u_ref[0, 0:n_lat, qw:qw + 2 * hd].astype(F32), ck_ref[...], sak_ref[...],
                               sbk_ref[...]).astype(BF16)
    kv_scr[n_lat:, :] = u_ref[0, n_lat:, qw:qw + 2 * hd]
    kc = kv_scr[n_lat:, 0:hd]
    vc = kv_scr[n_lat:, hd:2 * hd]

    def qblock(i, carry):
        r0 = pl.multiple_of(i * blk, blk)
        rows = pl.ds(r0, blk)
        q = _rope(u_ref[0, rows, 0:qw].astype(F32), cq_ref[rows, :], saq_ref[rows, :], sbq_ref[rows, :]) * scale
        q = q.astype(BF16)
        ks = pl.multiple_of(jnp.clip((i - 1) * blk, 0, n_lat - span), blk)
        kl = kv_scr[pl.ds(ks, span), 0:hd]
        vl = kv_scr[pl.ds(ks, span), hd:2 * hd]
        qpos = r0 + _iota((blk, span), 0)
        kpos = ks + _iota((blk, span), 1)
        mask = jnp.abs(qpos - kpos) <= SW_WINDOW
        for r in range(rep):
            o = _sink_attend(q[:, r * hd:(r + 1) * hd], [(kc, vc, None), (kl, vl, mask)], sink_ref[g * rep + r])
            o_ref[0, rows, r * hd:(r + 1) * hd] = o
        return carry

    lax.fori_loop(0, n_lat // blk, qblock, 0)
    qc = (u_ref[0, n_lat:, 0:qw].astype(F32) * scale).astype(BF16)
    for r in range(rep):
        o_ref[0, n_lat:, r * hd:(r + 1) * hd] = _sink_attend(qc[:, r * hd:(r + 1) * hd], [(kc, vc, None)], sink_ref[g * rep + r])


def _swa_mixer(u, n_lat, n_ctx, sink):
    bsz, t, _ = u.shape
    hd, rep = SW_HD, SW_REP
    gw = rep * hd + 2 * hd
    blk = SW_WINDOW
    assert n_lat % blk == 0 and n_lat >= 3 * blk and n_lat % GRID_W == 0
    cq, saq, sbq = _rope_tables(n_lat, rep * hd, rep)
    ck, sak, sbk = _rope_tables(n_lat, 2 * hd, 1)
    body = functools.partial(_swa_body, n_lat=n_lat, n_ctx=n_ctx, blk=blk)
    tab = lambda w: pl.BlockSpec((n_lat, w), lambda b, g: (0, 0))
    return pl.pallas_call(
        body, out_shape=S((bsz, t, SW_HQ * hd), F32), grid=(bsz, SW_HKV),
        in_specs=[pl.BlockSpec((1, t, gw), lambda b, g: (b, 0, g)),
                  tab(rep * hd), tab(rep * hd), tab(rep * hd), tab(2 * hd), tab(2 * hd), tab(2 * hd),
                  pl.BlockSpec(memory_space=pltpu.SMEM)],
        out_specs=pl.BlockSpec((1, t, rep * hd), lambda b, g: (b, 0, g)),
        scratch_shapes=[pltpu.VMEM((t, 2 * hd), BF16)],
        compiler_params=_cp(("parallel", "parallel")), name="swa_attention",
    )(u, cq, saq, sbq, ck, sak, sbk, sink)


def _seq_conv(x, w_ref, n_lat):
    t = x.shape[0]
    row = _iota((t, 1), 0)
    first = (row == 0) | (row == n_lat)
    last = (row == n_lat - 1) | (row == t - 1)
    xp = jnp.where(first, 0.0, pltpu.roll(x, 1, 0))
    xn = jnp.where(last, 0.0, pltpu.roll(x, t - 1, 0))
    return w_ref[0:1, :] * xp + w_ref[1:2, :] * x + w_ref[2:3, :] * xn


def _chunk_scan(x, reverse):
    t = x.shape[0]
    pos = _iota((t, 1), 0) & (CHUNK - 1)
    s = 1
    while s < CHUNK:
        if reverse:
            x = x + jnp.where(pos < CHUNK - s, pltpu.roll(x, t - s, 0), 0.0)
        else:
            x = x + jnp.where(pos >= s, pltpu.roll(x, s, 0), 0.0)
        s *= 2
    return x


def _chunk_order(s, n_lat_chunks, n_ctx_chunks, direction):
    if direction == 0:
        return jnp.where(s < n_ctx_chunks, n_lat_chunks + s, s - n_ctx_chunks)
    return n_lat_chunks + n_ctx_chunks - 1 - s


def _intra_unroll(n_super):
    return next(u for u in (3, 2, 1) if n_super % u == 0)


def _gated_rms(o, nw, z):
    return o * lax.rsqrt(jnp.mean(o * o, axis=-1, keepdims=True) + EPS) * nw * _silu(z)


def _bmm(a, b):
    return jnp.einsum('bij,bjk->bik', a.astype(BF16), b.astype(BF16), preferred_element_type=F32)


def _bmm_nt(a, b):
    return jnp.einsum('bik,bjk->bij', a.astype(BF16), b.astype(BF16), preferred_element_type=F32)


def _unit_tri_inverse(a, ii, jj):
    eye = (ii == jj).astype(F32)
    a8 = jnp.where((ii >> 3) == (jj >> 3), a, 0.0)
    a8_2 = _bmm(a8, a8)
    a8_4 = _bmm(a8_2, a8_2)
    x = _bmm(_bmm(eye - a8, eye + a8_2), eye + a8_4)
    sh = 3
    while (1 << sh) < CHUNK:
        e = jnp.where(((ii >> (sh + 1)) == (jj >> (sh + 1))) & ((ii >> sh) != (jj >> sh)), a, 0.0)
        x = x - _bmm(_bmm(x, e), x)
        sh += 1
    return x


def _gdn_body(q_ref, k_ref, v_ref, z_ref, ba_ref, cwq_ref, cwk_ref, cwv_ref, par_ref, nw_ref, o_ref,
              kn_s, qn_s, qe_s, kk_s, egl_s, kb_s, kbe_s, vb_s, gc_s, gt_s, qp_s, k2_s, n_s, oacc,
              *, n_lat, n_ctx):
    t = n_lat + n_ctx
    h = pl.program_id(1)
    lane = _iota((1, LANES), 1)
    q = _silu(_seq_conv(q_ref[0].astype(F32), cwq_ref, n_lat))
    k = _silu(_seq_conv(k_ref[0].astype(F32), cwk_ref, n_lat))
    v = _silu(_seq_conv(v_ref[0].astype(F32), cwv_ref, n_lat))
    qn = q * lax.rsqrt(jnp.sum(q * q, axis=-1, keepdims=True) + EPS) * (GD_DK ** -0.5)
    kn = k * lax.rsqrt(jnp.sum(k * k, axis=-1, keepdims=True) + EPS)
    kn_s[...] = kn.astype(BF16)
    qn_s[...] = qn.astype(BF16)
    ba = ba_ref[0].astype(F32)
    par = par_ref[...]

    def pick(x, c):
        return jnp.sum(jnp.where(lane == c, x, 0.0), axis=-1, keepdims=True)

    gpack = jnp.zeros((t, LANES), F32)
    for d in range(2):
        beta = _sigmoid(pick(ba, d * GD_H + h))
        a_log = pick(par[d:d + 1, :], h)
        dt_b = pick(par[2 + d:3 + d, :], h)
        g = -jnp.exp(a_log) * _softplus(pick(ba, 2 * GD_H + d * GD_H + h) + dt_b)
        gb = jnp.broadcast_to(g, (t, LANES))
        pre = _chunk_scan(gb, False)
        suf = _chunk_scan(gb, True)
        gc = suf if d else pre
        glast = pre + suf - gb
        gc_s[d] = gc
        gpack = jnp.where(lane == d, gc, gpack)
        eg = jnp.exp(gc)
        kb = kn * beta
        qe_s[d] = (qn * eg).astype(BF16)
        kk_s[d] = (kn * jnp.exp(glast - gc)).astype(BF16)
        egl_s[d] = jnp.exp(glast)
        kb_s[d] = kb.astype(BF16)
        kbe_s[d] = (kb * eg).astype(BF16)
        vb_s[d] = (v * beta).astype(BF16)
    gt_s[...] = gpack.T[0:8, :]

    ii = _iota((SUPER, SUPER), 0)
    jj = _iota((SUPER, SUPER), 1)
    same = (ii >> 6) == (jj >> 6)
    per = SUPER // CHUNK
    unroll = _intra_unroll(t // SUPER)

    def intra(it, carry):
        r0s = [pl.multiple_of((it * unroll + kq) * SUPER, SUPER) for kq in range(unroll)]
        kcs = [kn_s[pl.ds(r0, SUPER), :] for r0 in r0s]
        pairs = [(kq, d) for kq in range(unroll) for d in range(2)]
        decs = []
        for kq, d in pairs:
            rows = pl.ds(r0s[kq], SUPER)
            incl = same & ((jj >= ii) if d else (jj <= ii))
            dif = jnp.broadcast_to(gc_s[d, rows, 0:1], (SUPER, SUPER)) - gt_s[d:d + 1, rows]
            decs.append(jnp.where(incl, jnp.exp(jnp.where(incl, dif, 0.0)), 0.0))
        dec = jnp.stack(decs)
        kc2 = jnp.stack([kcs[kq] for kq, _ in pairs])
        kb = jnp.stack([kb_s[d, pl.ds(r0s[kq], SUPER), :] for kq, d in pairs])
        rhs = jnp.stack([jnp.concatenate([vb_s[d, pl.ds(r0s[kq], SUPER), :], kbe_s[d, pl.ds(r0s[kq], SUPER), :]], axis=1)
                         for kq, d in pairs])
        a = jnp.where(ii == jj, 0.0, _bmm_nt(kb, kc2) * dec)
        uw = _bmm(_unit_tri_inverse(a, ii, jj), rhs).astype(BF16)
        qk = _bmm_nt(jnp.stack([qn_s[pl.ds(r0, SUPER), :] for r0 in r0s]), jnp.stack(kcs))
        auw = _bmm(jnp.stack([qk[kq] for kq, _ in pairs]) * dec, uw)
        for i, (kq, d) in enumerate(pairs):
            rows = pl.ds(r0s[kq], SUPER)
            qp_s[d, rows, :] = (qe_s[d, rows, :].astype(F32) - auw[i, :, LANES:]).astype(BF16)
            if d == 1:
                oacc[rows, :] = auw[i - 1, :, 0:LANES] + auw[i, :, 0:LANES]
            for c4 in range(per):
                kuw = _mm_tn(kk_s[d, pl.ds(r0s[kq] + c4 * CHUNK, CHUNK), :], uw[i, c4 * CHUNK:(c4 + 1) * CHUNK])
                crow = pl.ds(pl.multiple_of(((it * unroll + kq) * per + c4) * GD_DK, GD_DK), GD_DK)
                n_s[d, crow, :] = kuw[:, 0:LANES]
                k2_s[d, crow, :] = kuw[:, LANES:].astype(BF16)
        return carry

    lax.fori_loop(0, t // SUPER // unroll, intra, 0)

    nl, nc = n_lat // CHUNK, n_ctx // CHUNK

    def step(s, states):
        new = []
        for d in range(2):
            st = states[d]
            sb = st.astype(BF16)
            c = _chunk_order(s, nl, nc, d)
            rows = pl.ds(pl.multiple_of(c * CHUNK, CHUNK), CHUNK)
            crow = pl.ds(pl.multiple_of(c * GD_DK, GD_DK), GD_DK)
            oacc[rows, :] += jnp.dot(qp_s[d, rows, :], sb, preferred_element_type=F32)
            new.append(st * egl_s[d, pl.ds(c * CHUNK, 1), :] + n_s[d, crow, :]
                       - jnp.dot(k2_s[d, crow, :], sb, preferred_element_type=F32))
        return tuple(new)

    zero = jnp.zeros((GD_DK, LANES), F32)
    lax.fori_loop(0, nl + nc, step, (zero, zero))
    o_ref[0] = _gated_rms(oacc[...], nw_ref[...], z_ref[0].astype(F32))


def _gdn_mixer(u, n_lat, n_ctx, conv_w, a_log, dt_bias, norm_w):
    bsz, t, _ = u.shape
    assert t % SUPER == 0 and n_lat % SUPER == 0
    par = jnp.zeros((8, LANES), F32).at[0:2, :GD_H].set(a_log).at[2:4, :GD_H].set(dt_bias)
    body = functools.partial(_gdn_body, n_lat=n_lat, n_ctx=n_ctx)
    sec = lambda s: pl.BlockSpec((1, t, LANES), lambda b, h: (b, 0, s * GD_H + h))
    cw = lambda s: pl.BlockSpec((3, LANES), lambda b, h: (0, s * GD_H + h))
    both = lambda dt: pltpu.VMEM((2, t, LANES), dt)
    state_rows = (t // CHUNK) * GD_DK
    return pl.pallas_call(
        body, out_shape=S((bsz, t, GD_H * LANES), F32), grid=(bsz, GD_H),
        in_specs=[sec(0), sec(1), sec(2), sec(3),
                  pl.BlockSpec((1, t, LANES), lambda b, h: (b, 0, 4 * GD_H)),
                  cw(0), cw(1), cw(2),
                  pl.BlockSpec((8, LANES), lambda b, h: (0, 0)),
                  pl.BlockSpec((1, LANES), lambda b, h: (0, 0))],
        out_specs=pl.BlockSpec((1, t, LANES), lambda b, h: (b, 0, h)),
        scratch_shapes=[pltpu.VMEM((t, LANES), BF16), pltpu.VMEM((t, LANES), BF16),
                        both(BF16), both(BF16), both(F32), both(BF16), both(BF16), both(BF16), both(F32),
                        pltpu.VMEM((8, t), F32), both(BF16), pltpu.VMEM((2, state_rows, LANES), BF16),
                        pltpu.VMEM((2, state_rows, LANES), F32), pltpu.VMEM((t, LANES), F32)],
        compiler_params=_cp(("parallel", "parallel"), 56), name="gdn_mixer",
    )(u, u, u, u, u, conv_w, conv_w, conv_w, par, norm_w.reshape(1, LANES))


def _hgrn_body(q_ref, ff_ref, fb_ref, i_ref, g_ref, lbp_ref, nw_ref, o_ref,
               qe_s, ke_s, kk_s, egl_s, v_s, n_s, st_s, oacc, *, n_lat, n_ctx, layer):
    t = n_lat + n_ctx
    e = jnp.exp(lbp_ref[...] - jnp.max(lbp_ref[...], axis=0, keepdims=True))
    lb = jnp.sum(e[1:layer + 1, :], axis=0, keepdims=True) / jnp.sum(e, axis=0, keepdims=True)
    q = _silu(q_ref[0].astype(F32))
    v_s[...] = i_ref[0]
    for d, f_ref in enumerate((ff_ref, fb_ref)):
        f = f_ref[0].astype(F32)
        sig = _sigmoid(f)
        logf = jnp.log(lb + (1.0 - lb) * sig)
        kin = (1.0 - lb) * _sigmoid(-f)
        pre = _chunk_scan(logf, False)
        suf = _chunk_scan(logf, True)
        gc = suf if d else pre
        glast = pre + suf - logf
        qe_s[d] = (q * jnp.exp(gc)).astype(BF16)
        ke_s[d] = (kin * jnp.exp(-gc)).astype(BF16)
        kk_s[d] = (kin * jnp.exp(glast - gc)).astype(BF16)
        egl_s[d] = jnp.exp(glast)

    ii = _iota((SUPER, SUPER), 0)
    jj = _iota((SUPER, SUPER), 1)
    same = (ii >> 6) == (jj >> 6)
    per = SUPER // CHUNK
    unroll = _intra_unroll(t // SUPER)

    def chunk_rows(sc, c4):
        rows = pl.ds(pl.multiple_of(sc * SUPER + c4 * CHUNK, CHUNK), CHUNK)
        crow = pl.ds(pl.multiple_of((sc * per + c4) * LANES, LANES), LANES)
        return rows, crow

    def intra(it, carry):
        pairs = [(kq, d) for kq in range(unroll) for d in range(2)]
        rows = [pl.ds(pl.multiple_of((it * unroll + kq) * SUPER, SUPER), SUPER) for kq in range(unroll)]
        incl = jnp.stack([same & ((jj >= ii) if d else (jj <= ii)) for _, d in pairs])
        at = jnp.where(incl, _bmm_nt(jnp.stack([qe_s[d, rows[kq], :] for kq, d in pairs]),
                                     jnp.stack([ke_s[d, rows[kq], :] for kq, d in pairs])), 0.0)
        part = _bmm(at, jnp.stack([v_s[rows[kq], :] for kq, _ in pairs]))
        for kq in range(unroll):
            oacc[rows[kq], :] = part[2 * kq] + part[2 * kq + 1]
            for d in range(2):
                for c4 in range(per):
                    r64, crow = chunk_rows(it * unroll + kq, c4)
                    n_s[d, crow, :] = _mm_tn(v_s[r64, :], kk_s[d, r64, :])
        return carry

    lax.fori_loop(0, t // SUPER // unroll, intra, 0)
    nl, nc = n_lat // CHUNK, n_ctx // CHUNK

    def scan(s, states):
        new = []
        for d in range(2):
            c = _chunk_order(s, nl, nc, d)
            crow = pl.ds(pl.multiple_of(c * LANES, LANES), LANES)
            st_s[d, crow, :] = states[d].astype(BF16)
            new.append(states[d] * egl_s[d, pl.ds(c * CHUNK, 1), :] + n_s[d, crow, :])
        return tuple(new)

    zero = jnp.zeros((LANES, HG_DK), F32)
    lax.fori_loop(0, nl + nc, scan, (zero, zero))

    def inter(it, carry):
        for kq in range(unroll):
            sc = it * unroll + kq
            for c4 in range(per):
                r64, crow = chunk_rows(sc, c4)
                oacc[r64, :] += (_mm_nt(qe_s[0, r64, :], st_s[0, crow, :]) + _mm_nt(qe_s[1, r64, :], st_s[1, crow, :]))
        return carry

    lax.fori_loop(0, t // SUPER // unroll, inter, 0)
    o_ref[0] = _gated_rms(oacc[...], nw_ref[...], g_ref[0].astype(F32))


def _hgrn_mixer(u, n_lat, n_ctx, hg_lb, norm_w, layer):
    bsz, t, n5 = u.shape
    d = n5 // 5
    nh = d // HG_DK
    depth = hg_lb.shape[0]
    assert t % SUPER == 0 and n_lat % SUPER == 0
    body = functools.partial(_hgrn_body, n_lat=n_lat, n_ctx=n_ctx, layer=layer)
    sec = lambda s: pl.BlockSpec((1, t, LANES), lambda b, h: (b, 0, s * nh + h))
    both = lambda dt: pltpu.VMEM((2, t, LANES), dt)
    state_rows = (t // CHUNK) * LANES
    return pl.pallas_call(
        body, out_shape=S((bsz, t, d), F32), grid=(bsz, nh),
        in_specs=[sec(0), sec(1), sec(2), sec(3), sec(4),
                  pl.BlockSpec((depth, LANES), lambda b, h: (0, h)),
                  pl.BlockSpec((1, LANES), lambda b, h: (0, 0))],
        out_specs=pl.BlockSpec((1, t, LANES), lambda b, h: (b, 0, h)),
        scratch_shapes=[both(BF16), both(BF16), both(BF16), both(F32), pltpu.VMEM((t, LANES), BF16),
                        pltpu.VMEM((2, state_rows, LANES), F32), pltpu.VMEM((2, state_rows, LANES), BF16),
                        pltpu.VMEM((t, LANES), F32)],
        compiler_params=_cp(("parallel", "parallel")), name="hgrn2_mixer",
    )(u, u, u, u, u, hg_lb, norm_w.reshape(1, LANES))


def _norm_router_body(x_ref, nw_ref, sh_ref, sc_ref, rw_ref, h_ref, lg_ref, *, n_lat, n_b, tm):
    b, i = pl.program_id(0), pl.program_id(1)
    h = _norm_mod(x_ref[0], nw_ref[...], sh_ref, sc_ref, b, i * tm, n_lat, n_b)
    h_ref[0] = h.astype(BF16)
    lg_ref[0] = _mm_f32(h, rw_ref[...])


def _norm_router(xs, norm_w, mod, router_p, n_lat):
    bsz, t, d = xs.shape
    tm = t // 3 if t % 3 == 0 and (t // 3) % 16 == 0 else t
    body = functools.partial(_norm_router_body, n_lat=n_lat, n_b=bsz, tm=tm)
    return pl.pallas_call(
        body, out_shape=(S((bsz, t, d), BF16), S((bsz, t, LANES), F32)), grid=(bsz, t // tm),
        in_specs=[pl.BlockSpec((1, tm, d), lambda b, i: (b, i, 0)),
                  pl.BlockSpec((1, d), lambda b, i: (0, 0)),
                  pl.BlockSpec((MOD_ROWS, d), lambda b, i: (0, 3)),
                  pl.BlockSpec((MOD_ROWS, d), lambda b, i: (0, 4)),
                  pl.BlockSpec((d, LANES), lambda b, i: (0, 0))],
        out_specs=(pl.BlockSpec((1, tm, d), lambda b, i: (b, i, 0)),
                   pl.BlockSpec((1, tm, LANES), lambda b, i: (b, i, 0))),
        compiler_params=_cp(("parallel", "parallel")), name="moe_norm_router",
    )(xs, norm_w.reshape(1, d), mod, mod, router_p)


def _excl_count_rows(x):
    n = x.shape[0]
    blk = min(n, 256)
    tri = jnp.where(_iota((blk, blk), 0) > _iota((blk, blk), 1), 1.0, 0.0).astype(BF16)
    run = jnp.zeros((1, x.shape[1]), F32)
    outs = []
    for r in range(n // blk):
        xb = x[r * blk:(r + 1) * blk]
        outs.append(jnp.dot(tri, xb.astype(BF16), preferred_element_type=F32) + run)
        run = run + jnp.sum(xb, axis=0, keepdims=True)
    return jnp.concatenate(outs, axis=0) if len(outs) > 1 else outs[0]


def _topcap_slots(v, cap, lane_ok):
    capf = float(cap)

    def count_ge(thr):
        return jnp.sum(jnp.where(v >= thr, 1.0, 0.0), axis=0, keepdims=True)

    def bisect(_, c):
        lo, hi = c
        mid = jnp.sqrt(jnp.maximum(lo, 1e-37)) * jnp.sqrt(hi)
        ok = count_ge(mid) >= capf
        return jnp.where(ok, mid, lo), jnp.where(ok, hi, mid)

    shape = (1, v.shape[1])
    lo, hi = lax.fori_loop(0, 34, bisect, (jnp.zeros(shape, F32), jnp.full(shape, 2.0, F32)))
    thr, found, upper = lo, jnp.zeros(shape, F32), hi
    for _ in range(4):
        m = jnp.max(jnp.where(v < upper, v, -1.0), axis=0, keepdims=True)
        ok = jnp.where(count_ge(m) >= capf, 1.0, 0.0) * (1.0 - found)
        thr = jnp.where(ok > 0, m, thr)
        found = jnp.maximum(found, ok)
        upper = jnp.where(found > 0, upper, m)
    gt = jnp.where(v > thr, 1.0, 0.0)
    eq = jnp.where(v == thr, 1.0, 0.0)
    need = capf - jnp.sum(gt, axis=0, keepdims=True)
    sel = jnp.maximum(gt, eq * jnp.where(_excl_count_rows(eq) < need, 1.0, 0.0)) * lane_ok
    return jnp.where(sel > 0, _excl_count_rows(sel), -1.0)


def _route_body(lg_ref, aff_ref, code_ref, codet_ref, *, n_lat, cap_l, cap_c):
    lane_ok = _iota((1, LANES), 1) < N_EXPERTS
    lg = jnp.where(lane_ok, lg_ref[0], NEG_INF)
    e = jnp.exp(lg - jnp.max(lg, axis=-1, keepdims=True))
    aff = e / jnp.sum(e, axis=-1, keepdims=True)
    aff_ref[0] = aff
    okf = lane_ok.astype(F32)
    code = jnp.concatenate([_topcap_slots(aff[0:n_lat], cap_l, okf), _topcap_slots(aff[n_lat:], cap_c, okf)], axis=0)
    code_ref[0] = code
    codet_ref[0] = code.T


def _route(logits, n_lat, cap_l, cap_c):
    bsz, t, _ = logits.shape
    body = functools.partial(_route_body, n_lat=n_lat, cap_l=cap_l, cap_c=cap_c)
    blk = pl.BlockSpec((1, t, LANES), lambda b: (b, 0, 0))
    return pl.pallas_call(
        body, out_shape=(S((bsz, t, LANES), F32), S((bsz, t, LANES), F32), S((bsz, LANES, t), F32)), grid=(bsz,),
        in_specs=[blk], out_specs=(blk, blk, pl.BlockSpec((1, LANES, t), lambda b: (b, 0, 0))),
        compiler_params=_cp(("parallel",)), name="moe_route",
    )(logits)


def _gather_body(codet_ref, h_ref, xl_ref, xc_ref, *, n_lat, n_ctx, cap_l, cap_c, grp):
    hl = h_ref[0, 0:n_lat, :]
    hc = h_ref[0, n_lat:, :]
    il = _iota((cap_l, n_lat), 0).astype(F32)
    ic = _iota((cap_c, n_ctx), 0).astype(F32)
    for e0 in range(0, N_EXPERTS, grp):
        onehot = jnp.concatenate([jnp.where(il == codet_ref[0, e:e + 1, 0:n_lat], 1.0, 0.0).astype(BF16)
                                  for e in range(e0, e0 + grp)], axis=0)
        xe = jnp.dot(onehot, hl, preferred_element_type=F32).astype(BF16)
        for r in range(grp):
            xl_ref[0, e0 + r] = xe[r * cap_l:(r + 1) * cap_l]
    onehot = jnp.concatenate([jnp.where(ic == codet_ref[0, e:e + 1, n_lat:], 1.0, 0.0).astype(BF16)
                              for e in range(N_EXPERTS)], axis=0)
    xc_ref[0] = jnp.dot(onehot, hc, preferred_element_type=F32).astype(BF16)


def _gather(codet, h, n_lat, cap_l, cap_c):
    bsz, t, d = h.shape
    body = functools.partial(_gather_body, n_lat=n_lat, n_ctx=t - n_lat, cap_l=cap_l, cap_c=cap_c, grp=4)
    return pl.pallas_call(
        body, out_shape=(S((bsz, N_EXPERTS, cap_l, d), BF16), S((bsz, N_EXPERTS * cap_c, d), BF16)), grid=(bsz,),
        in_specs=[pl.BlockSpec((1, LANES, t), lambda b: (b, 0, 0)), pl.BlockSpec((1, t, d), lambda b: (b, 0, 0))],
        out_specs=(pl.BlockSpec((1, N_EXPERTS, cap_l, d), lambda b: (b, 0, 0, 0)),
                   pl.BlockSpec((1, N_EXPERTS * cap_c, d), lambda b: (b, 0, 0))),
        compiler_params=_cp(("parallel",)), name="moe_gather",
    )(codet, h)


def _ffn_body(xl_ref, xc_ref, wg_ref, wu_ref, wd_ref, yl_ref, yc_ref, x_scr, acc, *, nb, cap_l, cap_c):
    f = pl.program_id(1)
    d = x_scr.shape[1]

    @pl.when(f == 0)
    def _():
        x_scr[0:nb * cap_l, :] = xl_ref[:, 0].reshape(nb * cap_l, d)
        x_scr[nb * cap_l:, :] = xc_ref[:, 0].reshape(nb * cap_c, d)
        acc[...] = jnp.zeros_like(acc)

    x = x_scr[...]
    a = jnp.dot(x, wg_ref[0, 0].astype(BF16), preferred_element_type=F32)
    u = jnp.dot(x, wu_ref[0, 0].astype(BF16), preferred_element_type=F32)
    acc[...] += _mm(_silu(a) * u, wd_ref[0, 0])

    @pl.when(f == pl.num_programs(1) - 1)
    def _():
        y = acc[...].astype(BF16)
        yl_ref[:, 0] = y[0:nb * cap_l].reshape(nb, cap_l, d)
        yc_ref[:, 0] = y[nb * cap_l:].reshape(nb, cap_c, d)


def _expert_ffn(xl, xc, w_gate, w_up, w_down, layer):
    bsz, ne, cap_l, d = xl.shape
    cap_c = xc.shape[2]
    ffd = w_gate.shape[-1]
    tf = 256
    rows = bsz * (cap_l + cap_c)
    body = functools.partial(_ffn_body, nb=bsz, cap_l=cap_l, cap_c=cap_c)
    return pl.pallas_call(
        body, out_shape=(S(xl.shape, BF16), S(xc.shape, BF16)), grid=(ne, ffd // tf),
        in_specs=[pl.BlockSpec((bsz, 1, cap_l, d), lambda e, f: (0, e, 0, 0)),
                  pl.BlockSpec((bsz, 1, cap_c, d), lambda e, f: (0, e, 0, 0)),
                  pl.BlockSpec((1, 1, d, tf), lambda e, f: (layer, e, 0, f)),
                  pl.BlockSpec((1, 1, d, tf), lambda e, f: (layer, e, 0, f)),
                  pl.BlockSpec((1, 1, tf, d), lambda e, f: (layer, e, f, 0))],
        out_specs=(pl.BlockSpec((bsz, 1, cap_l, d), lambda e, f: (0, e, 0, 0)),
                   pl.BlockSpec((bsz, 1, cap_c, d), lambda e, f: (0, e, 0, 0))),
        scratch_shapes=[pltpu.VMEM((rows, d), BF16), pltpu.VMEM((rows, d), F32)],
        compiler_params=_cp(("parallel", "arbitrary")), name="moe_expert_ffn",
    )(xl, xc, w_gate, w_up, w_down)


def _combine_body(code_ref, aff_ref, yl_ref, yc_ref, res_ref, gate_ref, o_ref, *, n_lat, n_b, cap_l, cap_c, tm):
    b, i = pl.program_id(0), pl.program_id(1)
    code = code_ref[0]
    aff = aff_ref[0]

    def scatter(cap, y):
        slot = _iota((tm, cap), 1).astype(F32)
        q = jnp.concatenate([jnp.where(code[:, e:e + 1] == slot, aff[:, e:e + 1], 0.0) for e in range(N_EXPERTS)], axis=1)
        return jnp.dot(q.astype(BF16), y, preferred_element_type=F32)

    @pl.when(i * tm < n_lat)
    def _():
        o_ref[0] = res_ref[0] + gate_ref[pl.ds(b, 1), :] * scatter(cap_l, yl_ref[0])

    @pl.when(i * tm >= n_lat)
    def _():
        o_ref[0] = res_ref[0] + gate_ref[n_b:n_b + 1, :] * scatter(cap_c, yc_ref[0])


def _combine(code, aff, yl, yc, res, mod, n_lat):
    bsz, t, d = res.shape
    cap_l, cap_c = yl.shape[2], yc.shape[2]
    tm = t - n_lat
    assert n_lat % tm == 0
    body = functools.partial(_combine_body, n_lat=n_lat, n_b=bsz, cap_l=cap_l, cap_c=cap_c, tm=tm)
    tok = lambda w: pl.BlockSpec((1, tm, w), lambda b, i: (b, i, 0))
    return pl.pallas_call(
        body, out_shape=S((bsz, t, d), F32), grid=(bsz, t // tm),
        in_specs=[tok(LANES), tok(LANES),
                  pl.BlockSpec((1, N_EXPERTS * cap_l, d), lambda b, i: (b, 0, 0)),
                  pl.BlockSpec((1, N_EXPERTS * cap_c, d), lambda b, i: (b, 0, 0)),
                  tok(d), pl.BlockSpec((MOD_ROWS, d), lambda b, i: (0, 5))],
        out_specs=tok(d),
        compiler_params=_cp(("parallel", "parallel")), name="moe_combine",
    )(code, aff, yl.reshape(bsz, N_EXPERTS * cap_l, d), yc.reshape(bsz, N_EXPERTS * cap_c, d), res, mod)


def _moe_layer(xs, norm_w, mod, router_p, w_gate, w_up, w_down, layer, n_lat):
    bsz, t, d = xs.shape
    n_ctx = t - n_lat
    cap_l = EC_CAPACITY * n_lat // N_EXPERTS
    cap_c = EC_CAPACITY * n_ctx // N_EXPERTS
    h, logits = _norm_router(xs, norm_w, mod, router_p, n_lat)
    aff, code, codet = _route(logits, n_lat, cap_l, cap_c)
    xl, xc = _gather(codet, h, n_lat, cap_l, cap_c)
    yl, yc = _expert_ffn(xl, xc.reshape(bsz, N_EXPERTS, cap_c, d), w_gate, w_up, w_down, layer)
    return _combine(code, aff, yl, yc, xs, mod, n_lat)


def _final_norm_body(x_ref, w_ref, o_ref):
    x = x_ref[0]
    o_ref[0] = x * lax.rsqrt(jnp.mean(x * x, axis=-1, keepdims=True) + EPS) * w_ref[...]


def _final_norm(xs, w, n_lat):
    bsz, _, d = xs.shape
    tm = 512 if n_lat % 512 == 0 else n_lat
    return pl.pallas_call(
        _final_norm_body, out_shape=S((bsz, n_lat, d), F32), grid=(bsz, n_lat // tm),
        in_specs=[pl.BlockSpec((1, tm, d), lambda b, i: (b, i, 0)), pl.BlockSpec((1, d), lambda b, i: (0, 0))],
        out_specs=pl.BlockSpec((1, tm, d), lambda b, i: (b, i, 0)),
        compiler_params=_cp(("parallel", "parallel")), name="final_norm",
    )(xs, w.reshape(1, d))


def _swa_group_columns():
    hd, rep = SW_HD, SW_REP
    cols = []
    for g in range(SW_HKV):
        cols += list(range(g * rep * hd, (g + 1) * rep * hd))
        cols += list(range(SW_HQ * hd + g * hd, SW_HQ * hd + (g + 1) * hd))
        cols += list(range((SW_HQ + SW_HKV) * hd + g * hd, (SW_HQ + SW_HKV) * hd + (g + 1) * hd))
    return np.asarray(cols, np.int32)


def kernel(x, c, ctx, c_ctx, ada_w, ada_b, norm1_w, norm2_w, final_norm_w, hy_w_in, hy_b_in, hy_short_w, hy_short_b, hy_ffn_w1, hy_ffn_b1, hy_ffn_w2, hy_ffn_b2, hy_ffn_w3, hy_sin_freq, hy_filter_bias, hy_w_out, hy_b_out, sw_w_in, sw_sink, sw_w_out, gd_w_in, gd_conv_w, gd_a_log, gd_dt_bias, gd_norm_w, gd_w_out, hg_w_in, hg_lb, hg_norm_w, hg_w_out, moe_router, moe_w_gate, moe_w_up, moe_w_down):
    bsz, n_lat, d = x.shape
    n_ctx = ctx.shape[1]
    depth = ada_w.shape[0]
    assert bsz < MOD_ROWS and n_lat % n_ctx == 0
    xs = jnp.concatenate([x, ctx], axis=1)
    c16 = jnp.zeros((MOD_ROWS, d), F32).at[:bsz].set(c).at[bsz].set(c_ctx)
    mod = _modulation(c16, ada_w, ada_b)
    zero_bias = jnp.zeros((d,), F32)
    gd_pad = (-gd_w_in.shape[1]) % LANES
    gd_w = jnp.pad(gd_w_in, ((0, 0), (0, gd_pad))).astype(BF16)
    sw_w = sw_w_in[:, _swa_group_columns()].astype(BF16)
    for layer in range(depth):
        m = mod[layer]
        nw = norm1_w[layer]
        kind = layer % 4
        if kind == 0:
            u = _in_proj(xs, nw, m, 0, 1, hy_w_in.astype(BF16), hy_b_in, n_lat, "hy_in_proj")
            y = _hyena_mixer(u, n_lat, n_ctx, hy_short_w, hy_short_b, hy_ffn_w1, hy_ffn_b1, hy_ffn_w2, hy_ffn_b2,
                             hy_ffn_w3, hy_sin_freq, hy_filter_bias)
            xs = _out_proj(y, hy_w_out.astype(BF16), hy_b_out, xs, m, 2, n_lat, "hy_out_proj")
        elif kind == 1:
            u = _in_proj(xs, nw, m, 0, 1, sw_w, jnp.zeros((sw_w.shape[1],), F32), n_lat, "sw_in_proj")
            y = _swa_mixer(u, n_lat, n_ctx, sw_sink)
            xs = _out_proj(y, sw_w_out.astype(BF16), zero_bias, xs, m, 2, n_lat, "sw_out_proj")
        elif kind == 2:
            u = _in_proj(xs, nw, m, 0, 1, gd_w, jnp.zeros((gd_w.shape[1],), F32), n_lat, "gd_in_proj")
            y = _gdn_mixer(u, n_lat, n_ctx, gd_conv_w, gd_a_log, gd_dt_bias, gd_norm_w)
            xs = _out_proj(y, gd_w_out.astype(BF16), zero_bias, xs, m, 2, n_lat, "gd_out_proj")
        else:
            u = _in_proj(xs, nw, m, 0, 1, hg_w_in.astype(BF16), jnp.zeros((hg_w_in.shape[1],), F32), n_lat, "hg_in_proj")
            y = _hgrn_mixer(u, n_lat, n_ctx, hg_lb, hg_norm_w, layer)
            xs = _out_proj(y, hg_w_out.astype(BF16), zero_bias, xs, m, 2, n_lat, "hg_out_proj")
        router_p = jnp.pad(moe_router[layer], ((0, 0), (0, LANES - N_EXPERTS)))
        xs = _moe_layer(xs, norm2_w[layer], m, router_p, moe_w_gate, moe_w_up, moe_w_down, layer, n_lat)
    return _final_norm(xs, final_norm_w, n_lat)
```

```python
import functools
import math

import jax
import jax.numpy as jnp
import numpy as np
from jax import lax
from jax.experimental import pallas as pl
from jax.experimental.pallas import tpu as pltpu

F32 = jnp.float32
BF16 = jnp.bfloat16
HIGHEST = lax.Precision.HIGHEST
EPS = 1e-6
NEG_INF = -1e30
LANES = 128
BF16_ROWS = 16
MOD_ROWS = 16

GRID_W = 64
HY_ORDER = 2
HY_EMB = 33
HY_FAST_DECAY = 0.3
HY_SLOW_DECAY = 1.5
HY_TARGET = 1e-2
HY_SHIFT = 0.05
SW_HQ, SW_HKV, SW_HD, SW_WINDOW = 16, 4, 64, 128
SW_REP = SW_HQ // SW_HKV
ROPE_BASE = 10000.0
GD_H, GD_DK = 8, 128
HG_DK = 128
CHUNK = 64
SUPER = 256
GD_SUPER = 128
INTRA_ROWS = 768
DFT_ROWS = 512
N_EXPERTS = 16
EC_CAPACITY = 2

S = jax.ShapeDtypeStruct


def _cp(sem, vmem_mb=48):
    return pltpu.CompilerParams(dimension_semantics=sem, vmem_limit_bytes=vmem_mb * 2**20)


def _iota(shape, dim):
    return lax.broadcasted_iota(jnp.int32, shape, dim)


def _sigmoid(x):
    return 0.5 * jnp.tanh(0.5 * x) + 0.5


def _silu(x):
    return x * _sigmoid(x)


def _softplus(x):
    return jnp.maximum(x, 0.0) + jnp.log(1.0 + jnp.exp(-jnp.abs(x)))


def _mm(a, b):
    return jnp.dot(a.astype(BF16), b.astype(BF16), preferred_element_type=F32)


def _mm_nt(a, b):
    return lax.dot_general(a.astype(BF16), b.astype(BF16), (((1,), (1,)), ((), ())), preferred_element_type=F32)


def _mm_tn(a, b):
    return lax.dot_general(a.astype(BF16), b.astype(BF16), (((0,), (0,)), ((), ())), preferred_element_type=F32)


def _mm_f32(a, b):
    return jnp.dot(a, b, precision=HIGHEST, preferred_element_type=F32)


def _mod_body(c_ref, w_ref, b_ref, o_ref):
    o_ref[0] = _mm_f32(_silu(c_ref[...]), w_ref[0]) + b_ref[0]


def _modulation(c16, ada_w, ada_b):
    depth, d, n = ada_w.shape
    tn = 1024
    return pl.pallas_call(
        _mod_body, out_shape=S((depth, MOD_ROWS, n), F32), grid=(depth, n // tn),
        in_specs=[pl.BlockSpec((MOD_ROWS, d), lambda l, j: (0, 0)),
                  pl.BlockSpec((1, d, tn), lambda l, j: (l, 0, j)),
                  pl.BlockSpec((1, 1, tn), lambda l, j: (l, 0, j))],
        out_specs=pl.BlockSpec((1, MOD_ROWS, tn), lambda l, j: (l, 0, j)),
        compiler_params=_cp(("parallel", "parallel")), name="adaln_mod",
    )(c16, ada_w, ada_b.reshape(depth, 1, n))


def _norm_mod(x, nw, shift_ref, scale_ref, b, row0, n_lat, n_b):
    tm = x.shape[0]
    y = x * lax.rsqrt(jnp.mean(x * x, axis=-1, keepdims=True) + EPS) * nw
    is_ctx = (row0 + _iota((tm, 1), 0)) >= n_lat
    shift = jnp.where(is_ctx, shift_ref[n_b:n_b + 1, :], shift_ref[pl.ds(b, 1), :])
    scale = jnp.where(is_ctx, scale_ref[n_b:n_b + 1, :], scale_ref[pl.ds(b, 1), :])
    return y * (1.0 + scale) + shift


def _row_gate(gate_ref, b, row0, tm, n_lat, n_b):
    is_ctx = (row0 + _iota((tm, 1), 0)) >= n_lat
    return jnp.where(is_ctx, gate_ref[n_b:n_b + 1, :], gate_ref[pl.ds(b, 1), :])


def _in_proj_body(x_ref, nw_ref, sh_ref, sc_ref, w_ref, bias_ref, o_ref, h_scr, *, n_lat, n_b, tm):
    b, i = pl.program_id(0), pl.program_id(1)

    @pl.when(pl.program_id(2) == 0)
    def _():
        h_scr[...] = _norm_mod(x_ref[0], nw_ref[...], sh_ref, sc_ref, b, i * tm, n_lat, n_b).astype(BF16)

    o_ref[0] = (jnp.dot(h_scr[...], w_ref[...], preferred_element_type=F32) + bias_ref[...]).astype(o_ref.dtype)


def _in_proj(xs, norm_w, mod, shift_idx, scale_idx, w, bias, n_lat, name):
    bsz, t, d = xs.shape
    n = w.shape[1]
    tm = t
    tn = 512 if n % 512 == 0 else (384 if n % 384 == 0 else 128)
    body = functools.partial(_in_proj_body, n_lat=n_lat, n_b=bsz, tm=tm)
    return pl.pallas_call(
        body, out_shape=S((bsz, t, n), BF16), grid=(bsz, t // tm, n // tn),
        in_specs=[pl.BlockSpec((1, tm, d), lambda b, i, j: (b, i, 0)),
                  pl.BlockSpec((1, d), lambda b, i, j: (0, 0)),
                  pl.BlockSpec((MOD_ROWS, d), lambda b, i, j: (0, shift_idx)),
                  pl.BlockSpec((MOD_ROWS, d), lambda b, i, j: (0, scale_idx)),
                  pl.BlockSpec((d, tn), lambda b, i, j: (0, j)),
                  pl.BlockSpec((1, tn), lambda b, i, j: (0, j))],
        out_specs=pl.BlockSpec((1, tm, tn), lambda b, i, j: (b, i, j)),
        scratch_shapes=[pltpu.VMEM((tm, d), BF16)],
        compiler_params=_cp(("parallel", "parallel", "arbitrary")), name=name,
    )(xs, norm_w.reshape(1, d), mod, mod, w, bias.reshape(1, n))


def _out_proj_body(y_ref, w_ref, bias_ref, res_ref, gate_ref, o_ref, *, n_lat, n_b, tm):
    b, i = pl.program_id(0), pl.program_id(1)
    y = jnp.dot(y_ref[0].astype(BF16), w_ref[...], preferred_element_type=F32) + bias_ref[...]
    o_ref[0] = res_ref[0] + _row_gate(gate_ref, b, i * tm, tm, n_lat, n_b) * y


def _out_proj(y, w, bias, res, mod, gate_idx, n_lat, name):
    bsz, t, dy = y.shape
    d = w.shape[1]
    tm = t // 3 if t % 3 == 0 and (t // 3) % 8 == 0 else t
    body = functools.partial(_out_proj_body, n_lat=n_lat, n_b=bsz, tm=tm)
    return pl.pallas_call(
        body, out_shape=S((bsz, t, d), F32), grid=(bsz, t // tm),
        in_specs=[pl.BlockSpec((1, tm, dy), lambda b, i: (b, i, 0)),
                  pl.BlockSpec((dy, d), lambda b, i: (0, 0)),
                  pl.BlockSpec((1, d), lambda b, i: (0, 0)),
                  pl.BlockSpec((1, tm, d), lambda b, i: (b, i, 0)),
                  pl.BlockSpec((MOD_ROWS, d), lambda b, i: (0, gate_idx))],
        out_specs=pl.BlockSpec((1, tm, d), lambda b, i: (b, i, 0)),
        compiler_params=_cp(("parallel", "parallel")), name=name,
    )(y, w, bias.reshape(1, d), res, mod)


def _dft_tables(n):
    k = np.arange(n, dtype=np.int64)
    ang = (np.outer(k, k) % (2 * n)).astype(np.float64) * (math.pi / n)
    c = np.cos(ang)
    s = -np.sin(ang)
    s[0, :] = 1.0 - 2.0 * (k % 2)
    return jnp.asarray(np.stack([c, s]), dtype=BF16)


def _hy_positional(n):
    t = np.linspace(0.0, 1.0, n)[:, None]
    bands = (HY_EMB - 1) // 2
    w = (2.0 * math.pi * np.arange(n) / n)[:, None]
    f = np.linspace(1e-4, bands - 1, bands)[None, :]
    z = np.concatenate([t, np.cos(f * w), -np.sin(f * w)], axis=-1)
    zp = np.zeros((n, LANES), np.float32)
    zp[:, :HY_EMB] = z
    return jnp.asarray(zp), jnp.asarray(t.astype(np.float32))


def _hy_filter_body(z_ref, w1_ref, b1_ref, w2_ref, b2_ref, fr_ref, w3f_ref, w3b_ref, t_ref, dl_ref, cs_ref,
                    p1_ref, p2_ref, pn_ref, *, n):
    h = jnp.sin(fr_ref[0:1, :] * (_mm_f32(z_ref[...], w1_ref[...]) + b1_ref[...]))
    h = jnp.sin(fr_ref[1:2, :] * (_mm_f32(h, w2_ref[...]) + b2_ref[...]))
    win = jnp.exp(-t_ref[...] * dl_ref[...]) + HY_SHIFT
    hf = _mm_f32(h, w3f_ref[...]) * win
    hb = _mm_f32(h, w3b_ref[...]) * win
    row = _iota((n, 1), 0)
    hb = jnp.where(row == 0, 0.0, hb)
    hs, hd = hf + hb, hf - hb
    alt = (1 - 2 * (row & 1)).astype(F32)
    knyq = jnp.sum(alt * hs, axis=0, keepdims=True)
    pn_ref[0] = jnp.broadcast_to(knyq * (0.5 / n), pn_ref.shape[1:])
    hs, hd = hs.astype(BF16), hd.astype(BF16)
    rc = min(n, DFT_ROWS)
    for r0 in range(0, n, rc):
        rows = slice(r0, r0 + rc)
        wgt = jnp.where(row[rows] == 0, 0.5 / n, 1.0 / n)
        p1_ref[0, rows, :] = _mm(cs_ref[0, rows, :], hs) * wgt
        p2_ref[0, rows, :] = jnp.where(row[rows] == 0, 0.0, _mm(cs_ref[1, rows, :], hd) * wgt)


def _hy_filter_spectra(n, w1p, b1, w2, b2, freq, w3, cs, d, td):
    zpad, tcol = _hy_positional(n)
    max_decay = math.log(HY_TARGET) / HY_FAST_DECAY
    min_decay = math.log(HY_TARGET) / HY_SLOW_DECAY
    deltas = jnp.asarray(np.abs(np.linspace(min_decay, max_decay, d)), dtype=F32)[None, :]
    nd = d // td
    ff = w1p.shape[1]
    body = functools.partial(_hy_filter_body, n=n)
    const = lambda o, j: (0, 0)
    return pl.pallas_call(
        body, out_shape=(S((HY_ORDER, n, d), F32), S((HY_ORDER, n, d), F32), S((HY_ORDER, 8, d), F32)),
        grid=(HY_ORDER, nd),
        in_specs=[pl.BlockSpec((n, LANES), const), pl.BlockSpec((LANES, ff), const), pl.BlockSpec((1, ff), const),
                  pl.BlockSpec((ff, ff), const), pl.BlockSpec((1, ff), const), pl.BlockSpec((2, ff), const),
                  pl.BlockSpec((ff, td), lambda o, j: (0, (2 * o) * nd + j)),
                  pl.BlockSpec((ff, td), lambda o, j: (0, (2 * o + 1) * nd + j)),
                  pl.BlockSpec((n, 1), const), pl.BlockSpec((1, td), lambda o, j: (0, j)),
                  pl.BlockSpec((2, n, n), lambda o, j: (0, 0, 0), pipeline_mode=pl.Buffered(1))],
        out_specs=(pl.BlockSpec((1, n, td), lambda o, j: (o, 0, j)),
                   pl.BlockSpec((1, n, td), lambda o, j: (o, 0, j)),
                   pl.BlockSpec((1, 8, td), lambda o, j: (o, 0, j))),
        compiler_params=_cp(("parallel", "parallel")), name=f"hy_filter_{n}",
    )(zpad, w1p, b1.reshape(1, ff), w2, b2.reshape(1, ff), freq, w3, w3, tcol, deltas, cs)


def _short_conv(x, w_ref, b_ref):
    n = x.shape[0]
    row = _iota((n, 1), 0)
    xp = jnp.where(row == 0, 0.0, pltpu.roll(x, 1, 0))
    xn = jnp.where(row == n - 1, 0.0, pltpu.roll(x, n - 1, 0))
    return w_ref[0:1, :] * xp + w_ref[1:2, :] * x + w_ref[2:3, :] * xn + b_ref[...]


def _short_conv_rows(x_ref, w_ref, b_ref, r0, rc, n):
    x = x_ref[0, r0:r0 + rc, :].astype(F32)
    row = _iota((rc, 1), 0)
    pk = BF16_ROWS
    prev = x_ref[0, r0 - pk:r0, :].astype(F32)[pk - 1:pk] if r0 > 0 else 0.0
    nxt = x_ref[0, r0 + rc:r0 + rc + pk, :].astype(F32)[0:1] if r0 + rc < n else 0.0
    xp = jnp.where(row == 0, prev, pltpu.roll(x, 1, 0))
    xn = jnp.where(row == rc - 1, nxt, pltpu.roll(x, rc - 1, 0))
    return w_ref[0:1, :] * xp + w_ref[1:2, :] * x + w_ref[2:3, :] * xn + b_ref[...]


def _hy_conv_body(a_ref, g_ref, wa_ref, ba_ref, wg_ref, bg_ref, cs_ref, p1_ref, p2_ref, pn_ref, fb_ref, o_ref,
                  a_s, ab_s, yr_s, yi_s, *, n, conv_a):
    a = a_ref[0].astype(F32)
    if conv_a:
        a = _short_conv(a, wa_ref, ba_ref)
    a_s[...] = a
    ab_s[...] = a.astype(BF16)
    rc = min(n, DFT_ROWS)
    row = _iota((rc, 1), 0)
    alt = (1 - 2 * (row & 1)).astype(F32)
    asum = None
    for r0 in range(0, n, rc):
        rows = slice(r0, r0 + rc)
        re = _mm(cs_ref[0, rows, :], ab_s[...])
        im = _mm(cs_ref[1, rows, :], ab_s[...])
        p1, p2 = p1_ref[0, rows, :], p2_ref[0, rows, :]
        yr_s[rows, :] = (re * p1 - im * p2).astype(BF16)
        yi = re * p2 + im * (jnp.where(row == 0, pn_ref[0, 0:1, :], p1) if r0 == 0 else p1)
        yi_s[rows, :] = yi.astype(BF16)
        part = jnp.sum(alt * yi, axis=0, keepdims=True)
        asum = part if asum is None else asum + part
        if r0 == 0:
            nyq = yi[0:1, :]
    for r0 in range(0, n, rc):
        rows = slice(r0, r0 + rc)
        y = _mm(cs_ref[0, rows, :], yr_s[...]) + _mm(cs_ref[1, rows, :], yi_s[...])
        y = y + alt * nyq
        if r0 == 0:
            y = y - jnp.where(row == 0, asum, 0.0)
        g = _short_conv_rows(g_ref, wg_ref, bg_ref, r0, rc, n)
        o_ref[0, rows, :] = g * (y + a_s[rows, :] * fb_ref[0])


def _hy_conv(a, a_col0, a_row, g, g_col0, g_row, n, conv_w, conv_b, cs, p1, p2, pn, fbias, order, conv_a, d, td, name):
    bsz = a.shape[0]
    nd = d // td
    body = functools.partial(_hy_conv_body, n=n, conv_a=conv_a)
    cw = conv_w
    cb = conv_b.reshape(1, -1)
    return pl.pallas_call(
        body, out_shape=S((bsz, n, d), F32), grid=(nd, bsz),
        in_specs=[pl.BlockSpec((1, n, td), lambda j, b: (b, a_row, a_col0 * nd + j)),
                  pl.BlockSpec((1, n, td), lambda j, b: (b, g_row, g_col0 * nd + j)),
                  pl.BlockSpec((3, td), lambda j, b: (0, a_col0 * nd + j if conv_a else j)),
                  pl.BlockSpec((1, td), lambda j, b: (0, a_col0 * nd + j if conv_a else j)),
                  pl.BlockSpec((3, td), lambda j, b: (0, g_col0 * nd + j)),
                  pl.BlockSpec((1, td), lambda j, b: (0, g_col0 * nd + j)),
                  pl.BlockSpec((2, n, n), lambda j, b: (0, 0, 0), pipeline_mode=pl.Buffered(1)),
                  pl.BlockSpec((1, n, td), lambda j, b: (order, 0, j), pipeline_mode=pl.Buffered(1)),
                  pl.BlockSpec((1, n, td), lambda j, b: (order, 0, j), pipeline_mode=pl.Buffered(1)),
                  pl.BlockSpec((1, 8, td), lambda j, b: (order, 0, j)),
                  pl.BlockSpec((1, 1, td), lambda j, b: (order, 0, j))],
        out_specs=pl.BlockSpec((1, n, td), lambda j, b: (b, 0, j)),
        scratch_shapes=[pltpu.VMEM((n, td), F32), pltpu.VMEM((n, td), BF16), pltpu.VMEM((n, td), BF16),
                        pltpu.VMEM((n, td), BF16)],
        compiler_params=_cp(("parallel", "parallel"), 56), name=name,
    )(a, g, cw, cb, cw, cb, cs, p1, p2, pn, fbias.reshape(HY_ORDER, 1, d))


def _hyena_mixer(u, n_lat, n_ctx, short_w, short_b, w1, b1, w2, b2, w3, freq, fbias):
    d = u.shape[2] // 3
    ff = w1.shape[1]
    w1p = jnp.zeros((LANES, ff), F32).at[:w1.shape[0]].set(w1)
    outs = []
    for n, row in ((n_lat, 0), (n_ctx, n_lat // n_ctx)):
        td = 256 if n > 512 else 512
        cs = _dft_tables(n)
        p1, p2, pn = _hy_filter_spectra(n, w1p, b1, w2, b2, freq, w3, cs, d, td)
        z1 = _hy_conv(u, 0, row, u, 1, row, n, short_w, short_b, cs, p1, p2, pn, fbias, 0, True, d, td, f"hy_conv1_{n}")
        z2 = _hy_conv(z1, 0, 0, u, 2, row, n, short_w, short_b, cs, p1, p2, pn, fbias, 1, False, d, td, f"hy_conv2_{n}")
        outs.append(z2)
    return jnp.concatenate(outs, axis=1)


def _rope_tables(n_lat, width, rot_heads):
    hd = SW_HD
    rows = n_lat // GRID_W
    row = np.repeat(np.arange(rows, dtype=np.float64), GRID_W)
    col = np.tile(np.arange(GRID_W, dtype=np.float64), rows)
    nf = hd // 4
    inv = ROPE_BASE ** (-np.arange(nf, dtype=np.float64) / nf)
    ang = np.concatenate([row[:, None] * inv, col[:, None] * inv], axis=-1)
    cos, sin = np.cos(ang), np.sin(ang)
    zero = np.zeros_like(sin)
    c = np.ones((n_lat, width), np.float32)
    sa = np.zeros((n_lat, width), np.float32)
    sb = np.zeros((n_lat, width), np.float32)
    for h in range(rot_heads):
        c[:, h * hd:(h + 1) * hd] = np.concatenate([cos, cos], axis=-1)
        sa[:, h * hd:(h + 1) * hd] = np.concatenate([-sin, zero], axis=-1)
        sb[:, h * hd:(h + 1) * hd] = np.concatenate([zero, sin], axis=-1)
    return jnp.asarray(c), jnp.asarray(sa), jnp.asarray(sb)


def _rope(x, c, sa, sb):
    w = x.shape[1]
    half = SW_HD // 2
    return x * c + pltpu.roll(x, w - half, 1) * sa + pltpu.roll(x, half, 1) * sb


def _sink_attend(q, kvs, sink):
    ss = []
    m = None
    for k, _, mask in kvs:
        s = _mm_nt(q, k)
        if mask is not None:
            s = jnp.where(mask, s, NEG_INF)
        ss.append(s)
        sm = jnp.max(s, axis=-1, keepdims=True)
        m = sm if m is None else jnp.maximum(m, sm)
    m = jnp.maximum(m, sink)
    den = jnp.exp(sink - m)
    o = None
    for s, (_, v, _) in zip(ss, kvs):
        p = jnp.exp(s - m)
        den = den + jnp.sum(p, axis=-1, keepdims=True)
        pv = _mm(p, v)
        o = pv if o is None else o + pv
    return o / den


def _swa_body(u_ref, cq_ref, saq_ref, sbq_ref, ck_ref, sak_ref, sbk_ref, sink_ref, o_ref, kv_scr,
              *, n_lat, n_ctx, blk):
    g = pl.program_id(1)
    hd, rep = SW_HD, SW_REP
    qw = rep * hd
    scale = hd ** -0.5
    span = 3 * blk
    kv_scr[0:n_lat, :] = _rope(u_ref[0, 0:n_lat, qw:qw + 2 * hd].astype(F32), ck_ref[...], sak_ref[...],
                               sbk_ref[...]).astype(BF16)
    kv_scr[n_lat:, :] = u_ref[0, n_lat:, qw:qw + 2 * hd]
    kc = kv_scr[n_lat:, 0:hd]
    vc = kv_scr[n_lat:, hd:2 * hd]

    def qblock(i, carry):
        r0 = pl.multiple_of(i * blk, blk)
        rows = pl.ds(r0, blk)
        q = _rope(u_ref[0, rows, 0:qw].astype(F32), cq_ref[rows, :], saq_ref[rows, :], sbq_ref[rows, :]) * scale
        q = q.astype(BF16)
        ks = pl.multiple_of(jnp.clip((i - 1) * blk, 0, n_lat - span), blk)
        kl = kv_scr[pl.ds(ks, span), 0:hd]
        vl = kv_scr[pl.ds(ks, span), hd:2 * hd]
        qpos = r0 + _iota((blk, span), 0)
        kpos = ks + _iota((blk, span), 1)
        mask = jnp.abs(qpos - kpos) <= SW_WINDOW
        for r in range(rep):
            o = _sink_attend(q[:, r * hd:(r + 1) * hd], [(kc, vc, None), (kl, vl, mask)], sink_ref[g * rep + r])
            o_ref[0, rows, r * hd:(r + 1) * hd] = o
        return carry

    lax.fori_loop(0, n_lat // blk, qblock, 0)
    qc = (u_ref[0, n_lat:, 0:qw].astype(F32) * scale).astype(BF16)
    for r in range(rep):
        o_ref[0, n_lat:, r * hd:(r + 1) * hd] = _sink_attend(qc[:, r * hd:(r + 1) * hd], [(kc, vc, None)], sink_ref[g * rep + r])


def _swa_mixer(u, n_lat, n_ctx, sink):
    bsz, t, _ = u.shape
    hd, rep = SW_HD, SW_REP
    gw = rep * hd + 2 * hd
    blk = SW_WINDOW
    assert n_lat % blk == 0 and n_lat >= 3 * blk and n_lat % GRID_W == 0
    cq, saq, sbq = _rope_tables(n_lat, rep * hd, rep)
    ck, sak, sbk = _rope_tables(n_lat, 2 * hd, 1)
    body = functools.partial(_swa_body, n_lat=n_lat, n_ctx=n_ctx, blk=blk)
    tab = lambda w: pl.BlockSpec((n_lat, w), lambda b, g: (0, 0))
    return pl.pallas_call(
        body, out_shape=S((bsz, t, SW_HQ * hd), F32), grid=(bsz, SW_HKV),
        in_specs=[pl.BlockSpec((1, t, gw), lambda b, g: (b, 0, g)),
                  tab(rep * hd), tab(rep * hd), tab(rep * hd), tab(2 * hd), tab(2 * hd), tab(2 * hd),
                  pl.BlockSpec(memory_space=pltpu.SMEM)],
        out_specs=pl.BlockSpec((1, t, rep * hd), lambda b, g: (b, 0, g)),
        scratch_shapes=[pltpu.VMEM((t, 2 * hd), BF16)],
        compiler_params=_cp(("parallel", "parallel")), name="swa_attention",
    )(u, cq, saq, sbq, ck, sak, sbk, sink)


def _seq_conv(x, w_ref, n_lat):
    t = x.shape[0]
    row = _iota((t, 1), 0)
    first = (row == 0) | (row == n_lat)
    last = (row == n_lat - 1) | (row == t - 1)
    xp = jnp.where(first, 0.0, pltpu.roll(x, 1, 0))
    xn = jnp.where(last, 0.0, pltpu.roll(x, t - 1, 0))
    return w_ref[0:1, :] * xp + w_ref[1:2, :] * x + w_ref[2:3, :] * xn


def _chunk_scan(x, reverse):
    t = x.shape[0]
    pos = _iota((t, 1), 0) & (CHUNK - 1)
    s = 1
    while s < CHUNK:
        if reverse:
            x = x + jnp.where(pos < CHUNK - s, pltpu.roll(x, t - s, 0), 0.0)
        else:
            x = x + jnp.where(pos >= s, pltpu.roll(x, s, 0), 0.0)
        s *= 2
    return x


def _chunk_scan_lanes(x, reverse):
    t = x.shape[1]
    pos = _iota((1, t), 1) & (CHUNK - 1)
    s = 1
    while s < CHUNK:
        if reverse:
            x = x + jnp.where(pos < CHUNK - s, pltpu.roll(x, t - s, 1), 0.0)
        else:
            x = x + jnp.where(pos >= s, pltpu.roll(x, s, 1), 0.0)
        s *= 2
    return x


def _chunk_order(s, n_lat_chunks, n_ctx_chunks, direction):
    if direction == 0:
        return jnp.where(s < n_ctx_chunks, n_lat_chunks + s, s - n_ctx_chunks)
    return n_lat_chunks + n_ctx_chunks - 1 - s


def _intra_unroll(n_super, sup=SUPER):
    return next(u for u in range(INTRA_ROWS // sup, 0, -1) if n_super % u == 0)


def _gated_rms(o, nw, z):
    return o * lax.rsqrt(jnp.mean(o * o, axis=-1, keepdims=True) + EPS) * nw * _silu(z)


def _bmm(a, b):
    return jnp.einsum('bij,bjk->bik', a.astype(BF16), b.astype(BF16), preferred_element_type=F32)


def _bmm_nt(a, b):
    return jnp.einsum('bik,bjk->bij', a.astype(BF16), b.astype(BF16), preferred_element_type=F32)


def _unit_tri_inverse(a, ii, jj):
    eye = (ii == jj).astype(F32)
    a8 = jnp.where((ii >> 3) == (jj >> 3), a, 0.0)
    a8_2 = _bmm(a8, a8)
    a8_4 = _bmm(a8_2, a8_2)
    x = _bmm(_bmm(eye - a8, eye + a8_2), eye + a8_4)
    sh = 3
    while (1 << sh) < CHUNK:
        e = jnp.where(((ii >> (sh + 1)) == (jj >> (sh + 1))) & ((ii >> sh) != (jj >> sh)), a, 0.0)
        x = x - _bmm(_bmm(x, e), x)
        sh += 1
    return x


def _gdn_body(q_ref, k_ref, v_ref, z_ref, ba_ref, cwq_ref, cwk_ref, cwv_ref, par_ref, nw_ref, o_ref,
              kn_s, qn_s, qe_s, kk_s, egl_s, kb_s, kbe_s, vb_s, gc_s, gt_s, qp_s, k2_s, n_s, oacc,
              *, n_lat, n_ctx):
    t = n_lat + n_ctx
    h = pl.program_id(1)
    lane = _iota((1, LANES), 1)
    q = _silu(_seq_conv(q_ref[0].astype(F32), cwq_ref, n_lat))
    k = _silu(_seq_conv(k_ref[0].astype(F32), cwk_ref, n_lat))
    v = _silu(_seq_conv(v_ref[0].astype(F32), cwv_ref, n_lat))
    qn = q * lax.rsqrt(jnp.sum(q * q, axis=-1, keepdims=True) + EPS) * (GD_DK ** -0.5)
    kn = k * lax.rsqrt(jnp.sum(k * k, axis=-1, keepdims=True) + EPS)
    kn_s[...] = kn.astype(BF16)
    qn_s[...] = qn.astype(BF16)
    ba = ba_ref[0].astype(F32)
    par = par_ref[...]

    def pick(x, c):
        return jnp.sum(jnp.where(lane == c, x, 0.0), axis=-1, keepdims=True)

    gpack = jnp.zeros((t, LANES), F32)
    for d in range(2):
        a_log = pick(par[d:d + 1, :], h)
        dt_b = pick(par[2 + d:3 + d, :], h)
        g = -jnp.exp(a_log) * _softplus(pick(ba, 2 * GD_H + d * GD_H + h) + dt_b)
        gpack = jnp.where(lane == d, g, gpack)
    g_t = gpack.T[0:8, :]
    pre = _chunk_scan_lanes(g_t, False)
    suf = _chunk_scan_lanes(g_t, True)
    row8 = _iota((8, 1), 0)
    gcum_t = jnp.where(row8 == 1, suf, pre)
    gt_s[...] = gcum_t
    packed = jnp.where(row8 < 2, gcum_t, pltpu.roll(pre + suf - g_t, 2, 0))
    cols = jnp.concatenate([packed, jnp.zeros((LANES - 8, t), F32)], axis=0).T
    gc_s[...] = cols
    for d in range(2):
        beta = _sigmoid(pick(ba, d * GD_H + h))
        gc = jnp.broadcast_to(cols[:, d:d + 1], (t, LANES))
        glast = jnp.broadcast_to(cols[:, 2 + d:3 + d], (t, LANES))
        eg = jnp.exp(gc)
        kb = kn * beta
        qe_s[d] = (qn * eg).astype(BF16)
        kk_s[d] = (kn * jnp.exp(glast - gc)).astype(BF16)
        egl_s[d] = jnp.exp(glast)
        kb_s[d] = kb.astype(BF16)
        kbe_s[d] = (kb * eg).astype(BF16)
        vb_s[d] = (v * beta).astype(BF16)

    sup = GD_SUPER
    ii = _iota((sup, sup), 0)
    jj = _iota((sup, sup), 1)
    same = (ii >> 6) == (jj >> 6)
    per = sup // CHUNK
    unroll = _intra_unroll(t // sup, sup)

    def intra(it, carry):
        r0s = [pl.multiple_of((it * unroll + kq) * sup, sup) for kq in range(unroll)]
        kcs = [kn_s[pl.ds(r0, sup), :] for r0 in r0s]
        pairs = [(kq, d) for kq in range(unroll) for d in range(2)]
        decs = []
        for kq, d in pairs:
            rows = pl.ds(r0s[kq], sup)
            incl = same & ((jj >= ii) if d else (jj <= ii))
            dif = jnp.broadcast_to(gc_s[rows, d:d + 1], (sup, sup)) - gt_s[d:d + 1, rows]
            decs.append(jnp.where(incl, jnp.exp(jnp.where(incl, dif, 0.0)), 0.0))
        dec = jnp.stack(decs)
        kc2 = jnp.stack([kcs[kq] for kq, _ in pairs])
        kb = jnp.stack([kb_s[d, pl.ds(r0s[kq], sup), :] for kq, d in pairs])
        rhs = jnp.stack([jnp.concatenate([vb_s[d, pl.ds(r0s[kq], sup), :], kbe_s[d, pl.ds(r0s[kq], sup), :]], axis=1)
                         for kq, d in pairs])
        a = jnp.where(ii == jj, 0.0, _bmm_nt(kb, kc2) * dec)
        uw = _bmm(_unit_tri_inverse(a, ii, jj), rhs).astype(BF16)
        qk = _bmm_nt(jnp.stack([qn_s[pl.ds(r0, sup), :] for r0 in r0s]), jnp.stack(kcs))
        auw = _bmm(jnp.stack([qk[kq] for kq, _ in pairs]) * dec, uw)
        for i, (kq, d) in enumerate(pairs):
            rows = pl.ds(r0s[kq], sup)
            qp_s[d, rows, :] = (qe_s[d, rows, :].astype(F32) - auw[i, :, LANES:]).astype(BF16)
            if d == 1:
                oacc[rows, :] = auw[i - 1, :, 0:LANES] + auw[i, :, 0:LANES]
            for c4 in range(per):
                kuw = _mm_tn(kk_s[d, pl.ds(r0s[kq] + c4 * CHUNK, CHUNK), :], uw[i, c4 * CHUNK:(c4 + 1) * CHUNK])
                crow = pl.ds(pl.multiple_of(((it * unroll + kq) * per + c4) * GD_DK, GD_DK), GD_DK)
                n_s[d, crow, :] = kuw[:, 0:LANES]
                k2_s[d, crow, :] = kuw[:, LANES:].astype(BF16)
        return carry

    lax.fori_loop(0, t // sup // unroll, intra, 0)

    nl, nc = n_lat // CHUNK, n_ctx // CHUNK

    def step(s, states):
        new = []
        for d in range(2):
            st = states[d]
            sb = st.astype(BF16)
            c = _chunk_order(s, nl, nc, d)
            rows = pl.ds(pl.multiple_of(c * CHUNK, CHUNK), CHUNK)
            crow = pl.ds(pl.multiple_of(c * GD_DK, GD_DK), GD_DK)
            oacc[rows, :] += jnp.dot(qp_s[d, rows, :], sb, preferred_element_type=F32)
            new.append(st * egl_s[d, pl.ds(c * CHUNK, 1), :] + n_s[d, crow, :]
                       - jnp.dot(k2_s[d, crow, :], sb, preferred_element_type=F32))
        return tuple(new)

    zero = jnp.zeros((GD_DK, LANES), F32)
    lax.fori_loop(0, nl + nc, step, (zero, zero))
    o_ref[0] = _gated_rms(oacc[...], nw_ref[...], z_ref[0].astype(F32))


def _gdn_mixer(u, n_lat, n_ctx, conv_w, a_log, dt_bias, norm_w):
    bsz, t, _ = u.shape
    assert t % SUPER == 0 and n_lat % SUPER == 0
    par = jnp.zeros((8, LANES), F32).at[0:2, :GD_H].set(a_log).at[2:4, :GD_H].set(dt_bias)
    body = functools.partial(_gdn_body, n_lat=n_lat, n_ctx=n_ctx)
    sec = lambda s: pl.BlockSpec((1, t, LANES), lambda b, h: (b, 0, s * GD_H + h))
    cw = lambda s: pl.BlockSpec((3, LANES), lambda b, h: (0, s * GD_H + h))
    both = lambda dt: pltpu.VMEM((2, t, LANES), dt)
    state_rows = (t // CHUNK) * GD_DK
    return pl.pallas_call(
        body, out_shape=S((bsz, t, GD_H * LANES), F32), grid=(bsz, GD_H),
        in_specs=[sec(0), sec(1), sec(2), sec(3),
                  pl.BlockSpec((1, t, LANES), lambda b, h: (b, 0, 4 * GD_H)),
                  cw(0), cw(1), cw(2),
                  pl.BlockSpec((8, LANES), lambda b, h: (0, 0)),
                  pl.BlockSpec((1, LANES), lambda b, h: (0, 0))],
        out_specs=pl.BlockSpec((1, t, LANES), lambda b, h: (b, 0, h)),
        scratch_shapes=[pltpu.VMEM((t, LANES), BF16), pltpu.VMEM((t, LANES), BF16),
                        both(BF16), both(BF16), both(F32), both(BF16), both(BF16), both(BF16),
                        pltpu.VMEM((t, LANES), F32),
                        pltpu.VMEM((8, t), F32), both(BF16), pltpu.VMEM((2, state_rows, LANES), BF16),
                        pltpu.VMEM((2, state_rows, LANES), F32), pltpu.VMEM((t, LANES), F32)],
        compiler_params=_cp(("parallel", "parallel"), 56), name="gdn_mixer",
    )(u, u, u, u, u, conv_w, conv_w, conv_w, par, norm_w.reshape(1, LANES))


def _hgrn_body(q_ref, ff_ref, fb_ref, i_ref, g_ref, lbp_ref, nw_ref, o_ref,
               qe_s, ke_s, kk_s, egl_s, v_s, n_s, st_s, oacc, *, n_lat, n_ctx, layer):
    t = n_lat + n_ctx
    e = jnp.exp(lbp_ref[...] - jnp.max(lbp_ref[...], axis=0, keepdims=True))
    lb = jnp.sum(e[1:layer + 1, :], axis=0, keepdims=True) / jnp.sum(e, axis=0, keepdims=True)
    q = _silu(q_ref[0].astype(F32))
    v_s[...] = i_ref[0]
    for d, f_ref in enumerate((ff_ref, fb_ref)):
        f = f_ref[0].astype(F32)
        sig = _sigmoid(f)
        logf = jnp.log(lb + (1.0 - lb) * sig)
        kin = (1.0 - lb) * (1.0 - sig)
        pre = _chunk_scan(logf, False)
        suf = _chunk_scan(logf, True)
        gc = suf if d else pre
        glast = pre + suf - logf
        qe_s[d] = (q * jnp.exp(gc)).astype(BF16)
        ke_s[d] = (kin * jnp.exp(-gc)).astype(BF16)
        kk_s[d] = (kin * jnp.exp(glast - gc)).astype(BF16)
        egl_s[d] = jnp.exp(glast)

    ii = _iota((SUPER, SUPER), 0)
    jj = _iota((SUPER, SUPER), 1)
    same = (ii >> 6) == (jj >> 6)
    per = SUPER // CHUNK
    unroll = _intra_unroll(t // SUPER)

    def chunk_rows(sc, c4):
        rows = pl.ds(pl.multiple_of(sc * SUPER + c4 * CHUNK, CHUNK), CHUNK)
        crow = pl.ds(pl.multiple_of((sc * per + c4) * LANES, LANES), LANES)
        return rows, crow

    def intra(it, carry):
        pairs = [(kq, d) for kq in range(unroll) for d in range(2)]
        rows = [pl.ds(pl.multiple_of((it * unroll + kq) * SUPER, SUPER), SUPER) for kq in range(unroll)]
        incl = jnp.stack([same & ((jj >= ii) if d else (jj <= ii)) for _, d in pairs])
        at = jnp.where(incl, _bmm_nt(jnp.stack([qe_s[d, rows[kq], :] for kq, d in pairs]),
                                     jnp.stack([ke_s[d, rows[kq], :] for kq, d in pairs])), 0.0)
        part = _bmm(at, jnp.stack([v_s[rows[kq], :] for kq, _ in pairs]))
        for kq in range(unroll):
            oacc[rows[kq], :] = part[2 * kq] + part[2 * kq + 1]
            for d in range(2):
                for c4 in range(per):
                    r64, crow = chunk_rows(it * unroll + kq, c4)
                    n_s[d, crow, :] = _mm_tn(v_s[r64, :], kk_s[d, r64, :])
        return carry

    lax.fori_loop(0, t // SUPER // unroll, intra, 0)
    nl, nc = n_lat // CHUNK, n_ctx // CHUNK

    def scan(s, states):
        new = []
        for d in range(2):
            c = _chunk_order(s, nl, nc, d)
            crow = pl.ds(pl.multiple_of(c * LANES, LANES), LANES)
            st_s[d, crow, :] = states[d].astype(BF16)
            new.append(states[d] * egl_s[d, pl.ds(c * CHUNK, 1), :] + n_s[d, crow, :])
        return tuple(new)

    zero = jnp.zeros((LANES, HG_DK), F32)
    lax.fori_loop(0, nl + nc, scan, (zero, zero))

    def inter(it, carry):
        for kq in range(unroll):
            sc = it * unroll + kq
            for c4 in range(per):
                r64, crow = chunk_rows(sc, c4)
                oacc[r64, :] += (_mm_nt(qe_s[0, r64, :], st_s[0, crow, :]) + _mm_nt(qe_s[1, r64, :], st_s[1, crow, :]))
        return carry

    lax.fori_loop(0, t // SUPER // unroll, inter, 0)
    o_ref[0] = _gated_rms(oacc[...], nw_ref[...], g_ref[0].astype(F32))


def _hgrn_mixer(u, n_lat, n_ctx, hg_lb, norm_w, layer):
    bsz, t, n5 = u.shape
    d = n5 // 5
    nh = d // HG_DK
    depth = hg_lb.shape[0]
    assert t % SUPER == 0 and n_lat % SUPER == 0
    body = functools.partial(_hgrn_body, n_lat=n_lat, n_ctx=n_ctx, layer=layer)
    sec = lambda s: pl.BlockSpec((1, t, LANES), lambda b, h: (b, 0, s * nh + h))
    both = lambda dt: pltpu.VMEM((2, t, LANES), dt)
    state_rows = (t // CHUNK) * LANES
    return pl.pallas_call(
        body, out_shape=S((bsz, t, d), F32), grid=(bsz, nh),
        in_specs=[sec(0), sec(1), sec(2), sec(3), sec(4),
                  pl.BlockSpec((depth, LANES), lambda b, h: (0, h)),
                  pl.BlockSpec((1, LANES), lambda b, h: (0, 0))],
        out_specs=pl.BlockSpec((1, t, LANES), lambda b, h: (b, 0, h)),
        scratch_shapes=[both(BF16), both(BF16), both(BF16), both(F32), pltpu.VMEM((t, LANES), BF16),
                        pltpu.VMEM((2, state_rows, LANES), F32), pltpu.VMEM((2, state_rows, LANES), BF16),
                        pltpu.VMEM((t, LANES), F32)],
        compiler_params=_cp(("parallel", "parallel")), name="hgrn2_mixer",
    )(u, u, u, u, u, hg_lb, norm_w.reshape(1, LANES))


def _norm_router_body(x_ref, nw_ref, sh_ref, sc_ref, rw_ref, h_ref, lg_ref, *, n_lat, n_b, tm):
    b, i = pl.program_id(0), pl.program_id(1)
    h = _norm_mod(x_ref[0], nw_ref[...], sh_ref, sc_ref, b, i * tm, n_lat, n_b)
    hi = h.astype(BF16)
    lo = (h - hi.astype(F32)).astype(BF16)
    h_ref[0] = hi
    dot = functools.partial(jnp.dot, preferred_element_type=F32)
    lg_ref[0] = dot(hi, rw_ref[0]) + (dot(hi, rw_ref[1]) + dot(lo, rw_ref[0]))


def _norm_router(xs, norm_w, mod, router_p, n_lat):
    bsz, t, d = xs.shape
    tm = t // 3 if t % 3 == 0 and (t // 3) % 16 == 0 else t
    body = functools.partial(_norm_router_body, n_lat=n_lat, n_b=bsz, tm=tm)
    return pl.pallas_call(
        body, out_shape=(S((bsz, t, d), BF16), S((bsz, t, LANES), F32)), grid=(bsz, t // tm),
        in_specs=[pl.BlockSpec((1, tm, d), lambda b, i: (b, i, 0)),
                  pl.BlockSpec((1, d), lambda b, i: (0, 0)),
                  pl.BlockSpec((MOD_ROWS, d), lambda b, i: (0, 3)),
                  pl.BlockSpec((MOD_ROWS, d), lambda b, i: (0, 4)),
                  pl.BlockSpec((2, d, LANES), lambda b, i: (0, 0, 0))],
        out_specs=(pl.BlockSpec((1, tm, d), lambda b, i: (b, i, 0)),
                   pl.BlockSpec((1, tm, LANES), lambda b, i: (b, i, 0))),
        compiler_params=_cp(("parallel", "parallel")), name="moe_norm_router",
    )(xs, norm_w.reshape(1, d), mod, mod, router_p)


def _excl_count_lanes(x):
    n = x.shape[1]
    blk = min(n, 256)
    tri = jnp.where(_iota((blk, blk), 0) < _iota((blk, blk), 1), 1.0, 0.0).astype(BF16)
    run = jnp.zeros((x.shape[0], 1), F32)
    outs = []
    for r in range(n // blk):
        xb = x[:, r * blk:(r + 1) * blk]
        outs.append(jnp.dot(xb.astype(BF16), tri, preferred_element_type=F32) + run)
        run = run + jnp.sum(xb, axis=1, keepdims=True)
    return jnp.concatenate(outs, axis=1) if len(outs) > 1 else outs[0]


def _topcap_slots(groups):
    def count_ge(v, thr):
        return jnp.sum(jnp.where(v >= thr, 1.0, 0.0), axis=1, keepdims=True)

    def bisect(_, carry):
        new = []
        for (v, cap), (lo, hi) in zip(groups, carry):
            mid = jnp.sqrt(jnp.maximum(lo, 1e-37)) * jnp.sqrt(hi)
            ok = count_ge(v, mid) >= float(cap)
            new.append((jnp.where(ok, mid, lo), jnp.where(ok, hi, mid)))
        return tuple(new)

    init = tuple((jnp.zeros((v.shape[0], 1), F32), jnp.full((v.shape[0], 1), 2.0, F32)) for v, _ in groups)
    brackets = lax.fori_loop(0, 34, bisect, init)
    codes = []
    for (v, cap), (lo, hi) in zip(groups, brackets):
        thr, found, upper = lo, jnp.zeros_like(lo), hi
        for _ in range(4):
            m = jnp.max(jnp.where(v < upper, v, -1.0), axis=1, keepdims=True)
            ok = jnp.where(count_ge(v, m) >= float(cap), 1.0, 0.0) * (1.0 - found)
            thr = jnp.where(ok > 0, m, thr)
            found = jnp.maximum(found, ok)
            upper = jnp.where(found > 0, upper, m)
        gt = jnp.where(v > thr, 1.0, 0.0)
        eq = jnp.where(v == thr, 1.0, 0.0)
        need = float(cap) - jnp.sum(gt, axis=1, keepdims=True)
        sel = jnp.maximum(gt, eq * jnp.where(_excl_count_lanes(eq) < need, 1.0, 0.0))
        codes.append(jnp.where(sel > 0, _excl_count_lanes(sel), -1.0))
    return codes


def _route_body(lg_ref, aff_ref, code_ref, codet_ref, *, n_lat, cap_l, cap_c):
    t = lg_ref.shape[1]
    lane_ok = _iota((1, LANES), 1) < N_EXPERTS
    lg = jnp.where(lane_ok, lg_ref[0], NEG_INF)
    e = jnp.exp(lg - jnp.max(lg, axis=-1, keepdims=True))
    aff = e / jnp.sum(e, axis=-1, keepdims=True)
    aff_ref[0] = aff
    aff_t = aff.T[0:N_EXPERTS, :]
    code_l, code_c = _topcap_slots([(aff_t[:, 0:n_lat], cap_l), (aff_t[:, n_lat:], cap_c)])
    code_t = jnp.concatenate([code_l, code_c], axis=1)
    codet_ref[0] = code_t
    code_ref[0] = jnp.concatenate([code_t, jnp.full((LANES - N_EXPERTS, t), -1.0, F32)], axis=0).T


def _route(logits, n_lat, cap_l, cap_c):
    bsz, t, _ = logits.shape
    body = functools.partial(_route_body, n_lat=n_lat, cap_l=cap_l, cap_c=cap_c)
    blk = pl.BlockSpec((1, t, LANES), lambda b: (b, 0, 0))
    return pl.pallas_call(
        body, out_shape=(S((bsz, t, LANES), F32), S((bsz, t, LANES), F32), S((bsz, N_EXPERTS, t), F32)), grid=(bsz,),
        in_specs=[blk], out_specs=(blk, blk, pl.BlockSpec((1, N_EXPERTS, t), lambda b: (b, 0, 0))),
        compiler_params=_cp(("parallel",)), name="moe_route",
    )(logits)


def _gather_body(codet_ref, h_ref, xl_ref, xc_ref, *, n_lat, n_ctx, cap_l, cap_c, grp):
    hl = h_ref[0, 0:n_lat, :]
    hc = h_ref[0, n_lat:, :]
    il = _iota((cap_l, n_lat), 0).astype(F32)
    ic = _iota((cap_c, n_ctx), 0).astype(F32)
    for e0 in range(0, N_EXPERTS, grp):
        onehot = jnp.concatenate([jnp.where(il == codet_ref[0, e:e + 1, 0:n_lat], 1.0, 0.0).astype(BF16)
                                  for e in range(e0, e0 + grp)], axis=0)
        xe = jnp.dot(onehot, hl, preferred_element_type=F32).astype(BF16)
        for r in range(grp):
            xl_ref[0, e0 + r] = xe[r * cap_l:(r + 1) * cap_l]
    onehot = jnp.concatenate([jnp.where(ic == codet_ref[0, e:e + 1, n_lat:], 1.0, 0.0).astype(BF16)
                              for e in range(N_EXPERTS)], axis=0)
    xc_ref[0] = jnp.dot(onehot, hc, preferred_element_type=F32).astype(BF16)


def _gather(codet, h, n_lat, cap_l, cap_c):
    bsz, t, d = h.shape
    body = functools.partial(_gather_body, n_lat=n_lat, n_ctx=t - n_lat, cap_l=cap_l, cap_c=cap_c, grp=4)
    return pl.pallas_call(
        body, out_shape=(S((bsz, N_EXPERTS, cap_l, d), BF16), S((bsz, N_EXPERTS * cap_c, d), BF16)), grid=(bsz,),
        in_specs=[pl.BlockSpec((1, N_EXPERTS, t), lambda b: (b, 0, 0)), pl.BlockSpec((1, t, d), lambda b: (b, 0, 0))],
        out_specs=(pl.BlockSpec((1, N_EXPERTS, cap_l, d), lambda b: (b, 0, 0, 0)),
                   pl.BlockSpec((1, N_EXPERTS * cap_c, d), lambda b: (b, 0, 0))),
        compiler_params=_cp(("parallel",)), name="moe_gather",
    )(codet, h)


def _ffn_body(xl_ref, xc_ref, wg_ref, wu_ref, wd_ref, yl_ref, yc_ref, x_scr, acc, *, nb, cap_l, cap_c):
    f = pl.program_id(1)
    d = x_scr.shape[1]

    @pl.when(f == 0)
    def _():
        x_scr[0:nb * cap_l, :] = xl_ref[:, 0].reshape(nb * cap_l, d)
        x_scr[nb * cap_l:, :] = xc_ref[:, 0].reshape(nb * cap_c, d)
        acc[...] = jnp.zeros_like(acc)

    x = x_scr[...]
    a = jnp.dot(x, wg_ref[0, 0].astype(BF16), preferred_element_type=F32)
    u = jnp.dot(x, wu_ref[0, 0].astype(BF16), preferred_element_type=F32)
    acc[...] += _mm(_silu(a) * u, wd_ref[0, 0])

    @pl.when(f == pl.num_programs(1) - 1)
    def _():
        y = acc[...].astype(BF16)
        yl_ref[:, 0] = y[0:nb * cap_l].reshape(nb, cap_l, d)
        yc_ref[:, 0] = y[nb * cap_l:].reshape(nb, cap_c, d)


def _expert_ffn(xl, xc, w_gate, w_up, w_down, layer):
    bsz, ne, cap_l, d = xl.shape
    cap_c = xc.shape[2]
    ffd = w_gate.shape[-1]
    tf = 256
    rows = bsz * (cap_l + cap_c)
    body = functools.partial(_ffn_body, nb=bsz, cap_l=cap_l, cap_c=cap_c)
    return pl.pallas_call(
        body, out_shape=(S(xl.shape, BF16), S(xc.shape, BF16)), grid=(ne, ffd // tf),
        in_specs=[pl.BlockSpec((bsz, 1, cap_l, d), lambda e, f: (0, e, 0, 0)),
                  pl.BlockSpec((bsz, 1, cap_c, d), lambda e, f: (0, e, 0, 0)),
                  pl.BlockSpec((1, 1, d, tf), lambda e, f: (layer, e, 0, f)),
                  pl.BlockSpec((1, 1, d, tf), lambda e, f: (layer, e, 0, f)),
                  pl.BlockSpec((1, 1, tf, d), lambda e, f: (layer, e, f, 0))],
        out_specs=(pl.BlockSpec((bsz, 1, cap_l, d), lambda e, f: (0, e, 0, 0)),
                   pl.BlockSpec((bsz, 1, cap_c, d), lambda e, f: (0, e, 0, 0))),
        scratch_shapes=[pltpu.VMEM((rows, d), BF16), pltpu.VMEM((rows, d), F32)],
        compiler_params=_cp(("parallel", "arbitrary")), name="moe_expert_ffn",
    )(xl, xc, w_gate, w_up, w_down)


def _combine_body(code_ref, aff_ref, yl_ref, yc_ref, res_ref, gate_ref, o_ref, *, n_lat, n_b, cap_l, cap_c, tm):
    b, i = pl.program_id(0), pl.program_id(1)
    code = code_ref[0]
    aff = aff_ref[0]

    def scatter(cap, y):
        slot = _iota((tm, cap), 1).astype(F32)
        q = jnp.concatenate([jnp.where(code[:, e:e + 1] == slot, aff[:, e:e + 1], 0.0) for e in range(N_EXPERTS)], axis=1)
        return jnp.dot(q.astype(BF16), y, preferred_element_type=F32)

    @pl.when(i * tm < n_lat)
    def _():
        o_ref[0] = res_ref[0] + gate_ref[pl.ds(b, 1), :] * scatter(cap_l, yl_ref[0])

    @pl.when(i * tm >= n_lat)
    def _():
        o_ref[0] = res_ref[0] + gate_ref[n_b:n_b + 1, :] * scatter(cap_c, yc_ref[0])


def _combine(code, aff, yl, yc, res, mod, n_lat):
    bsz, t, d = res.shape
    cap_l, cap_c = yl.shape[2], yc.shape[2]
    tm = t - n_lat
    assert n_lat % tm == 0
    body = functools.partial(_combine_body, n_lat=n_lat, n_b=bsz, cap_l=cap_l, cap_c=cap_c, tm=tm)
    tok = lambda w: pl.BlockSpec((1, tm, w), lambda b, i: (b, i, 0))
    return pl.pallas_call(
        body, out_shape=S((bsz, t, d), F32), grid=(bsz, t // tm),
        in_specs=[tok(LANES), tok(LANES),
                  pl.BlockSpec((1, N_EXPERTS * cap_l, d), lambda b, i: (b, 0, 0)),
                  pl.BlockSpec((1, N_EXPERTS * cap_c, d), lambda b, i: (b, 0, 0)),
                  tok(d), pl.BlockSpec((MOD_ROWS, d), lambda b, i: (0, 5))],
        out_specs=tok(d),
        compiler_params=_cp(("parallel", "parallel")), name="moe_combine",
    )(code, aff, yl.reshape(bsz, N_EXPERTS * cap_l, d), yc.reshape(bsz, N_EXPERTS * cap_c, d), res, mod)


def _moe_layer(xs, norm_w, mod, router_p, w_gate, w_up, w_down, layer, n_lat):
    bsz, t, d = xs.shape
    n_ctx = t - n_lat
    cap_l = EC_CAPACITY * n_lat // N_EXPERTS
    cap_c = EC_CAPACITY * n_ctx // N_EXPERTS
    h, logits = _norm_router(xs, norm_w, mod, router_p, n_lat)
    aff, code, codet = _route(logits, n_lat, cap_l, cap_c)
    xl, xc = _gather(codet, h, n_lat, cap_l, cap_c)
    yl, yc = _expert_ffn(xl, xc.reshape(bsz, N_EXPERTS, cap_c, d), w_gate, w_up, w_down, layer)
    return _combine(code, aff, yl, yc, xs, mod, n_lat)


def _final_norm_body(x_ref, w_ref, o_ref):
    x = x_ref[0]
    o_ref[0] = x * lax.rsqrt(jnp.mean(x * x, axis=-1, keepdims=True) + EPS) * w_ref[...]


def _final_norm(xs, w, n_lat):
    bsz, _, d = xs.shape
    tm = 512 if n_lat % 512 == 0 else n_lat
    return pl.pallas_call(
        _final_norm_body, out_shape=S((bsz, n_lat, d), F32), grid=(bsz, n_lat // tm),
        in_specs=[pl.BlockSpec((1, tm, d), lambda b, i: (b, i, 0)), pl.BlockSpec((1, d), lambda b, i: (0, 0))],
        out_specs=pl.BlockSpec((1, tm, d), lambda b, i: (b, i, 0)),
        compiler_params=_cp(("parallel", "parallel")), name="final_norm",
    )(xs, w.reshape(1, d))


def _swa_group_columns():
    hd, rep = SW_HD, SW_REP
    cols = []
    for g in range(SW_HKV):
        cols += list(range(g * rep * hd, (g + 1) * rep * hd))
        cols += list(range(SW_HQ * hd + g * hd, SW_HQ * hd + (g + 1) * hd))
        cols += list(range((SW_HQ + SW_HKV) * hd + g * hd, (SW_HQ + SW_HKV) * hd + (g + 1) * hd))
    return np.asarray(cols, np.int32)


def kernel(x, c, ctx, c_ctx, ada_w, ada_b, norm1_w, norm2_w, final_norm_w, hy_w_in, hy_b_in, hy_short_w, hy_short_b, hy_ffn_w1, hy_ffn_b1, hy_ffn_w2, hy_ffn_b2, hy_ffn_w3, hy_sin_freq, hy_filter_bias, hy_w_out, hy_b_out, sw_w_in, sw_sink, sw_w_out, gd_w_in, gd_conv_w, gd_a_log, gd_dt_bias, gd_norm_w, gd_w_out, hg_w_in, hg_lb, hg_norm_w, hg_w_out, moe_router, moe_w_gate, moe_w_up, moe_w_down):
    bsz, n_lat, d = x.shape
    n_ctx = ctx.shape[1]
    depth = ada_w.shape[0]
    assert bsz < MOD_ROWS and n_lat % n_ctx == 0
    xs = jnp.concatenate([x, ctx], axis=1)
    c16 = jnp.zeros((MOD_ROWS, d), F32).at[:bsz].set(c).at[bsz].set(c_ctx)
    mod = _modulation(c16, ada_w, ada_b)
    zero_bias = jnp.zeros((d,), F32)
    gd_pad = (-gd_w_in.shape[1]) % LANES
    gd_w = jnp.pad(gd_w_in, ((0, 0), (0, gd_pad))).astype(BF16)
    sw_w = sw_w_in[:, _swa_group_columns()].astype(BF16)
    for layer in range(depth):
        m = mod[layer]
        nw = norm1_w[layer]
        kind = layer % 4
        if kind == 0:
            u = _in_proj(xs, nw, m, 0, 1, hy_w_in.astype(BF16), hy_b_in, n_lat, "hy_in_proj")
            y = _hyena_mixer(u, n_lat, n_ctx, hy_short_w, hy_short_b, hy_ffn_w1, hy_ffn_b1, hy_ffn_w2, hy_ffn_b2,
                             hy_ffn_w3, hy_sin_freq, hy_filter_bias)
            xs = _out_proj(y, hy_w_out.astype(BF16), hy_b_out, xs, m, 2, n_lat, "hy_out_proj")
        elif kind == 1:
            u = _in_proj(xs, nw, m, 0, 1, sw_w, jnp.zeros((sw_w.shape[1],), F32), n_lat, "sw_in_proj")
            y = _swa_mixer(u, n_lat, n_ctx, sw_sink)
            xs = _out_proj(y, sw_w_out.astype(BF16), zero_bias, xs, m, 2, n_lat, "sw_out_proj")
        elif kind == 2:
            u = _in_proj(xs, nw, m, 0, 1, gd_w, jnp.zeros((gd_w.shape[1],), F32), n_lat, "gd_in_proj")
            y = _gdn_mixer(u, n_lat, n_ctx, gd_conv_w, gd_a_log, gd_dt_bias, gd_norm_w)
            xs = _out_proj(y, gd_w_out.astype(BF16), zero_bias, xs, m, 2, n_lat, "gd_out_proj")
        else:
            u = _in_proj(xs, nw, m, 0, 1, hg_w_in.astype(BF16), jnp.zeros((hg_w_in.shape[1],), F32), n_lat, "hg_in_proj")
            y = _hgrn_mixer(u, n_lat, n_ctx, hg_lb, hg_norm_w, layer)
            xs = _out_proj(y, hg_w_out.astype(BF16), zero_bias, xs, m, 2, n_lat, "hg_out_proj")
        router_w = jnp.pad(moe_router[layer], ((0, 0), (0, LANES - N_EXPERTS)))
        router_hi = router_w.astype(BF16)
        router_p = jnp.stack([router_hi, (router_w - router_hi.astype(F32)).astype(BF16)])
        xs = _moe_layer(xs, norm2_w[layer], m, router_p, moe_w_gate, moe_w_up, moe_w_down, layer, n_lat)
    return _final_norm(xs, final_norm_w, n_lat)
```

```python
import functools
import math

import jax
import jax.numpy as jnp
import numpy as np
from jax import lax
from jax.experimental import pallas as pl
from jax.experimental.pallas import tpu as pltpu

F32 = jnp.float32
BF16 = jnp.bfloat16
HIGHEST = lax.Precision.HIGHEST
EPS = 1e-6
NEG_INF = -1e30
LANES = 128
BF16_ROWS = 16
MOD_ROWS = 16

GRID_W = 64
HY_ORDER = 2
HY_EMB = 33
HY_FAST_DECAY = 0.3
HY_SLOW_DECAY = 1.5
HY_TARGET = 1e-2
HY_SHIFT = 0.05
SW_HQ, SW_HKV, SW_HD, SW_WINDOW = 16, 4, 64, 128
SW_REP = SW_HQ // SW_HKV
ROPE_BASE = 10000.0
GD_H, GD_DK = 8, 128
HG_DK = 128
CHUNK = 64
SUPER = 256
GD_SUPER = 128
INTRA_ROWS = 768
DFT_ROWS = 512
N_EXPERTS = 16
EC_CAPACITY = 2

S = jax.ShapeDtypeStruct


def _cp(sem, vmem_mb=48):
    return pltpu.CompilerParams(dimension_semantics=sem, vmem_limit_bytes=vmem_mb * 2**20)


def _iota(shape, dim):
    return lax.broadcasted_iota(jnp.int32, shape, dim)


def _sigmoid(x):
    return 0.5 * jnp.tanh(0.5 * x) + 0.5


def _silu(x):
    return x * _sigmoid(x)


def _softplus(x):
    return jnp.maximum(x, 0.0) + jnp.log(1.0 + jnp.exp(-jnp.abs(x)))


def _mm(a, b):
    return jnp.dot(a.astype(BF16), b.astype(BF16), preferred_element_type=F32)


def _mm_nt(a, b):
    return lax.dot_general(a.astype(BF16), b.astype(BF16), (((1,), (1,)), ((), ())), preferred_element_type=F32)


def _mm_tn(a, b):
    return lax.dot_general(a.astype(BF16), b.astype(BF16), (((0,), (0,)), ((), ())), preferred_element_type=F32)


def _mm_f32(a, b):
    return jnp.dot(a, b, precision=HIGHEST, preferred_element_type=F32)


def _mod_body(c_ref, w_ref, b_ref, o_ref):
    o_ref[0] = _mm_f32(_silu(c_ref[...]), w_ref[0]) + b_ref[0]


def _modulation(c16, ada_w, ada_b):
    depth, d, n = ada_w.shape
    tn = 1024
    return pl.pallas_call(
        _mod_body, out_shape=S((depth, MOD_ROWS, n), F32), grid=(depth, n // tn),
        in_specs=[pl.BlockSpec((MOD_ROWS, d), lambda l, j: (0, 0)),
                  pl.BlockSpec((1, d, tn), lambda l, j: (l, 0, j)),
                  pl.BlockSpec((1, 1, tn), lambda l, j: (l, 0, j))],
        out_specs=pl.BlockSpec((1, MOD_ROWS, tn), lambda l, j: (l, 0, j)),
        compiler_params=_cp(("parallel", "parallel")), name="adaln_mod",
    )(c16, ada_w, ada_b.reshape(depth, 1, n))


def _norm_mod(x, nw, shift_ref, scale_ref, b, row0, n_lat, n_b):
    tm = x.shape[0]
    y = x * lax.rsqrt(jnp.mean(x * x, axis=-1, keepdims=True) + EPS) * nw
    is_ctx = (row0 + _iota((tm, 1), 0)) >= n_lat
    shift = jnp.where(is_ctx, shift_ref[n_b:n_b + 1, :], shift_ref[pl.ds(b, 1), :])
    scale = jnp.where(is_ctx, scale_ref[n_b:n_b + 1, :], scale_ref[pl.ds(b, 1), :])
    return y * (1.0 + scale) + shift


def _row_gate(gate_ref, b, row0, tm, n_lat, n_b):
    is_ctx = (row0 + _iota((tm, 1), 0)) >= n_lat
    return jnp.where(is_ctx, gate_ref[n_b:n_b + 1, :], gate_ref[pl.ds(b, 1), :])


def _in_proj_body(x_ref, nw_ref, sh_ref, sc_ref, w_ref, bias_ref, o_ref, h_scr, *, n_lat, n_b, tm):
    b, i = pl.program_id(0), pl.program_id(1)

    @pl.when(pl.program_id(2) == 0)
    def _():
        h_scr[...] = _norm_mod(x_ref[0], nw_ref[...], sh_ref, sc_ref, b, i * tm, n_lat, n_b).astype(BF16)

    o_ref[0] = (jnp.dot(h_scr[...], w_ref[...], preferred_element_type=F32) + bias_ref[...]).astype(o_ref.dtype)


def _in_proj(xs, norm_w, mod, shift_idx, scale_idx, w, bias, n_lat, name):
    bsz, t, d = xs.shape
    n = w.shape[1]
    tm = t
    tn = 512 if n % 512 == 0 else (384 if n % 384 == 0 else 128)
    body = functools.partial(_in_proj_body, n_lat=n_lat, n_b=bsz, tm=tm)
    return pl.pallas_call(
        body, out_shape=S((bsz, t, n), BF16), grid=(bsz, t // tm, n // tn),
        in_specs=[pl.BlockSpec((1, tm, d), lambda b, i, j: (b, i, 0)),
                  pl.BlockSpec((1, d), lambda b, i, j: (0, 0)),
                  pl.BlockSpec((MOD_ROWS, d), lambda b, i, j: (0, shift_idx)),
                  pl.BlockSpec((MOD_ROWS, d), lambda b, i, j: (0, scale_idx)),
                  pl.BlockSpec((d, tn), lambda b, i, j: (0, j)),
                  pl.BlockSpec((1, tn), lambda b, i, j: (0, j))],
        out_specs=pl.BlockSpec((1, tm, tn), lambda b, i, j: (b, i, j)),
        scratch_shapes=[pltpu.VMEM((tm, d), BF16)],
        compiler_params=_cp(("parallel", "parallel", "arbitrary")), name=name,
    )(xs, norm_w.reshape(1, d), mod, mod, w, bias.reshape(1, n))


def _out_proj_body(y_ref, w_ref, bias_ref, res_ref, gate_ref, nw_ref, sh_ref, sc_ref, rw_ref, o_ref, h_ref, lg_ref,
                   *, n_lat, n_b, tm):
    b, i = pl.program_id(0), pl.program_id(1)
    y = jnp.dot(y_ref[0].astype(BF16), w_ref[...], preferred_element_type=F32) + bias_ref[...]
    x = res_ref[0] + _row_gate(gate_ref, b, i * tm, tm, n_lat, n_b) * y
    o_ref[0] = x
    h = _norm_mod(x, nw_ref[...], sh_ref, sc_ref, b, i * tm, n_lat, n_b)
    hi = h.astype(BF16)
    lo = (h - hi.astype(F32)).astype(BF16)
    h_ref[0] = hi
    dot = functools.partial(jnp.dot, preferred_element_type=F32)
    lg_ref[0] = dot(hi, rw_ref[0]) + (dot(hi, rw_ref[1]) + dot(lo, rw_ref[0]))


def _out_proj(y, w, bias, res, mod, n_lat, norm2_w, router_p, name):
    bsz, t, dy = y.shape
    d = w.shape[1]
    tm = t // 3 if t % 3 == 0 and (t // 3) % BF16_ROWS == 0 else t
    body = functools.partial(_out_proj_body, n_lat=n_lat, n_b=bsz, tm=tm)
    tok = lambda width: pl.BlockSpec((1, tm, width), lambda b, i: (b, i, 0))
    modc = lambda idx: pl.BlockSpec((MOD_ROWS, d), lambda b, i: (0, idx))
    return pl.pallas_call(
        body, out_shape=(S((bsz, t, d), F32), S((bsz, t, d), BF16), S((bsz, t, LANES), F32)), grid=(bsz, t // tm),
        in_specs=[tok(dy), pl.BlockSpec((dy, d), lambda b, i: (0, 0)), pl.BlockSpec((1, d), lambda b, i: (0, 0)),
                  tok(d), modc(2), pl.BlockSpec((1, d), lambda b, i: (0, 0)), modc(3), modc(4),
                  pl.BlockSpec((2, d, LANES), lambda b, i: (0, 0, 0))],
        out_specs=(tok(d), tok(d), tok(LANES)),
        compiler_params=_cp(("parallel", "parallel")), name=name,
    )(y, w, bias.reshape(1, d), res, mod, norm2_w.reshape(1, d), mod, mod, router_p)


def _dft_tables(n):
    k = np.arange(n, dtype=np.int64)
    ang = (np.outer(k, k) % (2 * n)).astype(np.float64) * (math.pi / n)
    c = np.cos(ang)
    s = -np.sin(ang)
    s[0, :] = 1.0 - 2.0 * (k % 2)
    return jnp.asarray(np.stack([c, s]), dtype=BF16)


def _hy_positional(n):
    t = np.linspace(0.0, 1.0, n)[:, None]
    bands = (HY_EMB - 1) // 2
    w = (2.0 * math.pi * np.arange(n) / n)[:, None]
    f = np.linspace(1e-4, bands - 1, bands)[None, :]
    z = np.concatenate([t, np.cos(f * w), -np.sin(f * w)], axis=-1)
    zp = np.zeros((n, LANES), np.float32)
    zp[:, :HY_EMB] = z
    return jnp.asarray(zp), jnp.asarray(t.astype(np.float32))


def _hy_mlp_body(z_ref, w1_ref, b1_ref, w2_ref, b2_ref, fr_ref, h_ref):
    h = jnp.sin(fr_ref[0:1, :] * (_mm_f32(z_ref[...], w1_ref[...]) + b1_ref[...]))
    h_ref[...] = jnp.sin(fr_ref[1:2, :] * (_mm_f32(h, w2_ref[...]) + b2_ref[...]))


def _hy_filter_body(h_ref, w3f_ref, w3b_ref, t_ref, dl_ref, cs_ref, p1_ref, p2_ref, pn_ref, *, n):
    h = h_ref[...]
    win = jnp.exp(-t_ref[...] * dl_ref[...]) + HY_SHIFT
    hf = _mm_f32(h, w3f_ref[...]) * win
    hb = _mm_f32(h, w3b_ref[...]) * win
    row = _iota((n, 1), 0)
    hb = jnp.where(row == 0, 0.0, hb)
    hs, hd = hf + hb, hf - hb
    alt = (1 - 2 * (row & 1)).astype(F32)
    knyq = jnp.sum(alt * hs, axis=0, keepdims=True)
    pn_ref[0] = jnp.broadcast_to(knyq * (0.5 / n), pn_ref.shape[1:])
    hs, hd = hs.astype(BF16), hd.astype(BF16)
    rc = min(n, DFT_ROWS)
    for r0 in range(0, n, rc):
        rows = slice(r0, r0 + rc)
        wgt = jnp.where(row[rows] == 0, 0.5 / n, 1.0 / n)
        p1_ref[0, rows, :] = _mm(cs_ref[0, rows, :], hs) * wgt
        p2_ref[0, rows, :] = jnp.where(row[rows] == 0, 0.0, _mm(cs_ref[1, rows, :], hd) * wgt)


def _hy_filter_spectra(n, w1p, b1, w2, b2, freq, w3, cs, d, td):
    zpad, tcol = _hy_positional(n)
    max_decay = math.log(HY_TARGET) / HY_FAST_DECAY
    min_decay = math.log(HY_TARGET) / HY_SLOW_DECAY
    deltas = jnp.asarray(np.abs(np.linspace(min_decay, max_decay, d)), dtype=F32)[None, :]
    nd = d // td
    ff = w1p.shape[1]
    h = pl.pallas_call(_hy_mlp_body, out_shape=S((n, ff), F32), name=f"hy_filter_mlp_{n}")(
        zpad, w1p, b1.reshape(1, ff), w2, b2.reshape(1, ff), freq)
    body = functools.partial(_hy_filter_body, n=n)
    const = lambda o, j: (0, 0)
    return pl.pallas_call(
        body, out_shape=(S((HY_ORDER, n, d), F32), S((HY_ORDER, n, d), F32), S((HY_ORDER, 8, d), F32)),
        grid=(HY_ORDER, nd),
        in_specs=[pl.BlockSpec((n, ff), const),
                  pl.BlockSpec((ff, td), lambda o, j: (0, (2 * o) * nd + j)),
                  pl.BlockSpec((ff, td), lambda o, j: (0, (2 * o + 1) * nd + j)),
                  pl.BlockSpec((n, 1), const), pl.BlockSpec((1, td), lambda o, j: (0, j)),
                  pl.BlockSpec((2, n, n), lambda o, j: (0, 0, 0), pipeline_mode=pl.Buffered(1))],
        out_specs=(pl.BlockSpec((1, n, td), lambda o, j: (o, 0, j)),
                   pl.BlockSpec((1, n, td), lambda o, j: (o, 0, j)),
                   pl.BlockSpec((1, 8, td), lambda o, j: (o, 0, j))),
        compiler_params=_cp(("parallel", "parallel")), name=f"hy_filter_{n}",
    )(h, w3, w3, tcol, deltas, cs)


def _short_conv(x, w_ref, b_ref):
    n = x.shape[0]
    row = _iota((n, 1), 0)
    xp = jnp.where(row == 0, 0.0, pltpu.roll(x, 1, 0))
    xn = jnp.where(row == n - 1, 0.0, pltpu.roll(x, n - 1, 0))
    return w_ref[0:1, :] * xp + w_ref[1:2, :] * x + w_ref[2:3, :] * xn + b_ref[...]


def _short_conv_rows(x_ref, w_ref, b_ref, r0, rc, n):
    x = x_ref[0, r0:r0 + rc, :].astype(F32)
    row = _iota((rc, 1), 0)
    pk = BF16_ROWS
    prev = x_ref[0, r0 - pk:r0, :].astype(F32)[pk - 1:pk] if r0 > 0 else 0.0
    nxt = x_ref[0, r0 + rc:r0 + rc + pk, :].astype(F32)[0:1] if r0 + rc < n else 0.0
    xp = jnp.where(row == 0, prev, pltpu.roll(x, 1, 0))
    xn = jnp.where(row == rc - 1, nxt, pltpu.roll(x, rc - 1, 0))
    return w_ref[0:1, :] * xp + w_ref[1:2, :] * x + w_ref[2:3, :] * xn + b_ref[...]


def _hy_conv_body(a_ref, g_ref, wa_ref, ba_ref, wg_ref, bg_ref, cs_ref, p1_ref, p2_ref, pn_ref, fb_ref, o_ref,
                  a_s, ab_s, yr_s, yi_s, *, n, conv_a):
    a = a_ref[0].astype(F32)
    if conv_a:
        a = _short_conv(a, wa_ref, ba_ref)
    a_s[...] = a
    ab_s[...] = a.astype(BF16)
    rc = min(n, DFT_ROWS)
    row = _iota((rc, 1), 0)
    alt = (1 - 2 * (row & 1)).astype(F32)
    asum = None
    for r0 in range(0, n, rc):
        rows = slice(r0, r0 + rc)
        re = _mm(cs_ref[0, rows, :], ab_s[...])
        im = _mm(cs_ref[1, rows, :], ab_s[...])
        p1, p2 = p1_ref[0, rows, :], p2_ref[0, rows, :]
        yr_s[rows, :] = (re * p1 - im * p2).astype(BF16)
        yi = re * p2 + im * (jnp.where(row == 0, pn_ref[0, 0:1, :], p1) if r0 == 0 else p1)
        yi_s[rows, :] = yi.astype(BF16)
        part = jnp.sum(alt * yi, axis=0, keepdims=True)
        asum = part if asum is None else asum + part
        if r0 == 0:
            nyq = yi[0:1, :]
    for r0 in range(0, n, rc):
        rows = slice(r0, r0 + rc)
        y = _mm(cs_ref[0, rows, :], yr_s[...]) + _mm(cs_ref[1, rows, :], yi_s[...])
        y = y + alt * nyq
        if r0 == 0:
            y = y - jnp.where(row == 0, asum, 0.0)
        g = _short_conv_rows(g_ref, wg_ref, bg_ref, r0, rc, n)
        o_ref[0, rows, :] = g * (y + a_s[rows, :] * fb_ref[0])


def _hy_conv(a, a_col0, a_row, g, g_col0, g_row, n, conv_w, conv_b, cs, p1, p2, pn, fbias, order, conv_a, d, td, name):
    bsz = a.shape[0]
    nd = d // td
    body = functools.partial(_hy_conv_body, n=n, conv_a=conv_a)
    cw = conv_w
    cb = conv_b.reshape(1, -1)
    return pl.pallas_call(
        body, out_shape=S((bsz, n, d), F32), grid=(nd, bsz),
        in_specs=[pl.BlockSpec((1, n, td), lambda j, b: (b, a_row, a_col0 * nd + j)),
                  pl.BlockSpec((1, n, td), lambda j, b: (b, g_row, g_col0 * nd + j)),
                  pl.BlockSpec((3, td), lambda j, b: (0, a_col0 * nd + j if conv_a else j)),
                  pl.BlockSpec((1, td), lambda j, b: (0, a_col0 * nd + j if conv_a else j)),
                  pl.BlockSpec((3, td), lambda j, b: (0, g_col0 * nd + j)),
                  pl.BlockSpec((1, td), lambda j, b: (0, g_col0 * nd + j)),
                  pl.BlockSpec((2, n, n), lambda j, b: (0, 0, 0), pipeline_mode=pl.Buffered(1)),
                  pl.BlockSpec((1, n, td), lambda j, b: (order, 0, j), pipeline_mode=pl.Buffered(1)),
                  pl.BlockSpec((1, n, td), lambda j, b: (order, 0, j), pipeline_mode=pl.Buffered(1)),
                  pl.BlockSpec((1, 8, td), lambda j, b: (order, 0, j)),
                  pl.BlockSpec((1, 1, td), lambda j, b: (order, 0, j))],
        out_specs=pl.BlockSpec((1, n, td), lambda j, b: (b, 0, j)),
        scratch_shapes=[pltpu.VMEM((n, td), F32), pltpu.VMEM((n, td), BF16), pltpu.VMEM((n, td), BF16),
                        pltpu.VMEM((n, td), BF16)],
        compiler_params=_cp(("parallel", "parallel"), 56), name=name,
    )(a, g, cw, cb, cw, cb, cs, p1, p2, pn, fbias.reshape(HY_ORDER, 1, d))


def _hyena_mixer(u, n_lat, n_ctx, short_w, short_b, w1, b1, w2, b2, w3, freq, fbias):
    d = u.shape[2] // 3
    ff = w1.shape[1]
    w1p = jnp.zeros((LANES, ff), F32).at[:w1.shape[0]].set(w1)
    outs = []
    for n, row in ((n_lat, 0), (n_ctx, n_lat // n_ctx)):
        td = 256 if n > 512 else 512
        cs = _dft_tables(n)
        p1, p2, pn = _hy_filter_spectra(n, w1p, b1, w2, b2, freq, w3, cs, d, td)
        z1 = _hy_conv(u, 0, row, u, 1, row, n, short_w, short_b, cs, p1, p2, pn, fbias, 0, True, d, td, f"hy_conv1_{n}")
        z2 = _hy_conv(z1, 0, 0, u, 2, row, n, short_w, short_b, cs, p1, p2, pn, fbias, 1, False, d, td, f"hy_conv2_{n}")
        outs.append(z2)
    return jnp.concatenate(outs, axis=1)


def _rope_tables(n_lat, width, rot_heads):
    hd = SW_HD
    rows = n_lat // GRID_W
    row = np.repeat(np.arange(rows, dtype=np.float64), GRID_W)
    col = np.tile(np.arange(GRID_W, dtype=np.float64), rows)
    nf = hd // 4
    inv = ROPE_BASE ** (-np.arange(nf, dtype=np.float64) / nf)
    ang = np.concatenate([row[:, None] * inv, col[:, None] * inv], axis=-1)
    cos, sin = np.cos(ang), np.sin(ang)
    zero = np.zeros_like(sin)
    c = np.ones((n_lat, width), np.float32)
    sa = np.zeros((n_lat, width), np.float32)
    sb = np.zeros((n_lat, width), np.float32)
    for h in range(rot_heads):
        c[:, h * hd:(h + 1) * hd] = np.concatenate([cos, cos], axis=-1)
        sa[:, h * hd:(h + 1) * hd] = np.concatenate([-sin, zero], axis=-1)
        sb[:, h * hd:(h + 1) * hd] = np.concatenate([zero, sin], axis=-1)
    return jnp.asarray(c), jnp.asarray(sa), jnp.asarray(sb)


def _rope(x, c, sa, sb):
    w = x.shape[1]
    half = SW_HD // 2
    return x * c + pltpu.roll(x, w - half, 1) * sa + pltpu.roll(x, half, 1) * sb


def _sink_attend(q, kvs, sink):
    ss = []
    m = None
    for k, _, mask in kvs:
        s = _mm_nt(q, k)
        if mask is not None:
            s = jnp.where(mask, s, NEG_INF)
        ss.append(s)
        sm = jnp.max(s, axis=-1, keepdims=True)
        m = sm if m is None else jnp.maximum(m, sm)
    m = jnp.maximum(m, sink)
    den = jnp.exp(sink - m)
    o = None
    for s, (_, v, _) in zip(ss, kvs):
        p = jnp.exp(s - m)
        den = den + jnp.sum(p, axis=-1, keepdims=True)
        pv = _mm(p, v)
        o = pv if o is None else o + pv
    return o / den


def _swa_body(u_ref, cq_ref, saq_ref, sbq_ref, ck_ref, sak_ref, sbk_ref, sink_ref, o_ref, kv_scr, s_a, s_b,
              *, n_lat, n_ctx, blk):
    g = pl.program_id(1)
    hd, rep = SW_HD, SW_REP
    qw = rep * hd
    scale = hd ** -0.5
    span = 3 * blk
    nb = n_lat // blk
    kv_scr[0:n_lat, :] = _rope(u_ref[0, 0:n_lat, qw:qw + 2 * hd].astype(F32), ck_ref[...], sak_ref[...],
                               sbk_ref[...]).astype(BF16)
    kv_scr[n_lat:, :] = u_ref[0, n_lat:, qw:qw + 2 * hd]
    kc = kv_scr[n_lat:, 0:hd]
    vc = kv_scr[n_lat:, hd:2 * hd]

    def key_start(i):
        return pl.multiple_of(jnp.clip((i - 1) * blk, 0, n_lat - span), blk)

    def scores(i, s_ref):
        rows = pl.ds(pl.multiple_of(i * blk, blk), blk)
        q = _rope(u_ref[0, rows, 0:qw].astype(F32), cq_ref[rows, :], saq_ref[rows, :], sbq_ref[rows, :]) * scale
        q = q.astype(BF16)
        kl = kv_scr[pl.ds(key_start(i), span), 0:hd]
        for r in range(rep):
            s_ref[r, :, 0:n_ctx] = _mm_nt(q[:, r * hd:(r + 1) * hd], kc)
            s_ref[r, :, n_ctx:] = _mm_nt(q[:, r * hd:(r + 1) * hd], kl)

    def attend(i, s_ref):
        r0 = pl.multiple_of(i * blk, blk)
        ks = key_start(i)
        vl = kv_scr[pl.ds(ks, span), hd:2 * hd]
        mask = jnp.abs((r0 + _iota((blk, span), 0)) - (ks + _iota((blk, span), 1))) <= SW_WINDOW
        for r in range(rep):
            sink = sink_ref[g * rep + r]
            sc = s_ref[r, :, 0:n_ctx]
            sl = jnp.where(mask, s_ref[r, :, n_ctx:], NEG_INF)
            m = jnp.maximum(jnp.maximum(jnp.max(sc, axis=-1, keepdims=True), jnp.max(sl, axis=-1, keepdims=True)), sink)
            pc = jnp.exp(sc - m)
            pl_ = jnp.exp(sl - m)
            den = jnp.exp(sink - m) + jnp.sum(pc, axis=-1, keepdims=True) + jnp.sum(pl_, axis=-1, keepdims=True)
            o_ref[0, pl.ds(r0, blk), r * hd:(r + 1) * hd] = (_mm(pc, vc) + _mm(pl_, vl)) / den

    scores(0, s_a)

    def pair(j, carry):
        scores(2 * j + 1, s_b)
        attend(2 * j, s_a)
        scores(jnp.minimum(2 * j + 2, nb - 1), s_a)
        attend(2 * j + 1, s_b)
        return carry

    lax.fori_loop(0, nb // 2, pair, 0)
    qc = (u_ref[0, n_lat:, 0:qw].astype(F32) * scale).astype(BF16)
    for r in range(rep):
        o_ref[0, n_lat:, r * hd:(r + 1) * hd] = _sink_attend(qc[:, r * hd:(r + 1) * hd], [(kc, vc, None)], sink_ref[g * rep + r])


def _swa_mixer(u, n_lat, n_ctx, sink):
    bsz, t, _ = u.shape
    hd, rep = SW_HD, SW_REP
    gw = rep * hd + 2 * hd
    blk = SW_WINDOW
    assert n_lat % (2 * blk) == 0 and n_lat >= 3 * blk and n_lat % GRID_W == 0
    cq, saq, sbq = _rope_tables(n_lat, rep * hd, rep)
    ck, sak, sbk = _rope_tables(n_lat, 2 * hd, 1)
    body = functools.partial(_swa_body, n_lat=n_lat, n_ctx=n_ctx, blk=blk)
    tab = lambda w: pl.BlockSpec((n_lat, w), lambda b, g: (0, 0))
    return pl.pallas_call(
        body, out_shape=S((bsz, t, SW_HQ * hd), F32), grid=(bsz, SW_HKV),
        in_specs=[pl.BlockSpec((1, t, gw), lambda b, g: (b, 0, g)),
                  tab(rep * hd), tab(rep * hd), tab(rep * hd), tab(2 * hd), tab(2 * hd), tab(2 * hd),
                  pl.BlockSpec(memory_space=pltpu.SMEM)],
        out_specs=pl.BlockSpec((1, t, rep * hd), lambda b, g: (b, 0, g)),
        scratch_shapes=[pltpu.VMEM((t, 2 * hd), BF16), pltpu.VMEM((rep, blk, n_ctx + 3 * blk), F32),
                        pltpu.VMEM((rep, blk, n_ctx + 3 * blk), F32)],
        compiler_params=_cp(("parallel", "parallel")), name="swa_attention",
    )(u, cq, saq, sbq, ck, sak, sbk, sink)


def _seq_conv(x, w_ref, n_lat):
    t = x.shape[0]
    row = _iota((t, 1), 0)
    first = (row == 0) | (row == n_lat)
    last = (row == n_lat - 1) | (row == t - 1)
    xp = jnp.where(first, 0.0, pltpu.roll(x, 1, 0))
    xn = jnp.where(last, 0.0, pltpu.roll(x, t - 1, 0))
    return w_ref[0:1, :] * xp + w_ref[1:2, :] * x + w_ref[2:3, :] * xn


def _chunk_scan(x, reverse):
    t = x.shape[0]
    pos = _iota((t, 1), 0) & (CHUNK - 1)
    s = 1
    while s < CHUNK:
        if reverse:
            x = x + jnp.where(pos < CHUNK - s, pltpu.roll(x, t - s, 0), 0.0)
        else:
            x = x + jnp.where(pos >= s, pltpu.roll(x, s, 0), 0.0)
        s *= 2
    return x


def _chunk_scan_lanes(x, reverse):
    t = x.shape[1]
    pos = _iota((1, t), 1) & (CHUNK - 1)
    s = 1
    while s < CHUNK:
        if reverse:
            x = x + jnp.where(pos < CHUNK - s, pltpu.roll(x, t - s, 1), 0.0)
        else:
            x = x + jnp.where(pos >= s, pltpu.roll(x, s, 1), 0.0)
        s *= 2
    return x


def _chunk_order(s, n_lat_chunks, n_ctx_chunks, direction):
    if direction == 0:
        return jnp.where(s < n_ctx_chunks, n_lat_chunks + s, s - n_ctx_chunks)
    return n_lat_chunks + n_ctx_chunks - 1 - s


def _intra_unroll(n_super, sup=SUPER):
    return next(u for u in range(INTRA_ROWS // sup, 0, -1) if n_super % u == 0)


def _gated_rms(o, nw, z):
    return o * lax.rsqrt(jnp.mean(o * o, axis=-1, keepdims=True) + EPS) * nw * _silu(z)


def _bmm(a, b):
    return jnp.einsum('bij,bjk->bik', a.astype(BF16), b.astype(BF16), preferred_element_type=F32)


def _bmm_nt(a, b):
    return jnp.einsum('bik,bjk->bij', a.astype(BF16), b.astype(BF16), preferred_element_type=F32)


def _unit_tri_inverse(a, ii, jj):
    eye = (ii == jj).astype(F32)
    a8 = jnp.where((ii >> 3) == (jj >> 3), a, 0.0)
    a8_2 = _bmm(a8, a8)
    a8_4 = _bmm(a8_2, a8_2)
    x = _bmm(_bmm(eye - a8, eye + a8_2), eye + a8_4)
    sh = 3
    while (1 << sh) < CHUNK:
        e = jnp.where(((ii >> (sh + 1)) == (jj >> (sh + 1))) & ((ii >> sh) != (jj >> sh)), a, 0.0)
        x = x - _bmm(_bmm(x, e), x)
        sh += 1
    return x


def _gdn_body(q_ref, k_ref, v_ref, z_ref, ba_ref, cwq_ref, cwk_ref, cwv_ref, par_ref, nw_ref, o_ref,
              kn_s, qn_s, qe_s, kk_s, egl_s, kb_s, kbe_s, vb_s, gc_s, gt_s, qp_s, k2_s, n_s, oacc,
              *, n_lat, n_ctx):
    t = n_lat + n_ctx
    h = pl.program_id(1)
    lane = _iota((1, LANES), 1)
    q = _silu(_seq_conv(q_ref[0].astype(F32), cwq_ref, n_lat))
    k = _silu(_seq_conv(k_ref[0].astype(F32), cwk_ref, n_lat))
    v = _silu(_seq_conv(v_ref[0].astype(F32), cwv_ref, n_lat))
    qn = q * lax.rsqrt(jnp.sum(q * q, axis=-1, keepdims=True) + EPS) * (GD_DK ** -0.5)
    kn = k * lax.rsqrt(jnp.sum(k * k, axis=-1, keepdims=True) + EPS)
    kn_s[...] = kn.astype(BF16)
    qn_s[...] = qn.astype(BF16)
    ba = ba_ref[0].astype(F32)
    par = par_ref[...]

    def pick(x, c):
        return jnp.sum(jnp.where(lane == c, x, 0.0), axis=-1, keepdims=True)

    gpack = jnp.zeros((t, LANES), F32)
    for d in range(2):
        a_log = pick(par[d:d + 1, :], h)
        dt_b = pick(par[2 + d:3 + d, :], h)
        g = -jnp.exp(a_log) * _softplus(pick(ba, 2 * GD_H + d * GD_H + h) + dt_b)
        gpack = jnp.where(lane == d, g, gpack)
    g_t = gpack.T[0:8, :]
    pre = _chunk_scan_lanes(g_t, False)
    suf = _chunk_scan_lanes(g_t, True)
    row8 = _iota((8, 1), 0)
    gcum_t = jnp.where(row8 == 1, suf, pre)
    gt_s[...] = gcum_t
    packed = jnp.where(row8 < 2, gcum_t, pltpu.roll(pre + suf - g_t, 2, 0))
    cols = jnp.concatenate([packed, jnp.zeros((LANES - 8, t), F32)], axis=0).T
    gc_s[...] = cols
    for d in range(2):
        beta = _sigmoid(pick(ba, d * GD_H + h))
        gc = jnp.broadcast_to(cols[:, d:d + 1], (t, LANES))
        glast = jnp.broadcast_to(cols[:, 2 + d:3 + d], (t, LANES))
        eg = jnp.exp(gc)
        kb = kn * beta
        qe_s[d] = (qn * eg).astype(BF16)
        kk_s[d] = (kn * jnp.exp(glast - gc)).astype(BF16)
        egl_s[d] = jnp.exp(glast)
        kb_s[d] = kb.astype(BF16)
        kbe_s[d] = (kb * eg).astype(BF16)
        vb_s[d] = (v * beta).astype(BF16)

    sup = GD_SUPER
    ii = _iota((sup, sup), 0)
    jj = _iota((sup, sup), 1)
    same = (ii >> 6) == (jj >> 6)
    per = sup // CHUNK
    unroll = _intra_unroll(t // sup, sup)

    def intra(it, carry):
        r0s = [pl.multiple_of((it * unroll + kq) * sup, sup) for kq in range(unroll)]
        kcs = [kn_s[pl.ds(r0, sup), :] for r0 in r0s]
        pairs = [(kq, d) for kq in range(unroll) for d in range(2)]
        decs = []
        for kq, d in pairs:
            rows = pl.ds(r0s[kq], sup)
            incl = same & ((jj >= ii) if d else (jj <= ii))
            dif = jnp.broadcast_to(gc_s[rows, d:d + 1], (sup, sup)) - gt_s[d:d + 1, rows]
            decs.append(jnp.where(incl, jnp.exp(jnp.where(incl, dif, 0.0)), 0.0))
        dec = jnp.stack(decs)
        kc2 = jnp.stack([kcs[kq] for kq, _ in pairs])
        kb = jnp.stack([kb_s[d, pl.ds(r0s[kq], sup), :] for kq, d in pairs])
        rhs = jnp.stack([jnp.concatenate([vb_s[d, pl.ds(r0s[kq], sup), :], kbe_s[d, pl.ds(r0s[kq], sup), :]], axis=1)
                         for kq, d in pairs])
        a = jnp.where(ii == jj, 0.0, _bmm_nt(kb, kc2) * dec)
        uw = _bmm(_unit_tri_inverse(a, ii, jj), rhs).astype(BF16)
        qk = _bmm_nt(jnp.stack([qn_s[pl.ds(r0, sup), :] for r0 in r0s]), jnp.stack(kcs))
        auw = _bmm(jnp.stack([qk[kq] for kq, _ in pairs]) * dec, uw)
        for i, (kq, d) in enumerate(pairs):
            rows = pl.ds(r0s[kq], sup)
            qp_s[d, rows, :] = (qe_s[d, rows, :].astype(F32) - auw[i, :, LANES:]).astype(BF16)
            if d == 1:
                oacc[rows, :] = auw[i - 1, :, 0:LANES] + auw[i, :, 0:LANES]
            for c4 in range(per):
                kuw = _mm_tn(kk_s[d, pl.ds(r0s[kq] + c4 * CHUNK, CHUNK), :], uw[i, c4 * CHUNK:(c4 + 1) * CHUNK])
                crow = pl.ds(pl.multiple_of(((it * unroll + kq) * per + c4) * GD_DK, GD_DK), GD_DK)
                n_s[d, crow, :] = kuw[:, 0:LANES]
                k2_s[d, crow, :] = kuw[:, LANES:].astype(BF16)
        return carry

    lax.fori_loop(0, t // sup // unroll, intra, 0)

    nl, nc = n_lat // CHUNK, n_ctx // CHUNK

    def step(s, states):
        new = []
        for d in range(2):
            st = states[d]
            sb = st.astype(BF16)
            c = _chunk_order(s, nl, nc, d)
            rows = pl.ds(pl.multiple_of(c * CHUNK, CHUNK), CHUNK)
            crow = pl.ds(pl.multiple_of(c * GD_DK, GD_DK), GD_DK)
            oacc[rows, :] += jnp.dot(qp_s[d, rows, :], sb, preferred_element_type=F32)
            new.append(st * egl_s[d, pl.ds(c * CHUNK, 1), :] + n_s[d, crow, :]
                       - jnp.dot(k2_s[d, crow, :], sb, preferred_element_type=F32))
        return tuple(new)

    zero = jnp.zeros((GD_DK, LANES), F32)
    lax.fori_loop(0, nl + nc, step, (zero, zero))
    o_ref[0] = _gated_rms(oacc[...], nw_ref[...], z_ref[0].astype(F32))


def _gdn_mixer(u, n_lat, n_ctx, conv_w, a_log, dt_bias, norm_w):
    bsz, t, _ = u.shape
    assert t % SUPER == 0 and n_lat % SUPER == 0
    par = jnp.zeros((8, LANES), F32).at[0:2, :GD_H].set(a_log).at[2:4, :GD_H].set(dt_bias)
    body = functools.partial(_gdn_body, n_lat=n_lat, n_ctx=n_ctx)
    sec = lambda s: pl.BlockSpec((1, t, LANES), lambda b, h: (b, 0, s * GD_H + h))
    cw = lambda s: pl.BlockSpec((3, LANES), lambda b, h: (0, s * GD_H + h))
    both = lambda dt: pltpu.VMEM((2, t, LANES), dt)
    state_rows = (t // CHUNK) * GD_DK
    return pl.pallas_call(
        body, out_shape=S((bsz, t, GD_H * LANES), F32), grid=(bsz, GD_H),
        in_specs=[sec(0), sec(1), sec(2), sec(3),
                  pl.BlockSpec((1, t, LANES), lambda b, h: (b, 0, 4 * GD_H)),
                  cw(0), cw(1), cw(2),
                  pl.BlockSpec((8, LANES), lambda b, h: (0, 0)),
                  pl.BlockSpec((1, LANES), lambda b, h: (0, 0))],
        out_specs=pl.BlockSpec((1, t, LANES), lambda b, h: (b, 0, h)),
        scratch_shapes=[pltpu.VMEM((t, LANES), BF16), pltpu.VMEM((t, LANES), BF16),
                        both(BF16), both(BF16), both(F32), both(BF16), both(BF16), both(BF16),
                        pltpu.VMEM((t, LANES), F32),
                        pltpu.VMEM((8, t), F32), both(BF16), pltpu.VMEM((2, state_rows, LANES), BF16),
                        pltpu.VMEM((2, state_rows, LANES), F32), pltpu.VMEM((t, LANES), F32)],
        compiler_params=_cp(("parallel", "parallel"), 56), name="gdn_mixer",
    )(u, u, u, u, u, conv_w, conv_w, conv_w, par, norm_w.reshape(1, LANES))


def _hgrn_body(q_ref, ff_ref, fb_ref, i_ref, g_ref, lbp_ref, nw_ref, o_ref,
               qe_s, ke_s, kk_s, egl_s, v_s, n_s, st_s, oacc, *, n_lat, n_ctx, layer):
    t = n_lat + n_ctx
    e = jnp.exp(lbp_ref[...] - jnp.max(lbp_ref[...], axis=0, keepdims=True))
    lb = jnp.sum(e[1:layer + 1, :], axis=0, keepdims=True) / jnp.sum(e, axis=0, keepdims=True)
    q = _silu(q_ref[0].astype(F32))
    v_s[...] = i_ref[0]
    for d, f_ref in enumerate((ff_ref, fb_ref)):
        f = f_ref[0].astype(F32)
        sig = _sigmoid(f)
        logf = jnp.log(lb + (1.0 - lb) * sig)
        kin = (1.0 - lb) * (1.0 - sig)
        gc = _chunk_scan(logf, bool(d))
        ends = gc.reshape(t // CHUNK, CHUNK, LANES)[:, 0:1, :] if d else gc.reshape(t // CHUNK, CHUNK, LANES)[:, CHUNK - 1:, :]
        glast = jnp.broadcast_to(ends, (t // CHUNK, CHUNK, LANES)).reshape(t, LANES)
        qe_s[d] = (q * jnp.exp(gc)).astype(BF16)
        ke_s[d] = (kin * jnp.exp(-gc)).astype(BF16)
        kk_s[d] = (kin * jnp.exp(glast - gc)).astype(BF16)
        egl_s[d] = jnp.exp(glast)

    ii = _iota((SUPER, SUPER), 0)
    jj = _iota((SUPER, SUPER), 1)
    same = (ii >> 6) == (jj >> 6)
    per = SUPER // CHUNK
    unroll = _intra_unroll(t // SUPER)

    def chunk_rows(sc, c4):
        rows = pl.ds(pl.multiple_of(sc * SUPER + c4 * CHUNK, CHUNK), CHUNK)
        crow = pl.ds(pl.multiple_of((sc * per + c4) * LANES, LANES), LANES)
        return rows, crow

    def intra(it, carry):
        pairs = [(kq, d) for kq in range(unroll) for d in range(2)]
        rows = [pl.ds(pl.multiple_of((it * unroll + kq) * SUPER, SUPER), SUPER) for kq in range(unroll)]
        incl = jnp.stack([same & ((jj >= ii) if d else (jj <= ii)) for _, d in pairs])
        at = jnp.where(incl, _bmm_nt(jnp.stack([qe_s[d, rows[kq], :] for kq, d in pairs]),
                                     jnp.stack([ke_s[d, rows[kq], :] for kq, d in pairs])), 0.0)
        part = _bmm(at, jnp.stack([v_s[rows[kq], :] for kq, _ in pairs]))
        for kq in range(unroll):
            oacc[rows[kq], :] = part[2 * kq] + part[2 * kq + 1]
            for d in range(2):
                for c4 in range(per):
                    r64, crow = chunk_rows(it * unroll + kq, c4)
                    n_s[d, crow, :] = _mm_tn(v_s[r64, :], kk_s[d, r64, :])
        return carry

    lax.fori_loop(0, t // SUPER // unroll, intra, 0)
    nl, nc = n_lat // CHUNK, n_ctx // CHUNK

    def scan(s, states):
        new = []
        for d in range(2):
            c = _chunk_order(s, nl, nc, d)
            crow = pl.ds(pl.multiple_of(c * LANES, LANES), LANES)
            st_s[d, crow, :] = states[d].astype(BF16)
            new.append(states[d] * egl_s[d, pl.ds(c * CHUNK, 1), :] + n_s[d, crow, :])
        return tuple(new)

    zero = jnp.zeros((LANES, HG_DK), F32)
    lax.fori_loop(0, nl + nc, scan, (zero, zero))

    def inter(it, carry):
        for kq in range(unroll):
            sc = it * unroll + kq
            for c4 in range(per):
                r64, crow = chunk_rows(sc, c4)
                oacc[r64, :] += (_mm_nt(qe_s[0, r64, :], st_s[0, crow, :]) + _mm_nt(qe_s[1, r64, :], st_s[1, crow, :]))
        return carry

    lax.fori_loop(0, t // SUPER // unroll, inter, 0)
    o_ref[0] = _gated_rms(oacc[...], nw_ref[...], g_ref[0].astype(F32))


def _hgrn_mixer(u, n_lat, n_ctx, hg_lb, norm_w, layer):
    bsz, t, n5 = u.shape
    d = n5 // 5
    nh = d // HG_DK
    depth = hg_lb.shape[0]
    assert t % SUPER == 0 and n_lat % SUPER == 0
    body = functools.partial(_hgrn_body, n_lat=n_lat, n_ctx=n_ctx, layer=layer)
    sec = lambda s: pl.BlockSpec((1, t, LANES), lambda b, h: (b, 0, s * nh + h))
    both = lambda dt: pltpu.VMEM((2, t, LANES), dt)
    state_rows = (t // CHUNK) * LANES
    return pl.pallas_call(
        body, out_shape=S((bsz, t, d), F32), grid=(bsz, nh),
        in_specs=[sec(0), sec(1), sec(2), sec(3), sec(4),
                  pl.BlockSpec((depth, LANES), lambda b, h: (0, h)),
                  pl.BlockSpec((1, LANES), lambda b, h: (0, 0))],
        out_specs=pl.BlockSpec((1, t, LANES), lambda b, h: (b, 0, h)),
        scratch_shapes=[both(BF16), both(BF16), both(BF16), both(F32), pltpu.VMEM((t, LANES), BF16),
                        pltpu.VMEM((2, state_rows, LANES), F32), pltpu.VMEM((2, state_rows, LANES), BF16),
                        pltpu.VMEM((t, LANES), F32)],
        compiler_params=_cp(("parallel", "parallel")), name="hgrn2_mixer",
    )(u, u, u, u, u, hg_lb, norm_w.reshape(1, LANES))


def _excl_count_lanes(x):
    n = x.shape[1]
    blk = min(n, 256)
    tri = jnp.where(_iota((blk, blk), 0) < _iota((blk, blk), 1), 1.0, 0.0).astype(BF16)
    run = jnp.zeros((x.shape[0], 1), F32)
    outs = []
    for r in range(n // blk):
        xb = x[:, r * blk:(r + 1) * blk]
        outs.append(jnp.dot(xb.astype(BF16), tri, preferred_element_type=F32) + run)
        run = run + jnp.sum(xb, axis=1, keepdims=True)
    return jnp.concatenate(outs, axis=1) if len(outs) > 1 else outs[0]


def _topcap_slots(groups):
    def count_ge(v, thr):
        return jnp.sum(jnp.where(v >= thr, 1.0, 0.0), axis=1, keepdims=True)

    def bisect(_, carry):
        new = []
        for (v, cap), (lo, hi) in zip(groups, carry):
            mid = jnp.sqrt(jnp.maximum(lo, 1e-37)) * jnp.sqrt(hi)
            ok = count_ge(v, mid) >= float(cap)
            new.append((jnp.where(ok, mid, lo), jnp.where(ok, hi, mid)))
        return tuple(new)

    init = tuple((jnp.zeros((v.shape[0], 1), F32), jnp.full((v.shape[0], 1), 2.0, F32)) for v, _ in groups)
    brackets = lax.fori_loop(0, 34, bisect, init)
    codes = []
    for (v, cap), (lo, hi) in zip(groups, brackets):
        thr, found, upper = lo, jnp.zeros_like(lo), hi
        for _ in range(4):
            m = jnp.max(jnp.where(v < upper, v, -1.0), axis=1, keepdims=True)
            ok = jnp.where(count_ge(v, m) >= float(cap), 1.0, 0.0) * (1.0 - found)
            thr = jnp.where(ok > 0, m, thr)
            found = jnp.maximum(found, ok)
            upper = jnp.where(found > 0, upper, m)
        gt = jnp.where(v > thr, 1.0, 0.0)
        eq = jnp.where(v == thr, 1.0, 0.0)
        need = float(cap) - jnp.sum(gt, axis=1, keepdims=True)
        sel = jnp.maximum(gt, eq * jnp.where(_excl_count_lanes(eq) < need, 1.0, 0.0))
        codes.append(jnp.where(sel > 0, _excl_count_lanes(sel), -1.0))
    return codes


def _route_body(lg_ref, aff_ref, code_ref, codet_ref, *, n_lat, cap_l, cap_c):
    t = lg_ref.shape[1]
    lane_ok = _iota((1, LANES), 1) < N_EXPERTS
    lg = jnp.where(lane_ok, lg_ref[0], NEG_INF)
    e = jnp.exp(lg - jnp.max(lg, axis=-1, keepdims=True))
    aff = e / jnp.sum(e, axis=-1, keepdims=True)
    aff_ref[0] = aff
    aff_t = aff.T[0:N_EXPERTS, :]
    code_l, code_c = _topcap_slots([(aff_t[:, 0:n_lat], cap_l), (aff_t[:, n_lat:], cap_c)])
    code_t = jnp.concatenate([code_l, code_c], axis=1)
    codet_ref[0] = code_t
    code_ref[0] = jnp.concatenate([code_t, jnp.full((LANES - N_EXPERTS, t), -1.0, F32)], axis=0).T


def _route(logits, n_lat, cap_l, cap_c):
    bsz, t, _ = logits.shape
    body = functools.partial(_route_body, n_lat=n_lat, cap_l=cap_l, cap_c=cap_c)
    blk = pl.BlockSpec((1, t, LANES), lambda b: (b, 0, 0))
    return pl.pallas_call(
        body, out_shape=(S((bsz, t, LANES), F32), S((bsz, t, LANES), F32), S((bsz, N_EXPERTS, t), F32)), grid=(bsz,),
        in_specs=[blk], out_specs=(blk, blk, pl.BlockSpec((1, N_EXPERTS, t), lambda b: (b, 0, 0))),
        compiler_params=_cp(("parallel",)), name="moe_route",
    )(logits)


def _gather_body(codet_ref, h_ref, xl_ref, xc_ref, *, n_lat, n_ctx, cap_l, cap_c, grp):
    hl = h_ref[0, 0:n_lat, :]
    hc = h_ref[0, n_lat:, :]
    il = _iota((cap_l, n_lat), 0).astype(F32)
    ic = _iota((cap_c, n_ctx), 0).astype(F32)
    for e0 in range(0, N_EXPERTS, grp):
        onehot = jnp.concatenate([jnp.where(il == codet_ref[0, e:e + 1, 0:n_lat], 1.0, 0.0).astype(BF16)
                                  for e in range(e0, e0 + grp)], axis=0)
        xe = jnp.dot(onehot, hl, preferred_element_type=F32).astype(BF16)
        for r in range(grp):
            xl_ref[0, e0 + r] = xe[r * cap_l:(r + 1) * cap_l]
    onehot = jnp.concatenate([jnp.where(ic == codet_ref[0, e:e + 1, n_lat:], 1.0, 0.0).astype(BF16)
                              for e in range(N_EXPERTS)], axis=0)
    xc_ref[0] = jnp.dot(onehot, hc, preferred_element_type=F32).astype(BF16)


def _gather(codet, h, n_lat, cap_l, cap_c):
    bsz, t, d = h.shape
    body = functools.partial(_gather_body, n_lat=n_lat, n_ctx=t - n_lat, cap_l=cap_l, cap_c=cap_c, grp=4)
    return pl.pallas_call(
        body, out_shape=(S((bsz, N_EXPERTS, cap_l, d), BF16), S((bsz, N_EXPERTS * cap_c, d), BF16)), grid=(bsz,),
        in_specs=[pl.BlockSpec((1, N_EXPERTS, t), lambda b: (b, 0, 0)), pl.BlockSpec((1, t, d), lambda b: (b, 0, 0))],
        out_specs=(pl.BlockSpec((1, N_EXPERTS, cap_l, d), lambda b: (b, 0, 0, 0)),
                   pl.BlockSpec((1, N_EXPERTS * cap_c, d), lambda b: (b, 0, 0))),
        compiler_params=_cp(("parallel",)), name="moe_gather",
    )(codet, h)


def _ffn_body(xl_ref, xc_ref, wg_ref, wu_ref, wd_ref, yl_ref, yc_ref, x_scr, acc, *, nb, cap_l, cap_c):
    f = pl.program_id(1)
    d = x_scr.shape[1]

    @pl.when(f == 0)
    def _():
        x_scr[0:nb * cap_l, :] = xl_ref[:, 0].reshape(nb * cap_l, d)
        x_scr[nb * cap_l:, :] = xc_ref[:, 0].reshape(nb * cap_c, d)
        acc[...] = jnp.zeros_like(acc)

    x = x_scr[...]
    a = jnp.dot(x, wg_ref[0, 0].astype(BF16), preferred_element_type=F32)
    u = jnp.dot(x, wu_ref[0, 0].astype(BF16), preferred_element_type=F32)
    acc[...] += _mm(_silu(a) * u, wd_ref[0, 0])

    @pl.when(f == pl.num_programs(1) - 1)
    def _():
        y = acc[...].astype(BF16)
        yl_ref[:, 0] = y[0:nb * cap_l].reshape(nb, cap_l, d)
        yc_ref[:, 0] = y[nb * cap_l:].reshape(nb, cap_c, d)


def _expert_ffn(xl, xc, w_gate, w_up, w_down, layer):
    bsz, ne, cap_l, d = xl.shape
    cap_c = xc.shape[2]
    ffd = w_gate.shape[-1]
    tf = 256
    rows = bsz * (cap_l + cap_c)
    body = functools.partial(_ffn_body, nb=bsz, cap_l=cap_l, cap_c=cap_c)
    return pl.pallas_call(
        body, out_shape=(S(xl.shape, BF16), S(xc.shape, BF16)), grid=(ne, ffd // tf),
        in_specs=[pl.BlockSpec((bsz, 1, cap_l, d), lambda e, f: (0, e, 0, 0)),
                  pl.BlockSpec((bsz, 1, cap_c, d), lambda e, f: (0, e, 0, 0)),
                  pl.BlockSpec((1, 1, d, tf), lambda e, f: (layer, e, 0, f)),
                  pl.BlockSpec((1, 1, d, tf), lambda e, f: (layer, e, 0, f)),
                  pl.BlockSpec((1, 1, tf, d), lambda e, f: (layer, e, f, 0))],
        out_specs=(pl.BlockSpec((bsz, 1, cap_l, d), lambda e, f: (0, e, 0, 0)),
                   pl.BlockSpec((bsz, 1, cap_c, d), lambda e, f: (0, e, 0, 0))),
        scratch_shapes=[pltpu.VMEM((rows, d), BF16), pltpu.VMEM((rows, d), F32)],
        compiler_params=_cp(("parallel", "arbitrary")), name="moe_expert_ffn",
    )(xl, xc, w_gate, w_up, w_down)


def _combine_body(code_ref, aff_ref, yl_ref, yc_ref, res_ref, gate_ref, fw_ref, o_ref,
                  *, n_lat, n_b, cap_l, cap_c, tm, final):
    b, i = pl.program_id(0), pl.program_id(1)
    code = code_ref[0]
    aff = aff_ref[0]

    def scatter(cap, y):
        slot = _iota((tm, cap), 1).astype(F32)
        q = jnp.concatenate([jnp.where(code[:, e:e + 1] == slot, aff[:, e:e + 1], 0.0) for e in range(N_EXPERTS)], axis=1)
        return jnp.dot(q.astype(BF16), y, preferred_element_type=F32)

    @pl.when(i * tm < n_lat)
    def _():
        x = res_ref[0] + gate_ref[pl.ds(b, 1), :] * scatter(cap_l, yl_ref[0])
        if final:
            x = x * lax.rsqrt(jnp.mean(x * x, axis=-1, keepdims=True) + EPS) * fw_ref[...]
        o_ref[0] = x

    @pl.when(i * tm >= n_lat)
    def _():
        o_ref[0] = res_ref[0] + gate_ref[n_b:n_b + 1, :] * scatter(cap_c, yc_ref[0])


def _combine(code, aff, yl, yc, res, mod, n_lat, final_w, final):
    bsz, t, d = res.shape
    cap_l, cap_c = yl.shape[2], yc.shape[2]
    tm = t - n_lat
    assert n_lat % tm == 0
    rows = n_lat if final else t
    body = functools.partial(_combine_body, n_lat=n_lat, n_b=bsz, cap_l=cap_l, cap_c=cap_c, tm=tm, final=final)
    tok = lambda w: pl.BlockSpec((1, tm, w), lambda b, i: (b, i, 0))
    return pl.pallas_call(
        body, out_shape=S((bsz, rows, d), F32), grid=(bsz, rows // tm),
        in_specs=[tok(LANES), tok(LANES),
                  pl.BlockSpec((1, N_EXPERTS * cap_l, d), lambda b, i: (b, 0, 0)),
                  pl.BlockSpec((1, N_EXPERTS * cap_c, d), lambda b, i: (b, 0, 0)),
                  tok(d), pl.BlockSpec((MOD_ROWS, d), lambda b, i: (0, 5)), pl.BlockSpec((1, d), lambda b, i: (0, 0))],
        out_specs=tok(d),
        compiler_params=_cp(("parallel", "parallel")), name="moe_combine",
    )(code, aff, yl.reshape(bsz, N_EXPERTS * cap_l, d), yc.reshape(bsz, N_EXPERTS * cap_c, d), res, mod,
      final_w.reshape(1, d))


def _moe_layer(xs, h, logits, mod, w_gate, w_up, w_down, layer, n_lat, final_w, final):
    bsz, t, d = xs.shape
    n_ctx = t - n_lat
    cap_l = EC_CAPACITY * n_lat // N_EXPERTS
    cap_c = EC_CAPACITY * n_ctx // N_EXPERTS
    aff, code, codet = _route(logits, n_lat, cap_l, cap_c)
    xl, xc = _gather(codet, h, n_lat, cap_l, cap_c)
    yl, yc = _expert_ffn(xl, xc.reshape(bsz, N_EXPERTS, cap_c, d), w_gate, w_up, w_down, layer)
    return _combine(code, aff, yl, yc, xs, mod, n_lat, final_w, final)


def _swa_group_columns():
    hd, rep = SW_HD, SW_REP
    cols = []
    for g in range(SW_HKV):
        cols += list(range(g * rep * hd, (g + 1) * rep * hd))
        cols += list(range(SW_HQ * hd + g * hd, SW_HQ * hd + (g + 1) * hd))
        cols += list(range((SW_HQ + SW_HKV) * hd + g * hd, (SW_HQ + SW_HKV) * hd + (g + 1) * hd))
    return np.asarray(cols, np.int32)


def kernel(x, c, ctx, c_ctx, ada_w, ada_b, norm1_w, norm2_w, final_norm_w, hy_w_in, hy_b_in, hy_short_w, hy_short_b, hy_ffn_w1, hy_ffn_b1, hy_ffn_w2, hy_ffn_b2, hy_ffn_w3, hy_sin_freq, hy_filter_bias, hy_w_out, hy_b_out, sw_w_in, sw_sink, sw_w_out, gd_w_in, gd_conv_w, gd_a_log, gd_dt_bias, gd_norm_w, gd_w_out, hg_w_in, hg_lb, hg_norm_w, hg_w_out, moe_router, moe_w_gate, moe_w_up, moe_w_down):
    bsz, n_lat, d = x.shape
    n_ctx = ctx.shape[1]
    depth = ada_w.shape[0]
    assert bsz < MOD_ROWS and n_lat % n_ctx == 0
    xs = jnp.concatenate([x, ctx], axis=1)
    c16 = jnp.zeros((MOD_ROWS, d), F32).at[:bsz].set(c).at[bsz].set(c_ctx)
    mod = _modulation(c16, ada_w, ada_b)
    zero_bias = jnp.zeros((d,), F32)
    gd_pad = (-gd_w_in.shape[1]) % LANES
    gd_w = jnp.pad(gd_w_in, ((0, 0), (0, gd_pad))).astype(BF16)
    sw_w = sw_w_in[:, _swa_group_columns()].astype(BF16)
    for layer in range(depth):
        m = mod[layer]
        nw = norm1_w[layer]
        kind = layer % 4
        if kind == 0:
            u = _in_proj(xs, nw, m, 0, 1, hy_w_in.astype(BF16), hy_b_in, n_lat, "hy_in_proj")
            y = _hyena_mixer(u, n_lat, n_ctx, hy_short_w, hy_short_b, hy_ffn_w1, hy_ffn_b1, hy_ffn_w2, hy_ffn_b2,
                             hy_ffn_w3, hy_sin_freq, hy_filter_bias)
            w_out, b_out, name = hy_w_out, hy_b_out, "hy_out_proj"
        elif kind == 1:
            u = _in_proj(xs, nw, m, 0, 1, sw_w, jnp.zeros((sw_w.shape[1],), F32), n_lat, "sw_in_proj")
            y = _swa_mixer(u, n_lat, n_ctx, sw_sink)
            w_out, b_out, name = sw_w_out, zero_bias, "sw_out_proj"
        elif kind == 2:
            u = _in_proj(xs, nw, m, 0, 1, gd_w, jnp.zeros((gd_w.shape[1],), F32), n_lat, "gd_in_proj")
            y = _gdn_mixer(u, n_lat, n_ctx, gd_conv_w, gd_a_log, gd_dt_bias, gd_norm_w)
            w_out, b_out, name = gd_w_out, zero_bias, "gd_out_proj"
        else:
            u = _in_proj(xs, nw, m, 0, 1, hg_w_in.astype(BF16), jnp.zeros((hg_w_in.shape[1],), F32), n_lat, "hg_in_proj")
            y = _hgrn_mixer(u, n_lat, n_ctx, hg_lb, hg_norm_w, layer)
            w_out, b_out, name = hg_w_out, zero_bias, "hg_out_proj"
        router_w = jnp.pad(moe_router[layer], ((0, 0), (0, LANES - N_EXPERTS)))
        router_hi = router_w.astype(BF16)
        router_p = jnp.stack([router_hi, (router_w - router_hi.astype(F32)).astype(BF16)])
        xs, h, logits = _out_proj(y, w_out.astype(BF16), b_out, xs, m, n_lat, norm2_w[layer], router_p, name)
        xs = _moe_layer(xs, h, logits, m, moe_w_gate, moe_w_up, moe_w_down, layer, n_lat, final_norm_w,
                        layer == depth - 1)
    return xs
```

```python
import functools
import math

import jax
import jax.numpy as jnp
import numpy as np
from jax import lax
from jax.experimental import pallas as pl
from jax.experimental.pallas import tpu as pltpu

F32 = jnp.float32
BF16 = jnp.bfloat16
HIGHEST = lax.Precision.HIGHEST
EPS = 1e-6
NEG_INF = -1e30
LANES = 128
BF16_ROWS = 16
MOD_ROWS = 16

GRID_W = 64
HY_ORDER = 2
HY_EMB = 33
HY_FAST_DECAY = 0.3
HY_SLOW_DECAY = 1.5
HY_TARGET = 1e-2
HY_SHIFT = 0.05
SW_HQ, SW_HKV, SW_HD, SW_WINDOW = 16, 4, 64, 128
SW_REP = SW_HQ // SW_HKV
ROPE_BASE = 10000.0
GD_H, GD_DK = 8, 128
HG_DK = 128
CHUNK = 64
SUPER = 256
GD_SUPER = 128
INTRA_ROWS = 768
IN_PROJ_TN = 512
N_EXPERTS = 16
EC_CAPACITY = 2

S = jax.ShapeDtypeStruct


def _cp(sem, vmem_mb=48):
    return pltpu.CompilerParams(dimension_semantics=sem, vmem_limit_bytes=vmem_mb * 2**20)


def _iota(shape, dim):
    return lax.broadcasted_iota(jnp.int32, shape, dim)


def _sigmoid(x):
    return 0.5 * jnp.tanh(0.5 * x) + 0.5


def _silu(x):
    return x * _sigmoid(x)


def _softplus(x):
    return jnp.maximum(x, 0.0) + jnp.log(1.0 + jnp.exp(-jnp.abs(x)))


def _mm(a, b):
    return jnp.dot(a.astype(BF16), b.astype(BF16), preferred_element_type=F32)


def _mm_nt(a, b):
    return lax.dot_general(a.astype(BF16), b.astype(BF16), (((1,), (1,)), ((), ())), preferred_element_type=F32)


def _mm_tn(a, b):
    return lax.dot_general(a.astype(BF16), b.astype(BF16), (((0,), (0,)), ((), ())), preferred_element_type=F32)


def _mm_f32(a, b):
    return jnp.dot(a, b, precision=HIGHEST, preferred_element_type=F32)


def _mod_body(c_ref, w_ref, b_ref, o_ref):
    o_ref[0] = _mm_f32(_silu(c_ref[...]), w_ref[0]) + b_ref[0]


def _modulation(c16, ada_w, ada_b):
    depth, d, n = ada_w.shape
    tn = 1024
    return pl.pallas_call(
        _mod_body, out_shape=S((depth, MOD_ROWS, n), F32), grid=(depth, n // tn),
        in_specs=[pl.BlockSpec((MOD_ROWS, d), lambda l, j: (0, 0)),
                  pl.BlockSpec((1, d, tn), lambda l, j: (l, 0, j)),
                  pl.BlockSpec((1, 1, tn), lambda l, j: (l, 0, j))],
        out_specs=pl.BlockSpec((1, MOD_ROWS, tn), lambda l, j: (l, 0, j)),
        compiler_params=_cp(("parallel", "parallel")), name="adaln_mod",
    )(c16, ada_w, ada_b.reshape(depth, 1, n))


def _norm_mod(x, nw, shift_ref, scale_ref, b, row0, n_lat, n_b):
    tm = x.shape[0]
    y = x * lax.rsqrt(jnp.mean(x * x, axis=-1, keepdims=True) + EPS) * nw
    is_ctx = (row0 + _iota((tm, 1), 0)) >= n_lat
    shift = jnp.where(is_ctx, shift_ref[n_b:n_b + 1, :], shift_ref[pl.ds(b, 1), :])
    scale = jnp.where(is_ctx, scale_ref[n_b:n_b + 1, :], scale_ref[pl.ds(b, 1), :])
    return y * (1.0 + scale) + shift


def _row_gate(gate_ref, b, row0, tm, n_lat, n_b):
    is_ctx = (row0 + _iota((tm, 1), 0)) >= n_lat
    return jnp.where(is_ctx, gate_ref[n_b:n_b + 1, :], gate_ref[pl.ds(b, 1), :])


def _in_proj_body(x_ref, nw_ref, sh_ref, sc_ref, w_ref, bias_ref, o_ref, h_scr, *, n_lat, n_b, tm):
    b, i = pl.program_id(0), pl.program_id(1)

    @pl.when(pl.program_id(2) == 0)
    def _():
        h_scr[...] = _norm_mod(x_ref[0], nw_ref[...], sh_ref, sc_ref, b, i * tm, n_lat, n_b).astype(BF16)

    o_ref[0] = (jnp.dot(h_scr[...], w_ref[...], preferred_element_type=F32) + bias_ref[...]).astype(o_ref.dtype)


def _in_proj(xs, norm_w, mod, shift_idx, scale_idx, w, bias, n_lat, name):
    bsz, t, d = xs.shape
    n = w.shape[1]
    tm = t
    tn = IN_PROJ_TN
    assert n % tn == 0
    body = functools.partial(_in_proj_body, n_lat=n_lat, n_b=bsz, tm=tm)
    return pl.pallas_call(
        body, out_shape=S((bsz, t, n), BF16), grid=(bsz, t // tm, n // tn),
        in_specs=[pl.BlockSpec((1, tm, d), lambda b, i, j: (b, i, 0)),
                  pl.BlockSpec((1, d), lambda b, i, j: (0, 0)),
                  pl.BlockSpec((MOD_ROWS, d), lambda b, i, j: (0, shift_idx)),
                  pl.BlockSpec((MOD_ROWS, d), lambda b, i, j: (0, scale_idx)),
                  pl.BlockSpec((d, tn), lambda b, i, j: (0, j)),
                  pl.BlockSpec((1, tn), lambda b, i, j: (0, j))],
        out_specs=pl.BlockSpec((1, tm, tn), lambda b, i, j: (b, i, j)),
        scratch_shapes=[pltpu.VMEM((tm, d), BF16)],
        compiler_params=_cp(("parallel", "parallel", "arbitrary")), name=name,
    )(xs, norm_w.reshape(1, d), mod, mod, w, bias.reshape(1, n))


def _out_proj_body(y_ref, w_ref, bias_ref, res_ref, gate_ref, nw_ref, sh_ref, sc_ref, rw_ref, o_ref, h_ref, lg_ref,
                   *, n_lat, n_b, tm):
    b, i = pl.program_id(0), pl.program_id(1)
    y = jnp.dot(y_ref[0].astype(BF16), w_ref[...], preferred_element_type=F32) + bias_ref[...]
    x = res_ref[0] + _row_gate(gate_ref, b, i * tm, tm, n_lat, n_b) * y
    o_ref[0] = x
    h = _norm_mod(x, nw_ref[...], sh_ref, sc_ref, b, i * tm, n_lat, n_b)
    hi = h.astype(BF16)
    lo = (h - hi.astype(F32)).astype(BF16)
    h_ref[0] = hi
    r = jnp.dot(jnp.concatenate([hi, lo], axis=0), rw_ref[...], preferred_element_type=F32)
    lg_ref[0] = (r[0:tm, 0:LANES] + r[0:tm, LANES:]) + (r[tm:, 0:LANES] + r[tm:, LANES:])


def _out_proj(y, w, bias, res, mod, n_lat, norm2_w, router_p, name):
    bsz, t, dy = y.shape
    d = w.shape[1]
    tm = t // 3 if t % 3 == 0 and (t // 3) % BF16_ROWS == 0 else t
    body = functools.partial(_out_proj_body, n_lat=n_lat, n_b=bsz, tm=tm)
    tok = lambda width: pl.BlockSpec((1, tm, width), lambda b, i: (b, i, 0))
    modc = lambda idx: pl.BlockSpec((MOD_ROWS, d), lambda b, i: (0, idx))
    return pl.pallas_call(
        body, out_shape=(S((bsz, t, d), F32), S((bsz, t, d), BF16), S((bsz, t, LANES), F32)), grid=(bsz, t // tm),
        in_specs=[tok(dy), pl.BlockSpec((dy, d), lambda b, i: (0, 0)), pl.BlockSpec((1, d), lambda b, i: (0, 0)),
                  tok(d), modc(2), pl.BlockSpec((1, d), lambda b, i: (0, 0)), modc(3), modc(4),
                  pl.BlockSpec((d, 2 * LANES), lambda b, i: (0, 0))],
        out_specs=(tok(d), tok(d), tok(LANES)),
        compiler_params=_cp(("parallel", "parallel")), name=name,
    )(y, w, bias.reshape(1, d), res, mod, norm2_w.reshape(1, d), mod, mod, router_p)


def _dft_tables(n):
    half = n // 2
    k = np.arange(half, dtype=np.int64)[:, None]
    m = np.arange(half, dtype=np.int64)[None, :]
    ang_e = ((k * 2 * m) % (2 * n)).astype(np.float64) * (math.pi / n)
    ang_o = ((k * (2 * m + 1)) % (2 * n)).astype(np.float64) * (math.pi / n)
    ce, se, co, so = np.cos(ang_e), np.sin(ang_e), np.cos(ang_o), np.sin(ang_o)
    return jnp.asarray(np.stack([ce, se, co, so, co.T, so.T]), dtype=BF16)


def _hy_positional(n):
    t = np.linspace(0.0, 1.0, n)[:, None]
    bands = (HY_EMB - 1) // 2
    w = (2.0 * math.pi * np.arange(n) / n)[:, None]
    f = np.linspace(1e-4, bands - 1, bands)[None, :]
    z = np.concatenate([t, np.cos(f * w), -np.sin(f * w)], axis=-1)
    zp = np.zeros((n, LANES), np.float32)
    zp[:, :HY_EMB] = z
    order = np.concatenate([np.arange(0, n, 2), np.arange(1, n, 2)])
    return jnp.asarray(zp[order]), jnp.asarray(t.astype(np.float32)[order])


def _hy_mlp_body(z_ref, w1_ref, b1_ref, w2_ref, b2_ref, fr_ref, h_ref):
    h = jnp.sin(fr_ref[0:1, :] * (_mm_f32(z_ref[...], w1_ref[...]) + b1_ref[...]))
    h_ref[...] = jnp.sin(fr_ref[1:2, :] * (_mm_f32(h, w2_ref[...]) + b2_ref[...]))


def _alt_sign(rows):
    return (1 - 2 * (_iota((rows, 1), 0) & 1)).astype(F32)


def _hy_filter_body(h_ref, w3f_ref, w3b_ref, t_ref, dl_ref, tab_ref, pk_ref, pm_ref, *, n):
    half = n // 2
    win = jnp.exp(-t_ref[...] * dl_ref[...]) + HY_SHIFT
    hf = _mm_f32(h_ref[...], w3f_ref[...]) * win
    hb = _mm_f32(h_ref[...], w3b_ref[...]) * win
    hb = jnp.where(_iota((n, 1), 0) == 0, 0.0, hb)
    hs, hd = hf + hb, hf - hb
    alt = _alt_sign(half)
    pm_ref[0] = jnp.concatenate([jnp.sum(alt * hs[0:half], axis=0, keepdims=True),
                                 -jnp.sum(alt * hd[half:], axis=0, keepdims=True),
                                 jnp.zeros((6, hs.shape[1]), F32)], axis=0) * (1.0 / n)
    wgt = jnp.where(_iota((half, 1), 0) == 0, 0.5 / n, 1.0 / n)
    ce, co = _mm(tab_ref[0], hs[0:half]), _mm(tab_ref[2], hs[half:])
    se, so = _mm(tab_ref[1], hd[0:half]), _mm(tab_ref[3], hd[half:])
    pk_ref[0, 0] = (ce + co) * wgt
    pk_ref[0, 1] = (ce - co) * wgt
    pk_ref[0, 2] = -(se + so) * wgt
    pk_ref[0, 3] = (se - so) * wgt


def _hy_filter_spectra(n, w1p, b1, w2, b2, freq, w3, tabs, d, td):
    zpad, tcol = _hy_positional(n)
    max_decay = math.log(HY_TARGET) / HY_FAST_DECAY
    min_decay = math.log(HY_TARGET) / HY_SLOW_DECAY
    deltas = jnp.asarray(np.abs(np.linspace(min_decay, max_decay, d)), dtype=F32)[None, :]
    nd = d // td
    ff = w1p.shape[1]
    half = n // 2
    h = pl.pallas_call(_hy_mlp_body, out_shape=S((n, ff), F32), name=f"hy_filter_mlp_{n}")(
        zpad, w1p, b1.reshape(1, ff), w2, b2.reshape(1, ff), freq)
    body = functools.partial(_hy_filter_body, n=n)
    const = lambda o, j: (0, 0)
    return pl.pallas_call(
        body, out_shape=(S((HY_ORDER, 4, half, d), F32), S((HY_ORDER, 8, d), F32)), grid=(HY_ORDER, nd),
        in_specs=[pl.BlockSpec((n, ff), const),
                  pl.BlockSpec((ff, td), lambda o, j: (0, (2 * o) * nd + j)),
                  pl.BlockSpec((ff, td), lambda o, j: (0, (2 * o + 1) * nd + j)),
                  pl.BlockSpec((n, 1), const), pl.BlockSpec((1, td), lambda o, j: (0, j)),
                  pl.BlockSpec((6, half, half), lambda o, j: (0, 0, 0), pipeline_mode=pl.Buffered(1))],
        out_specs=(pl.BlockSpec((1, 4, half, td), lambda o, j: (o, 0, 0, j)),
                   pl.BlockSpec((1, 8, td), lambda o, j: (o, 0, j))),
        compiler_params=_cp(("parallel", "parallel")), name=f"hy_filter_{n}",
    )(h, w3, w3, tcol, deltas, tabs)


def _short_conv(x, w_ref, b_ref):
    n = x.shape[0]
    row = _iota((n, 1), 0)
    xp = jnp.where(row == 0, 0.0, pltpu.roll(x, 1, 0))
    xn = jnp.where(row == n - 1, 0.0, pltpu.roll(x, n - 1, 0))
    return w_ref[0:1, :] * xp + w_ref[1:2, :] * x + w_ref[2:3, :] * xn + b_ref[...]


def _hy_conv_body(a_ref, g_ref, wa_ref, ba_ref, wg_ref, bg_ref, tab_ref, pk_ref, pm_ref, fb_ref, o_ref,
                  a_s, y_s, *, n, conv_a):
    half = n // 2
    nt = a_s.shape[0]
    a = a_ref[0].astype(F32)
    if conv_a:
        a = _short_conv(a, wa_ref, ba_ref)
    for c in range(nt):
        a_s[c] = a[:, c * LANES:(c + 1) * LANES]
    ve = jnp.concatenate([a_s[c, pl.ds(0, half, stride=2), :] for c in range(nt)], axis=1)
    vo = jnp.concatenate([a_s[c, pl.ds(1, half, stride=2), :] for c in range(nt)], axis=1)
    ec, es = _mm(tab_ref[0], ve), _mm(tab_ref[1], ve)
    oc, os_ = _mm(tab_ref[2], vo), _mm(tab_ref[3], vo)
    alt = _alt_sign(half)
    xr_mid = jnp.sum(alt * ve, axis=0, keepdims=True)
    xi_mid = -jnp.sum(alt * vo, axis=0, keepdims=True)
    kr_lo, kr_hi, ki_lo, ki_hi = pk_ref[0, 0], pk_ref[0, 1], pk_ref[0, 2], pk_ref[0, 3]
    xr_lo, xr_hi, xi_lo, xi_hi = ec + oc, ec - oc, -(es + os_), es - os_
    yr_lo = xr_lo * kr_lo - xi_lo * ki_lo
    yi_lo = xr_lo * ki_lo + xi_lo * kr_lo
    yr_hi = xr_hi * kr_hi - xi_hi * ki_hi
    yi_hi = xr_hi * ki_hi + xi_hi * kr_hi
    kr_mid, ki_mid = pm_ref[0, 0:1, :], pm_ref[0, 1:2, :]
    yr_mid = xr_mid * kr_mid - xi_mid * ki_mid
    yi_mid = xr_mid * ki_mid + xi_mid * kr_mid
    y_even = _mm(tab_ref[0], yr_lo + yr_hi) - _mm(tab_ref[1], yi_lo - yi_hi) + alt * yr_mid
    y_odd = _mm(tab_ref[4], yr_lo - yr_hi) - _mm(tab_ref[5], yi_lo + yi_hi) - alt * yi_mid
    for c in range(nt):
        y_s[c, pl.ds(0, half, stride=2), :] = y_even[:, c * LANES:(c + 1) * LANES]
        y_s[c, pl.ds(1, half, stride=2), :] = y_odd[:, c * LANES:(c + 1) * LANES]
    y = jnp.concatenate([y_s[c] for c in range(nt)], axis=1)
    g = _short_conv(g_ref[0].astype(F32), wg_ref, bg_ref)
    o_ref[0] = g * (y + a * fb_ref[0])


def _hy_conv(a, a_col0, a_row, g, g_col0, g_row, n, conv_w, conv_b, tabs, pk, pm, fbias, order, conv_a, d, td, name):
    bsz = a.shape[0]
    nd = d // td
    half = n // 2
    body = functools.partial(_hy_conv_body, n=n, conv_a=conv_a)
    cw = conv_w
    cb = conv_b.reshape(1, -1)
    return pl.pallas_call(
        body, out_shape=S((bsz, n, d), F32), grid=(nd, bsz),
        in_specs=[pl.BlockSpec((1, n, td), lambda j, b: (b, a_row, a_col0 * nd + j)),
                  pl.BlockSpec((1, n, td), lambda j, b: (b, g_row, g_col0 * nd + j)),
                  pl.BlockSpec((3, td), lambda j, b: (0, a_col0 * nd + j if conv_a else j)),
                  pl.BlockSpec((1, td), lambda j, b: (0, a_col0 * nd + j if conv_a else j)),
                  pl.BlockSpec((3, td), lambda j, b: (0, g_col0 * nd + j)),
                  pl.BlockSpec((1, td), lambda j, b: (0, g_col0 * nd + j)),
                  pl.BlockSpec((6, half, half), lambda j, b: (0, 0, 0), pipeline_mode=pl.Buffered(1)),
                  pl.BlockSpec((1, 4, half, td), lambda j, b: (order, 0, 0, j), pipeline_mode=pl.Buffered(1)),
                  pl.BlockSpec((1, 8, td), lambda j, b: (order, 0, j)),
                  pl.BlockSpec((1, 1, td), lambda j, b: (order, 0, j))],
        out_specs=pl.BlockSpec((1, n, td), lambda j, b: (b, 0, j)),
        scratch_shapes=[pltpu.VMEM((td // LANES, n, LANES), F32), pltpu.VMEM((td // LANES, n, LANES), F32)],
        compiler_params=_cp(("parallel", "parallel"), 56), name=name,
    )(a, g, cw, cb, cw, cb, tabs, pk, pm, fbias.reshape(HY_ORDER, 1, d))


def _hyena_mixer(u, n_lat, n_ctx, short_w, short_b, w1, b1, w2, b2, w3, freq, fbias):
    d = u.shape[2] // 3
    ff = w1.shape[1]
    w1p = jnp.zeros((LANES, ff), F32).at[:w1.shape[0]].set(w1)
    outs = []
    for n, row in ((n_lat, 0), (n_ctx, n_lat // n_ctx)):
        td = 256 if n > 512 else 512
        tabs = _dft_tables(n)
        pk, pm = _hy_filter_spectra(n, w1p, b1, w2, b2, freq, w3, tabs, d, td)
        z1 = _hy_conv(u, 0, row, u, 1, row, n, short_w, short_b, tabs, pk, pm, fbias, 0, True, d, td, f"hy_conv1_{n}")
        z2 = _hy_conv(z1, 0, 0, u, 2, row, n, short_w, short_b, tabs, pk, pm, fbias, 1, False, d, td, f"hy_conv2_{n}")
        outs.append(z2)
    return jnp.concatenate(outs, axis=1)


def _rope_tables(n_lat, width, rot_heads):
    hd = SW_HD
    rows = n_lat // GRID_W
    row = np.repeat(np.arange(rows, dtype=np.float64), GRID_W)
    col = np.tile(np.arange(GRID_W, dtype=np.float64), rows)
    nf = hd // 4
    inv = ROPE_BASE ** (-np.arange(nf, dtype=np.float64) / nf)
    ang = np.concatenate([row[:, None] * inv, col[:, None] * inv], axis=-1)
    cos, sin = np.cos(ang), np.sin(ang)
    zero = np.zeros_like(sin)
    c = np.ones((n_lat, width), np.float32)
    sa = np.zeros((n_lat, width), np.float32)
    sb = np.zeros((n_lat, width), np.float32)
    for h in range(rot_heads):
        c[:, h * hd:(h + 1) * hd] = np.concatenate([cos, cos], axis=-1)
        sa[:, h * hd:(h + 1) * hd] = np.concatenate([-sin, zero], axis=-1)
        sb[:, h * hd:(h + 1) * hd] = np.concatenate([zero, sin], axis=-1)
    return jnp.asarray(c), jnp.asarray(sa), jnp.asarray(sb)


def _rope(x, c, sa, sb):
    w = x.shape[1]
    half = SW_HD // 2
    return x * c + pltpu.roll(x, w - half, 1) * sa + pltpu.roll(x, half, 1) * sb


def _sink_attend(q, kvs, sink):
    ss = []
    m = None
    for k, _, mask in kvs:
        s = _mm_nt(q, k)
        if mask is not None:
            s = jnp.where(mask, s, NEG_INF)
        ss.append(s)
        sm = jnp.max(s, axis=-1, keepdims=True)
        m = sm if m is None else jnp.maximum(m, sm)
    m = jnp.maximum(m, sink)
    den = jnp.exp(sink - m)
    o = None
    for s, (_, v, _) in zip(ss, kvs):
        p = jnp.exp(s - m)
        den = den + jnp.sum(p, axis=-1, keepdims=True)
        pv = _mm(p, v)
        o = pv if o is None else o + pv
    return o / den


def _swa_body(u_ref, cq_ref, saq_ref, sbq_ref, ck_ref, sak_ref, sbk_ref, sink_ref, o_ref, kv_scr, s_a, s_b,
              *, n_lat, n_ctx, blk):
    g = pl.program_id(1)
    hd, rep = SW_HD, SW_REP
    qw = rep * hd
    scale = hd ** -0.5
    span = 3 * blk
    nb = n_lat // blk
    kv_scr[0:n_lat, :] = _rope(u_ref[0, 0:n_lat, qw:qw + 2 * hd].astype(F32), ck_ref[...], sak_ref[...],
                               sbk_ref[...]).astype(BF16)
    kv_scr[n_lat:, :] = u_ref[0, n_lat:, qw:qw + 2 * hd]
    kc = kv_scr[n_lat:, 0:hd]
    vc = kv_scr[n_lat:, hd:2 * hd]

    def key_start(i):
        return pl.multiple_of(jnp.clip((i - 1) * blk, 0, n_lat - span), blk)

    def scores(i, s_ref):
        rows = pl.ds(pl.multiple_of(i * blk, blk), blk)
        q = _rope(u_ref[0, rows, 0:qw].astype(F32), cq_ref[rows, :], saq_ref[rows, :], sbq_ref[rows, :]) * scale
        q = q.astype(BF16)
        kl = kv_scr[pl.ds(key_start(i), span), 0:hd]
        for r in range(rep):
            s_ref[r, :, 0:n_ctx] = _mm_nt(q[:, r * hd:(r + 1) * hd], kc)
            s_ref[r, :, n_ctx:] = _mm_nt(q[:, r * hd:(r + 1) * hd], kl)

    def attend(i, s_ref):
        r0 = pl.multiple_of(i * blk, blk)
        ks = key_start(i)
        vl = kv_scr[pl.ds(ks, span), hd:2 * hd]
        mask = jnp.abs((r0 + _iota((blk, span), 0)) - (ks + _iota((blk, span), 1))) <= SW_WINDOW
        for r in range(rep):
            sink = sink_ref[g * rep + r]
            sc = s_ref[r, :, 0:n_ctx]
            sl = jnp.where(mask, s_ref[r, :, n_ctx:], NEG_INF)
            m = jnp.maximum(jnp.maximum(jnp.max(sc, axis=-1, keepdims=True), jnp.max(sl, axis=-1, keepdims=True)), sink)
            pc = jnp.exp(sc - m)
            pl_ = jnp.exp(sl - m)
            den = jnp.exp(sink - m) + jnp.sum(pc, axis=-1, keepdims=True) + jnp.sum(pl_, axis=-1, keepdims=True)
            o_ref[0, pl.ds(r0, blk), r * hd:(r + 1) * hd] = (_mm(pc, vc) + _mm(pl_, vl)) / den

    scores(0, s_a)

    def pair(j, carry):
        scores(2 * j + 1, s_b)
        attend(2 * j, s_a)
        scores(jnp.minimum(2 * j + 2, nb - 1), s_a)
        attend(2 * j + 1, s_b)
        return carry

    lax.fori_loop(0, nb // 2, pair, 0)
    qc = (u_ref[0, n_lat:, 0:qw].astype(F32) * scale).astype(BF16)
    for r in range(rep):
        o_ref[0, n_lat:, r * hd:(r + 1) * hd] = _sink_attend(qc[:, r * hd:(r + 1) * hd], [(kc, vc, None)], sink_ref[g * rep + r])


def _swa_mixer(u, n_lat, n_ctx, sink):
    bsz, t, _ = u.shape
    hd, rep = SW_HD, SW_REP
    gw = rep * hd + 2 * hd
    blk = SW_WINDOW
    assert n_lat % (2 * blk) == 0 and n_lat >= 3 * blk and n_lat % GRID_W == 0
    cq, saq, sbq = _rope_tables(n_lat, rep * hd, rep)
    ck, sak, sbk = _rope_tables(n_lat, 2 * hd, 1)
    body = functools.partial(_swa_body, n_lat=n_lat, n_ctx=n_ctx, blk=blk)
    tab = lambda w: pl.BlockSpec((n_lat, w), lambda b, g: (0, 0))
    return pl.pallas_call(
        body, out_shape=S((bsz, t, SW_HQ * hd), F32), grid=(bsz, SW_HKV),
        in_specs=[pl.BlockSpec((1, t, gw), lambda b, g: (b, 0, g)),
                  tab(rep * hd), tab(rep * hd), tab(rep * hd), tab(2 * hd), tab(2 * hd), tab(2 * hd),
                  pl.BlockSpec(memory_space=pltpu.SMEM)],
        out_specs=pl.BlockSpec((1, t, rep * hd), lambda b, g: (b, 0, g)),
        scratch_shapes=[pltpu.VMEM((t, 2 * hd), BF16), pltpu.VMEM((rep, blk, n_ctx + 3 * blk), F32),
                        pltpu.VMEM((rep, blk, n_ctx + 3 * blk), F32)],
        compiler_params=_cp(("parallel", "parallel")), name="swa_attention",
    )(u, cq, saq, sbq, ck, sak, sbk, sink)


def _seq_conv(x, w_ref, n_lat):
    t = x.shape[0]
    row = _iota((t, 1), 0)
    first = (row == 0) | (row == n_lat)
    last = (row == n_lat - 1) | (row == t - 1)
    xp = jnp.where(first, 0.0, pltpu.roll(x, 1, 0))
    xn = jnp.where(last, 0.0, pltpu.roll(x, t - 1, 0))
    return w_ref[0:1, :] * xp + w_ref[1:2, :] * x + w_ref[2:3, :] * xn


def _chunk_scan(x, reverse):
    t = x.shape[0]
    pos = _iota((t, 1), 0) & (CHUNK - 1)
    s = 1
    while s < CHUNK:
        if reverse:
            x = x + jnp.where(pos < CHUNK - s, pltpu.roll(x, t - s, 0), 0.0)
        else:
            x = x + jnp.where(pos >= s, pltpu.roll(x, s, 0), 0.0)
        s *= 2
    return x


def _chunk_scan_lanes(x, reverse):
    t = x.shape[1]
    pos = _iota((1, t), 1) & (CHUNK - 1)
    s = 1
    while s < CHUNK:
        if reverse:
            x = x + jnp.where(pos < CHUNK - s, pltpu.roll(x, t - s, 1), 0.0)
        else:
            x = x + jnp.where(pos >= s, pltpu.roll(x, s, 1), 0.0)
        s *= 2
    return x


def _chunk_order(s, n_lat_chunks, n_ctx_chunks, direction):
    if direction == 0:
        return jnp.where(s < n_ctx_chunks, n_lat_chunks + s, s - n_ctx_chunks)
    return n_lat_chunks + n_ctx_chunks - 1 - s


def _intra_unroll(n_super, sup=SUPER):
    return next(u for u in range(INTRA_ROWS // sup, 0, -1) if n_super % u == 0)


def _gated_rms(o, nw, z):
    return o * lax.rsqrt(jnp.mean(o * o, axis=-1, keepdims=True) + EPS) * nw * _silu(z)


def _bmm(a, b):
    return jnp.einsum('bij,bjk->bik', a.astype(BF16), b.astype(BF16), preferred_element_type=F32)


def _bmm_nt(a, b):
    return jnp.einsum('bik,bjk->bij', a.astype(BF16), b.astype(BF16), preferred_element_type=F32)


def _unit_tri_inverse(a, ii, jj):
    eye = (ii == jj).astype(F32)
    a8 = jnp.where((ii >> 3) == (jj >> 3), a, 0.0)
    a8_2 = _bmm(a8, a8)
    a8_4 = _bmm(a8_2, a8_2)
    x = _bmm(_bmm(eye - a8, eye + a8_2), eye + a8_4)
    sh = 3
    while (1 << sh) < CHUNK:
        e = jnp.where(((ii >> (sh + 1)) == (jj >> (sh + 1))) & ((ii >> sh) != (jj >> sh)), a, 0.0)
        x = x - _bmm(_bmm(x, e), x)
        sh += 1
    return x


def _gdn_body(q_ref, k_ref, v_ref, z_ref, ba_ref, cwq_ref, cwk_ref, cwv_ref, par_ref, nw_ref, o_ref,
              kn_s, qn_s, qe_s, kk_s, egl_s, kb_s, kbe_s, vb_s, gc_s, gt_s, qp_s, k2_s, n_s, oacc,
              *, n_lat, n_ctx):
    t = n_lat + n_ctx
    h = pl.program_id(1)
    lane = _iota((1, LANES), 1)
    q = _silu(_seq_conv(q_ref[0].astype(F32), cwq_ref, n_lat))
    k = _silu(_seq_conv(k_ref[0].astype(F32), cwk_ref, n_lat))
    v = _silu(_seq_conv(v_ref[0].astype(F32), cwv_ref, n_lat))
    qn = q * lax.rsqrt(jnp.sum(q * q, axis=-1, keepdims=True) + EPS) * (GD_DK ** -0.5)
    kn = k * lax.rsqrt(jnp.sum(k * k, axis=-1, keepdims=True) + EPS)
    kn_s[...] = kn.astype(BF16)
    qn_s[...] = qn.astype(BF16)
    ba = ba_ref[0].astype(F32)
    par = par_ref[...]

    def pick(x, c):
        return jnp.sum(jnp.where(lane == c, x, 0.0), axis=-1, keepdims=True)

    gpack = jnp.zeros((t, LANES), F32)
    for d in range(2):
        a_log = pick(par[d:d + 1, :], h)
        dt_b = pick(par[2 + d:3 + d, :], h)
        g = -jnp.exp(a_log) * _softplus(pick(ba, 2 * GD_H + d * GD_H + h) + dt_b)
        gpack = jnp.where(lane == d, g, gpack)
    g_t = gpack.T[0:8, :]
    pre = _chunk_scan_lanes(g_t, False)
    suf = _chunk_scan_lanes(g_t, True)
    row8 = _iota((8, 1), 0)
    gcum_t = jnp.where(row8 == 1, suf, pre)
    gt_s[...] = gcum_t
    packed = jnp.where(row8 < 2, gcum_t, pltpu.roll(pre + suf - g_t, 2, 0))
    cols = jnp.concatenate([packed, jnp.zeros((LANES - 8, t), F32)], axis=0).T
    gc_s[...] = cols
    for d in range(2):
        beta = _sigmoid(pick(ba, d * GD_H + h))
        gc = jnp.broadcast_to(cols[:, d:d + 1], (t, LANES))
        glast = jnp.broadcast_to(cols[:, 2 + d:3 + d], (t, LANES))
        eg = jnp.exp(gc)
        kb = kn * beta
        qe_s[d] = (qn * eg).astype(BF16)
        kk_s[d] = (kn * jnp.exp(glast - gc)).astype(BF16)
        egl_s[d] = jnp.exp(glast)
        kb_s[d] = kb.astype(BF16)
        kbe_s[d] = (kb * eg).astype(BF16)
        vb_s[d] = (v * beta).astype(BF16)

    sup = GD_SUPER
    ii = _iota((sup, sup), 0)
    jj = _iota((sup, sup), 1)
    same = (ii >> 6) == (jj >> 6)
    per = sup // CHUNK
    unroll = _intra_unroll(t // sup, sup)

    def intra(it, carry):
        r0s = [pl.multiple_of((it * unroll + kq) * sup, sup) for kq in range(unroll)]
        kcs = [kn_s[pl.ds(r0, sup), :] for r0 in r0s]
        pairs = [(kq, d) for kq in range(unroll) for d in range(2)]
        decs = []
        for kq, d in pairs:
            rows = pl.ds(r0s[kq], sup)
            incl = same & ((jj >= ii) if d else (jj <= ii))
            dif = jnp.broadcast_to(gc_s[rows, d:d + 1], (sup, sup)) - gt_s[d:d + 1, rows]
            decs.append(jnp.where(incl, jnp.exp(jnp.where(incl, dif, 0.0)), 0.0))
        dec = jnp.stack(decs)
        kc2 = jnp.stack([kcs[kq] for kq, _ in pairs])
        kb = jnp.stack([kb_s[d, pl.ds(r0s[kq], sup), :] for kq, d in pairs])
        rhs = jnp.stack([jnp.concatenate([vb_s[d, pl.ds(r0s[kq], sup), :], kbe_s[d, pl.ds(r0s[kq], sup), :]], axis=1)
                         for kq, d in pairs])
        a = jnp.where(ii == jj, 0.0, _bmm_nt(kb, kc2) * dec)
        uw = _bmm(_unit_tri_inverse(a, ii, jj), rhs).astype(BF16)
        qk = _bmm_nt(jnp.stack([qn_s[pl.ds(r0, sup), :] for r0 in r0s]), jnp.stack(kcs))
        auw = _bmm(jnp.stack([qk[kq] for kq, _ in pairs]) * dec, uw)
        for i, (kq, d) in enumerate(pairs):
            rows = pl.ds(r0s[kq], sup)
            qp_s[d, rows, :] = (qe_s[d, rows, :].astype(F32) - auw[i, :, LANES:]).astype(BF16)
            if d == 1:
                oacc[rows, :] = auw[i - 1, :, 0:LANES] + auw[i, :, 0:LANES]
            for c4 in range(per):
                kuw = _mm_tn(kk_s[d, pl.ds(r0s[kq] + c4 * CHUNK, CHUNK), :], uw[i, c4 * CHUNK:(c4 + 1) * CHUNK])
                crow = pl.ds(pl.multiple_of(((it * unroll + kq) * per + c4) * GD_DK, GD_DK), GD_DK)
                n_s[d, crow, :] = kuw[:, 0:LANES]
                k2_s[d, crow, :] = kuw[:, LANES:].astype(BF16)
        return carry

    lax.fori_loop(0, t // sup // unroll, intra, 0)

    nl, nc = n_lat // CHUNK, n_ctx // CHUNK

    def step(s, states):
        new = []
        for d in range(2):
            st = states[d]
            sb = st.astype(BF16)
            c = _chunk_order(s, nl, nc, d)
            rows = pl.ds(pl.multiple_of(c * CHUNK, CHUNK), CHUNK)
            crow = pl.ds(pl.multiple_of(c * GD_DK, GD_DK), GD_DK)
            oacc[rows, :] += jnp.dot(qp_s[d, rows, :], sb, preferred_element_type=F32)
            new.append(st * egl_s[d, pl.ds(c * CHUNK, 1), :] + n_s[d, crow, :]
                       - jnp.dot(k2_s[d, crow, :], sb, preferred_element_type=F32))
        return tuple(new)

    zero = jnp.zeros((GD_DK, LANES), F32)
    lax.fori_loop(0, nl + nc, step, (zero, zero))
    o_ref[0] = _gated_rms(oacc[...], nw_ref[...], z_ref[0].astype(F32))


def _gdn_mixer(u, n_lat, n_ctx, conv_w, a_log, dt_bias, norm_w):
    bsz, t, _ = u.shape
    assert t % SUPER == 0 and n_lat % SUPER == 0
    par = jnp.zeros((8, LANES), F32).at[0:2, :GD_H].set(a_log).at[2:4, :GD_H].set(dt_bias)
    body = functools.partial(_gdn_body, n_lat=n_lat, n_ctx=n_ctx)
    sec = lambda s: pl.BlockSpec((1, t, LANES), lambda b, h: (b, 0, s * GD_H + h))
    cw = lambda s: pl.BlockSpec((3, LANES), lambda b, h: (0, s * GD_H + h))
    both = lambda dt: pltpu.VMEM((2, t, LANES), dt)
    state_rows = (t // CHUNK) * GD_DK
    return pl.pallas_call(
        body, out_shape=S((bsz, t, GD_H * LANES), F32), grid=(bsz, GD_H),
        in_specs=[sec(0), sec(1), sec(2), sec(3),
                  pl.BlockSpec((1, t, LANES), lambda b, h: (b, 0, 4 * GD_H)),
                  cw(0), cw(1), cw(2),
                  pl.BlockSpec((8, LANES), lambda b, h: (0, 0)),
                  pl.BlockSpec((1, LANES), lambda b, h: (0, 0))],
        out_specs=pl.BlockSpec((1, t, LANES), lambda b, h: (b, 0, h)),
        scratch_shapes=[pltpu.VMEM((t, LANES), BF16), pltpu.VMEM((t, LANES), BF16),
                        both(BF16), both(BF16), both(F32), both(BF16), both(BF16), both(BF16),
                        pltpu.VMEM((t, LANES), F32),
                        pltpu.VMEM((8, t), F32), both(BF16), pltpu.VMEM((2, state_rows, LANES), BF16),
                        pltpu.VMEM((2, state_rows, LANES), F32), pltpu.VMEM((t, LANES), F32)],
        compiler_params=_cp(("parallel", "parallel"), 56), name="gdn_mixer",
    )(u, u, u, u, u, conv_w, conv_w, conv_w, par, norm_w.reshape(1, LANES))


def _hgrn_body(q_ref, ff_ref, fb_ref, i_ref, g_ref, lbp_ref, nw_ref, o_ref,
               qe_s, ke_s, kk_s, egl_s, v_s, n_s, st_s, oacc, *, n_lat, n_ctx, layer):
    t = n_lat + n_ctx
    e = jnp.exp(lbp_ref[...] - jnp.max(lbp_ref[...], axis=0, keepdims=True))
    lb = jnp.sum(e[1:layer + 1, :], axis=0, keepdims=True) / jnp.sum(e, axis=0, keepdims=True)
    q = _silu(q_ref[0].astype(F32))
    v_s[...] = i_ref[0]
    for d, f_ref in enumerate((ff_ref, fb_ref)):
        f = f_ref[0].astype(F32)
        sig = _sigmoid(f)
        logf = jnp.log(lb + (1.0 - lb) * sig)
        kin = (1.0 - lb) * (1.0 - sig)
        gc = _chunk_scan(logf, bool(d))
        ends = gc.reshape(t // CHUNK, CHUNK, LANES)[:, 0:1, :] if d else gc.reshape(t // CHUNK, CHUNK, LANES)[:, CHUNK - 1:, :]
        glast = jnp.broadcast_to(ends, (t // CHUNK, CHUNK, LANES)).reshape(t, LANES)
        qe_s[d] = (q * jnp.exp(gc)).astype(BF16)
        ke_s[d] = (kin * jnp.exp(-gc)).astype(BF16)
        kk_s[d] = (kin * jnp.exp(glast - gc)).astype(BF16)
        egl_s[d] = jnp.exp(glast)

    ii = _iota((SUPER, SUPER), 0)
    jj = _iota((SUPER, SUPER), 1)
    same = (ii >> 6) == (jj >> 6)
    per = SUPER // CHUNK
    unroll = _intra_unroll(t // SUPER)

    def chunk_rows(sc, c4):
        rows = pl.ds(pl.multiple_of(sc * SUPER + c4 * CHUNK, CHUNK), CHUNK)
        crow = pl.ds(pl.multiple_of((sc * per + c4) * LANES, LANES), LANES)
        return rows, crow

    def intra(it, carry):
        pairs = [(kq, d) for kq in range(unroll) for d in range(2)]
        rows = [pl.ds(pl.multiple_of((it * unroll + kq) * SUPER, SUPER), SUPER) for kq in range(unroll)]
        incl = jnp.stack([same & ((jj >= ii) if d else (jj <= ii)) for _, d in pairs])
        at = jnp.where(incl, _bmm_nt(jnp.stack([qe_s[d, rows[kq], :] for kq, d in pairs]),
                                     jnp.stack([ke_s[d, rows[kq], :] for kq, d in pairs])), 0.0)
        part = _bmm(at, jnp.stack([v_s[rows[kq], :] for kq, _ in pairs]))
        for kq in range(unroll):
            oacc[rows[kq], :] = part[2 * kq] + part[2 * kq + 1]
            for d in range(2):
                for c4 in range(per):
                    r64, crow = chunk_rows(it * unroll + kq, c4)
                    n_s[d, crow, :] = _mm_tn(v_s[r64, :], kk_s[d, r64, :])
        return carry

    lax.fori_loop(0, t // SUPER // unroll, intra, 0)
    nl, nc = n_lat // CHUNK, n_ctx // CHUNK

    def scan(s, states):
        new = []
        for d in range(2):
            c = _chunk_order(s, nl, nc, d)
            crow = pl.ds(pl.multiple_of(c * LANES, LANES), LANES)
            st_s[d, crow, :] = states[d].astype(BF16)
            new.append(states[d] * egl_s[d, pl.ds(c * CHUNK, 1), :] + n_s[d, crow, :])
        return tuple(new)

    zero = jnp.zeros((LANES, HG_DK), F32)
    lax.fori_loop(0, nl + nc, scan, (zero, zero))

    def inter(it, carry):
        for kq in range(unroll):
            sc = it * unroll + kq
            for c4 in range(per):
                r64, crow = chunk_rows(sc, c4)
                oacc[r64, :] += (_mm_nt(qe_s[0, r64, :], st_s[0, crow, :]) + _mm_nt(qe_s[1, r64, :], st_s[1, crow, :]))
        return carry

    lax.fori_loop(0, t // SUPER // unroll, inter, 0)
    o_ref[0] = _gated_rms(oacc[...], nw_ref[...], g_ref[0].astype(F32))


def _hgrn_mixer(u, n_lat, n_ctx, hg_lb, norm_w, layer):
    bsz, t, n5 = u.shape
    d = n5 // 5
    nh = d // HG_DK
    depth = hg_lb.shape[0]
    assert t % SUPER == 0 and n_lat % SUPER == 0
    body = functools.partial(_hgrn_body, n_lat=n_lat, n_ctx=n_ctx, layer=layer)
    sec = lambda s: pl.BlockSpec((1, t, LANES), lambda b, h: (b, 0, s * nh + h))
    both = lambda dt: pltpu.VMEM((2, t, LANES), dt)
    state_rows = (t // CHUNK) * LANES
    return pl.pallas_call(
        body, out_shape=S((bsz, t, d), F32), grid=(bsz, nh),
        in_specs=[sec(0), sec(1), sec(2), sec(3), sec(4),
                  pl.BlockSpec((depth, LANES), lambda b, h: (0, h)),
                  pl.BlockSpec((1, LANES), lambda b, h: (0, 0))],
        out_specs=pl.BlockSpec((1, t, LANES), lambda b, h: (b, 0, h)),
        scratch_shapes=[both(BF16), both(BF16), both(BF16), both(F32), pltpu.VMEM((t, LANES), BF16),
                        pltpu.VMEM((2, state_rows, LANES), F32), pltpu.VMEM((2, state_rows, LANES), BF16),
                        pltpu.VMEM((t, LANES), F32)],
        compiler_params=_cp(("parallel", "parallel")), name="hgrn2_mixer",
    )(u, u, u, u, u, hg_lb, norm_w.reshape(1, LANES))


def _excl_count_lanes(x):
    n = x.shape[1]
    blk = min(n, 256)
    tri = jnp.where(_iota((blk, blk), 0) < _iota((blk, blk), 1), 1.0, 0.0).astype(BF16)
    run = jnp.zeros((x.shape[0], 1), F32)
    outs = []
    for r in range(n // blk):
        xb = x[:, r * blk:(r + 1) * blk]
        outs.append(jnp.dot(xb.astype(BF16), tri, preferred_element_type=F32) + run)
        run = run + jnp.sum(xb, axis=1, keepdims=True)
    return jnp.concatenate(outs, axis=1) if len(outs) > 1 else outs[0]


def _topcap_slots(groups):
    def count_ge(v, thr):
        return jnp.sum(jnp.where(v >= thr, 1.0, 0.0), axis=1, keepdims=True)

    def bisect(_, carry):
        new = []
        for (v, cap), (lo, hi) in zip(groups, carry):
            mid = jnp.sqrt(jnp.maximum(lo, 1e-37)) * jnp.sqrt(hi)
            ok = count_ge(v, mid) >= float(cap)
            new.append((jnp.where(ok, mid, lo), jnp.where(ok, hi, mid)))
        return tuple(new)

    init = tuple((jnp.zeros((v.shape[0], 1), F32), jnp.full((v.shape[0], 1), 2.0, F32)) for v, _ in groups)
    brackets = lax.fori_loop(0, 34, bisect, init)
    codes = []
    for (v, cap), (lo, hi) in zip(groups, brackets):
        thr, found, upper = lo, jnp.zeros_like(lo), hi
        for _ in range(4):
            m = jnp.max(jnp.where(v < upper, v, -1.0), axis=1, keepdims=True)
            ok = jnp.where(count_ge(v, m) >= float(cap), 1.0, 0.0) * (1.0 - found)
            thr = jnp.where(ok > 0, m, thr)
            found = jnp.maximum(found, ok)
            upper = jnp.where(found > 0, upper, m)
        gt = jnp.where(v > thr, 1.0, 0.0)
        eq = jnp.where(v == thr, 1.0, 0.0)
        need = float(cap) - jnp.sum(gt, axis=1, keepdims=True)
        sel = jnp.maximum(gt, eq * jnp.where(_excl_count_lanes(eq) < need, 1.0, 0.0))
        codes.append(jnp.where(sel > 0, _excl_count_lanes(sel), -1.0))
    return codes


def _route_body(lg_ref, aff_ref, code_ref, codet_ref, *, n_lat, cap_l, cap_c):
    t = lg_ref.shape[1]
    lane_ok = _iota((1, LANES), 1) < N_EXPERTS
    lg = jnp.where(lane_ok, lg_ref[0], NEG_INF)
    e = jnp.exp(lg - jnp.max(lg, axis=-1, keepdims=True))
    aff = e / jnp.sum(e, axis=-1, keepdims=True)
    aff_ref[0] = aff
    aff_t = aff.T[0:N_EXPERTS, :]
    code_l, code_c = _topcap_slots([(aff_t[:, 0:n_lat], cap_l), (aff_t[:, n_lat:], cap_c)])
    code_t = jnp.concatenate([code_l, code_c], axis=1)
    codet_ref[0] = code_t
    code_ref[0] = jnp.concatenate([code_t, jnp.full((LANES - N_EXPERTS, t), -1.0, F32)], axis=0).T


def _route(logits, n_lat, cap_l, cap_c):
    bsz, t, _ = logits.shape
    body = functools.partial(_route_body, n_lat=n_lat, cap_l=cap_l, cap_c=cap_c)
    blk = pl.BlockSpec((1, t, LANES), lambda b: (b, 0, 0))
    return pl.pallas_call(
        body, out_shape=(S((bsz, t, LANES), F32), S((bsz, t, LANES), F32), S((bsz, N_EXPERTS, t), F32)), grid=(bsz,),
        in_specs=[blk], out_specs=(blk, blk, pl.BlockSpec((1, N_EXPERTS, t), lambda b: (b, 0, 0))),
        compiler_params=_cp(("parallel",)), name="moe_route",
    )(logits)


def _gather_body(codet_ref, h_ref, xl_ref, xc_ref, *, n_lat, n_ctx, cap_l, cap_c, grp):
    hl = h_ref[0, 0:n_lat, :]
    hc = h_ref[0, n_lat:, :]
    il = _iota((cap_l, n_lat), 0).astype(F32)
    ic = _iota((cap_c, n_ctx), 0).astype(F32)
    for e0 in range(0, N_EXPERTS, grp):
        onehot = jnp.concatenate([jnp.where(il == codet_ref[0, e:e + 1, 0:n_lat], 1.0, 0.0).astype(BF16)
                                  for e in range(e0, e0 + grp)], axis=0)
        xe = jnp.dot(onehot, hl, preferred_element_type=F32).astype(BF16)
        for r in range(grp):
            xl_ref[0, e0 + r] = xe[r * cap_l:(r + 1) * cap_l]
    onehot = jnp.concatenate([jnp.where(ic == codet_ref[0, e:e + 1, n_lat:], 1.0, 0.0).astype(BF16)
                              for e in range(N_EXPERTS)], axis=0)
    xc_ref[0] = jnp.dot(onehot, hc, preferred_element_type=F32).astype(BF16)


def _gather(codet, h, n_lat, cap_l, cap_c):
    bsz, t, d = h.shape
    body = functools.partial(_gather_body, n_lat=n_lat, n_ctx=t - n_lat, cap_l=cap_l, cap_c=cap_c, grp=4)
    return pl.pallas_call(
        body, out_shape=(S((bsz, N_EXPERTS, cap_l, d), BF16), S((bsz, N_EXPERTS * cap_c, d), BF16)), grid=(bsz,),
        in_specs=[pl.BlockSpec((1, N_EXPERTS, t), lambda b: (b, 0, 0)), pl.BlockSpec((1, t, d), lambda b: (b, 0, 0))],
        out_specs=(pl.BlockSpec((1, N_EXPERTS, cap_l, d), lambda b: (b, 0, 0, 0)),
                   pl.BlockSpec((1, N_EXPERTS * cap_c, d), lambda b: (b, 0, 0))),
        compiler_params=_cp(("parallel",)), name="moe_gather",
    )(codet, h)


def _ffn_body(xl_ref, xc_ref, wg_ref, wu_ref, wd_ref, yl_ref, yc_ref, x_scr, acc, *, nb, cap_l, cap_c):
    f = pl.program_id(1)
    d = x_scr.shape[1]

    @pl.when(f == 0)
    def _():
        x_scr[0:nb * cap_l, :] = xl_ref[:, 0].reshape(nb * cap_l, d)
        x_scr[nb * cap_l:, :] = xc_ref[:, 0].reshape(nb * cap_c, d)
        acc[...] = jnp.zeros_like(acc)

    x = x_scr[...]
    a = jnp.dot(x, wg_ref[0, 0].astype(BF16), preferred_element_type=F32)
    u = jnp.dot(x, wu_ref[0, 0].astype(BF16), preferred_element_type=F32)
    acc[...] += _mm(_silu(a) * u, wd_ref[0, 0])

    @pl.when(f == pl.num_programs(1) - 1)
    def _():
        y = acc[...].astype(BF16)
        yl_ref[:, 0] = y[0:nb * cap_l].reshape(nb, cap_l, d)
        yc_ref[:, 0] = y[nb * cap_l:].reshape(nb, cap_c, d)


def _expert_ffn(xl, xc, w_gate, w_up, w_down, layer):
    bsz, ne, cap_l, d = xl.shape
    cap_c = xc.shape[2]
    ffd = w_gate.shape[-1]
    tf = 256
    rows = bsz * (cap_l + cap_c)
    body = functools.partial(_ffn_body, nb=bsz, cap_l=cap_l, cap_c=cap_c)
    return pl.pallas_call(
        body, out_shape=(S(xl.shape, BF16), S(xc.shape, BF16)), grid=(ne, ffd // tf),
        in_specs=[pl.BlockSpec((bsz, 1, cap_l, d), lambda e, f: (0, e, 0, 0)),
                  pl.BlockSpec((bsz, 1, cap_c, d), lambda e, f: (0, e, 0, 0)),
                  pl.BlockSpec((1, 1, d, tf), lambda e, f: (layer, e, 0, f)),
                  pl.BlockSpec((1, 1, d, tf), lambda e, f: (layer, e, 0, f)),
                  pl.BlockSpec((1, 1, tf, d), lambda e, f: (layer, e, f, 0))],
        out_specs=(pl.BlockSpec((bsz, 1, cap_l, d), lambda e, f: (0, e, 0, 0)),
                   pl.BlockSpec((bsz, 1, cap_c, d), lambda e, f: (0, e, 0, 0))),
        scratch_shapes=[pltpu.VMEM((rows, d), BF16), pltpu.VMEM((rows, d), F32)],
        compiler_params=_cp(("parallel", "arbitrary")), name="moe_expert_ffn",
    )(xl, xc, w_gate, w_up, w_down)


def _combine_body(code_ref, aff_ref, yl_ref, yc_ref, res_ref, gate_ref, fw_ref, o_ref,
                  *, n_lat, n_b, cap_l, cap_c, tm, final):
    b, i = pl.program_id(0), pl.program_id(1)
    code = code_ref[0]
    aff = aff_ref[0]

    def scatter(cap, y):
        slot = _iota((tm, cap), 1).astype(F32)
        q = jnp.concatenate([jnp.where(code[:, e:e + 1] == slot, aff[:, e:e + 1], 0.0) for e in range(N_EXPERTS)], axis=1)
        return jnp.dot(q.astype(BF16), y, preferred_element_type=F32)

    @pl.when(i * tm < n_lat)
    def _():
        x = res_ref[0] + gate_ref[pl.ds(b, 1), :] * scatter(cap_l, yl_ref[0])
        if final:
            x = x * lax.rsqrt(jnp.mean(x * x, axis=-1, keepdims=True) + EPS) * fw_ref[...]
        o_ref[0] = x

    @pl.when(i * tm >= n_lat)
    def _():
        o_ref[0] = res_ref[0] + gate_ref[n_b:n_b + 1, :] * scatter(cap_c, yc_ref[0])


def _combine(code, aff, yl, yc, res, mod, n_lat, final_w, final):
    bsz, t, d = res.shape
    cap_l, cap_c = yl.shape[2], yc.shape[2]
    tm = t - n_lat
    assert n_lat % tm == 0
    rows = n_lat if final else t
    body = functools.partial(_combine_body, n_lat=n_lat, n_b=bsz, cap_l=cap_l, cap_c=cap_c, tm=tm, final=final)
    tok = lambda w: pl.BlockSpec((1, tm, w), lambda b, i: (b, i, 0))
    return pl.pallas_call(
        body, out_shape=S((bsz, rows, d), F32), grid=(bsz, rows // tm),
        in_specs=[tok(LANES), tok(LANES),
                  pl.BlockSpec((1, N_EXPERTS * cap_l, d), lambda b, i: (b, 0, 0)),
                  pl.BlockSpec((1, N_EXPERTS * cap_c, d), lambda b, i: (b, 0, 0)),
                  tok(d), pl.BlockSpec((MOD_ROWS, d), lambda b, i: (0, 5)), pl.BlockSpec((1, d), lambda b, i: (0, 0))],
        out_specs=tok(d),
        compiler_params=_cp(("parallel", "parallel")), name="moe_combine",
    )(code, aff, yl.reshape(bsz, N_EXPERTS * cap_l, d), yc.reshape(bsz, N_EXPERTS * cap_c, d), res, mod,
      final_w.reshape(1, d))


def _moe_layer(xs, h, logits, mod, w_gate, w_up, w_down, layer, n_lat, final_w, final):
    bsz, t, d = xs.shape
    n_ctx = t - n_lat
    cap_l = EC_CAPACITY * n_lat // N_EXPERTS
    cap_c = EC_CAPACITY * n_ctx // N_EXPERTS
    aff, code, codet = _route(logits, n_lat, cap_l, cap_c)
    xl, xc = _gather(codet, h, n_lat, cap_l, cap_c)
    yl, yc = _expert_ffn(xl, xc.reshape(bsz, N_EXPERTS, cap_c, d), w_gate, w_up, w_down, layer)
    return _combine(code, aff, yl, yc, xs, mod, n_lat, final_w, final)


def _swa_group_columns():
    hd, rep = SW_HD, SW_REP
    cols = []
    for g in range(SW_HKV):
        cols += list(range(g * rep * hd, (g + 1) * rep * hd))
        cols += list(range(SW_HQ * hd + g * hd, SW_HQ * hd + (g + 1) * hd))
        cols += list(range((SW_HQ + SW_HKV) * hd + g * hd, (SW_HQ + SW_HKV) * hd + (g + 1) * hd))
    return np.asarray(cols, np.int32)


def kernel(x, c, ctx, c_ctx, ada_w, ada_b, norm1_w, norm2_w, final_norm_w, hy_w_in, hy_b_in, hy_short_w, hy_short_b, hy_ffn_w1, hy_ffn_b1, hy_ffn_w2, hy_ffn_b2, hy_ffn_w3, hy_sin_freq, hy_filter_bias, hy_w_out, hy_b_out, sw_w_in, sw_sink, sw_w_out, gd_w_in, gd_conv_w, gd_a_log, gd_dt_bias, gd_norm_w, gd_w_out, hg_w_in, hg_lb, hg_norm_w, hg_w_out, moe_router, moe_w_gate, moe_w_up, moe_w_down):
    bsz, n_lat, d = x.shape
    n_ctx = ctx.shape[1]
    depth = ada_w.shape[0]
    assert bsz < MOD_ROWS and n_lat % n_ctx == 0
    xs = jnp.concatenate([x, ctx], axis=1)
    c16 = jnp.zeros((MOD_ROWS, d), F32).at[:bsz].set(c).at[bsz].set(c_ctx)
    mod = _modulation(c16, ada_w, ada_b)
    zero_bias = jnp.zeros((d,), F32)
    gd_pad = (-gd_w_in.shape[1]) % IN_PROJ_TN
    gd_w = jnp.pad(gd_w_in, ((0, 0), (0, gd_pad))).astype(BF16)
    sw_w = sw_w_in[:, _swa_group_columns()].astype(BF16)
    for layer in range(depth):
        m = mod[layer]
        nw = norm1_w[layer]
        kind = layer % 4
        if kind == 0:
            u = _in_proj(xs, nw, m, 0, 1, hy_w_in.astype(BF16), hy_b_in, n_lat, "hy_in_proj")
            y = _hyena_mixer(u, n_lat, n_ctx, hy_short_w, hy_short_b, hy_ffn_w1, hy_ffn_b1, hy_ffn_w2, hy_ffn_b2,
                             hy_ffn_w3, hy_sin_freq, hy_filter_bias)
            w_out, b_out, name = hy_w_out, hy_b_out, "hy_out_proj"
        elif kind == 1:
            u = _in_proj(xs, nw, m, 0, 1, sw_w, jnp.zeros((sw_w.shape[1],), F32), n_lat, "sw_in_proj")
            y = _swa_mixer(u, n_lat, n_ctx, sw_sink)
            w_out, b_out, name = sw_w_out, zero_bias, "sw_out_proj"
        elif kind == 2:
            u = _in_proj(xs, nw, m, 0, 1, gd_w, jnp.zeros((gd_w.shape[1],), F32), n_lat, "gd_in_proj")
            y = _gdn_mixer(u, n_lat, n_ctx, gd_conv_w, gd_a_log, gd_dt_bias, gd_norm_w)
            w_out, b_out, name = gd_w_out, zero_bias, "gd_out_proj"
        else:
            u = _in_proj(xs, nw, m, 0, 1, hg_w_in.astype(BF16), jnp.zeros((hg_w_in.shape[1],), F32), n_lat, "hg_in_proj")
            y = _hgrn_mixer(u, n_lat, n_ctx, hg_lb, hg_norm_w, layer)
            w_out, b_out, name = hg_w_out, zero_bias, "hg_out_proj"
        router_w = jnp.pad(moe_router[layer], ((0, 0), (0, LANES - N_EXPERTS)))
        router_hi = router_w.astype(BF16)
        router_p = jnp.concatenate([router_hi, (router_w - router_hi.astype(F32)).astype(BF16)], axis=1)
        xs, h, logits = _out_proj(y, w_out.astype(BF16), b_out, xs, m, n_lat, norm2_w[layer], router_p, name)
        xs = _moe_layer(xs, h, logits, m, moe_w_gate, moe_w_up, moe_w_down, layer, n_lat, final_norm_w,
                        layer == depth - 1)
    return xs
```

```python
import functools
import math

import jax
import jax.numpy as jnp
import numpy as np
from jax import lax
from jax.experimental import pallas as pl
from jax.experimental.pallas import tpu as pltpu

F32 = jnp.float32
BF16 = jnp.bfloat16
HIGHEST = lax.Precision.HIGHEST
EPS = 1e-6
NEG_INF = -1e30
LANES = 128
BF16_ROWS = 16
MOD_ROWS = 16

GRID_W = 64
HY_ORDER = 2
HY_EMB = 33
HY_FAST_DECAY = 0.3
HY_SLOW_DECAY = 1.5
HY_TARGET = 1e-2
HY_SHIFT = 0.05
SW_HQ, SW_HKV, SW_HD, SW_WINDOW = 16, 4, 64, 128
SW_REP = SW_HQ // SW_HKV
ROPE_BASE = 10000.0
GD_H, GD_DK = 8, 128
HG_DK = 128
CHUNK = 64
SUPER = 256
GD_SUPER = 128
GD_HEADS_PER_STEP = 2
INTRA_ROWS = 768
IN_PROJ_TN = 512
N_EXPERTS = 16
EC_CAPACITY = 2

S = jax.ShapeDtypeStruct


def _cp(sem, vmem_mb=48):
    return pltpu.CompilerParams(dimension_semantics=sem, vmem_limit_bytes=vmem_mb * 2**20)


def _iota(shape, dim):
    return lax.broadcasted_iota(jnp.int32, shape, dim)


def _sigmoid(x):
    return 0.5 * jnp.tanh(0.5 * x) + 0.5


def _silu(x):
    return x * _sigmoid(x)


def _softplus(x):
    return jnp.maximum(x, 0.0) + jnp.log(1.0 + jnp.exp(-jnp.abs(x)))


def _mm(a, b):
    return jnp.dot(a.astype(BF16), b.astype(BF16), preferred_element_type=F32)


def _mm_nt(a, b):
    return lax.dot_general(a.astype(BF16), b.astype(BF16), (((1,), (1,)), ((), ())), preferred_element_type=F32)


def _mm_tn(a, b):
    return lax.dot_general(a.astype(BF16), b.astype(BF16), (((0,), (0,)), ((), ())), preferred_element_type=F32)


def _mm_f32(a, b):
    return jnp.dot(a, b, precision=HIGHEST, preferred_element_type=F32)


def _mod_body(c_ref, w_ref, b_ref, o_ref):
    o_ref[0] = _mm_f32(_silu(c_ref[...]), w_ref[0]) + b_ref[0]


def _modulation(c16, ada_w, ada_b):
    depth, d, n = ada_w.shape
    tn = 1024
    return pl.pallas_call(
        _mod_body, out_shape=S((depth, MOD_ROWS, n), F32), grid=(depth, n // tn),
        in_specs=[pl.BlockSpec((MOD_ROWS, d), lambda l, j: (0, 0)),
                  pl.BlockSpec((1, d, tn), lambda l, j: (l, 0, j)),
                  pl.BlockSpec((1, 1, tn), lambda l, j: (l, 0, j))],
        out_specs=pl.BlockSpec((1, MOD_ROWS, tn), lambda l, j: (l, 0, j)),
        compiler_params=_cp(("parallel", "parallel")), name="adaln_mod",
    )(c16, ada_w, ada_b.reshape(depth, 1, n))


def _norm_mod(x, nw, shift_ref, scale_ref, b, row0, n_lat, n_b):
    tm = x.shape[0]
    y = x * lax.rsqrt(jnp.mean(x * x, axis=-1, keepdims=True) + EPS) * nw
    is_ctx = (row0 + _iota((tm, 1), 0)) >= n_lat
    shift = jnp.where(is_ctx, shift_ref[n_b:n_b + 1, :], shift_ref[pl.ds(b, 1), :])
    scale = jnp.where(is_ctx, scale_ref[n_b:n_b + 1, :], scale_ref[pl.ds(b, 1), :])
    return y * (1.0 + scale) + shift


def _row_gate(gate_ref, b, row0, tm, n_lat, n_b):
    is_ctx = (row0 + _iota((tm, 1), 0)) >= n_lat
    return jnp.where(is_ctx, gate_ref[n_b:n_b + 1, :], gate_ref[pl.ds(b, 1), :])


def _in_proj_body(x_ref, nw_ref, sh_ref, sc_ref, w_ref, bias_ref, o_ref, h_scr, *, n_lat, n_b, tm):
    b, i = pl.program_id(0), pl.program_id(1)

    @pl.when(pl.program_id(2) == 0)
    def _():
        h_scr[...] = _norm_mod(x_ref[0], nw_ref[...], sh_ref, sc_ref, b, i * tm, n_lat, n_b).astype(BF16)

    o_ref[0] = (jnp.dot(h_scr[...], w_ref[...], preferred_element_type=F32) + bias_ref[...]).astype(o_ref.dtype)


def _in_proj(xs, norm_w, mod, shift_idx, scale_idx, w, bias, n_lat, name):
    bsz, t, d = xs.shape
    n = w.shape[1]
    tm = t
    tn = IN_PROJ_TN
    assert n % tn == 0
    body = functools.partial(_in_proj_body, n_lat=n_lat, n_b=bsz, tm=tm)
    return pl.pallas_call(
        body, out_shape=S((bsz, t, n), BF16), grid=(bsz, t // tm, n // tn),
        in_specs=[pl.BlockSpec((1, tm, d), lambda b, i, j: (b, i, 0)),
                  pl.BlockSpec((1, d), lambda b, i, j: (0, 0)),
                  pl.BlockSpec((MOD_ROWS, d), lambda b, i, j: (0, shift_idx)),
                  pl.BlockSpec((MOD_ROWS, d), lambda b, i, j: (0, scale_idx)),
                  pl.BlockSpec((d, tn), lambda b, i, j: (0, j)),
                  pl.BlockSpec((1, tn), lambda b, i, j: (0, j))],
        out_specs=pl.BlockSpec((1, tm, tn), lambda b, i, j: (b, i, j)),
        scratch_shapes=[pltpu.VMEM((tm, d), BF16)],
        compiler_params=_cp(("parallel", "parallel", "arbitrary")), name=name,
    )(xs, norm_w.reshape(1, d), mod, mod, w, bias.reshape(1, n))


def _out_proj_body(y_ref, w_ref, bias_ref, res_ref, gate_ref, nw_ref, sh_ref, sc_ref, rw_ref, o_ref, h_ref, lg_ref,
                   *, n_lat, n_b, tm):
    b, i = pl.program_id(0), pl.program_id(1)
    y = jnp.dot(y_ref[0].astype(BF16), w_ref[...], preferred_element_type=F32) + bias_ref[...]
    x = res_ref[0] + _row_gate(gate_ref, b, i * tm, tm, n_lat, n_b) * y
    o_ref[0] = x
    h = _norm_mod(x, nw_ref[...], sh_ref, sc_ref, b, i * tm, n_lat, n_b)
    hi = h.astype(BF16)
    lo = (h - hi.astype(F32)).astype(BF16)
    h_ref[0] = hi
    r = jnp.dot(jnp.concatenate([hi, lo], axis=0), rw_ref[...], preferred_element_type=F32)
    lg_ref[0] = (r[0:tm, 0:LANES] + r[0:tm, LANES:]) + (r[tm:, 0:LANES] + r[tm:, LANES:])


def _out_proj(y, w, bias, res, mod, n_lat, norm2_w, router_p, name):
    bsz, t, dy = y.shape
    d = w.shape[1]
    tm = t // 3 if t % 3 == 0 and (t // 3) % BF16_ROWS == 0 else t
    body = functools.partial(_out_proj_body, n_lat=n_lat, n_b=bsz, tm=tm)
    tok = lambda width: pl.BlockSpec((1, tm, width), lambda b, i: (b, i, 0))
    modc = lambda idx: pl.BlockSpec((MOD_ROWS, d), lambda b, i: (0, idx))
    return pl.pallas_call(
        body, out_shape=(S((bsz, t, d), F32), S((bsz, t, d), BF16), S((bsz, t, LANES), F32)), grid=(bsz, t // tm),
        in_specs=[tok(dy), pl.BlockSpec((dy, d), lambda b, i: (0, 0)), pl.BlockSpec((1, d), lambda b, i: (0, 0)),
                  tok(d), modc(2), pl.BlockSpec((1, d), lambda b, i: (0, 0)), modc(3), modc(4),
                  pl.BlockSpec((d, 2 * LANES), lambda b, i: (0, 0))],
        out_specs=(tok(d), tok(d), tok(LANES)),
        compiler_params=_cp(("parallel", "parallel")), name=name,
    )(y, w, bias.reshape(1, d), res, mod, norm2_w.reshape(1, d), mod, mod, router_p)


def _dft_tables(n):
    half = n // 2
    k = np.arange(half, dtype=np.int64)[:, None]
    m = np.arange(half, dtype=np.int64)[None, :]
    ang_e = ((k * 2 * m) % (2 * n)).astype(np.float64) * (math.pi / n)
    ang_o = ((k * (2 * m + 1)) % (2 * n)).astype(np.float64) * (math.pi / n)
    ce, se, co, so = np.cos(ang_e), np.sin(ang_e), np.cos(ang_o), np.sin(ang_o)
    return jnp.asarray(np.stack([ce, se, co, so, co.T, so.T]), dtype=BF16)


def _hy_positional(n):
    t = np.linspace(0.0, 1.0, n)[:, None]
    bands = (HY_EMB - 1) // 2
    w = (2.0 * math.pi * np.arange(n) / n)[:, None]
    f = np.linspace(1e-4, bands - 1, bands)[None, :]
    z = np.concatenate([t, np.cos(f * w), -np.sin(f * w)], axis=-1)
    zp = np.zeros((n, LANES), np.float32)
    zp[:, :HY_EMB] = z
    order = np.concatenate([np.arange(0, n, 2), np.arange(1, n, 2)])
    return jnp.asarray(zp[order]), jnp.asarray(t.astype(np.float32)[order])


def _hy_mlp_body(z_ref, w1_ref, b1_ref, w2_ref, b2_ref, fr_ref, h_ref):
    h = jnp.sin(fr_ref[0:1, :] * (_mm_f32(z_ref[...], w1_ref[...]) + b1_ref[...]))
    h_ref[...] = jnp.sin(fr_ref[1:2, :] * (_mm_f32(h, w2_ref[...]) + b2_ref[...]))


def _alt_sign(rows):
    return (1 - 2 * (_iota((rows, 1), 0) & 1)).astype(F32)


def _hy_filter_body(h_ref, w3f_ref, w3b_ref, t_ref, dl_ref, tab_ref, pk_ref, pm_ref, *, n):
    half = n // 2
    win = jnp.exp(-t_ref[...] * dl_ref[...]) + HY_SHIFT
    hf = _mm_f32(h_ref[...], w3f_ref[...]) * win
    hb = _mm_f32(h_ref[...], w3b_ref[...]) * win
    hb = jnp.where(_iota((n, 1), 0) == 0, 0.0, hb)
    hs, hd = hf + hb, hf - hb
    alt = _alt_sign(half)
    pm_ref[0] = jnp.concatenate([jnp.sum(alt * hs[0:half], axis=0, keepdims=True),
                                 -jnp.sum(alt * hd[half:], axis=0, keepdims=True),
                                 jnp.zeros((6, hs.shape[1]), F32)], axis=0) * (1.0 / n)
    wgt = jnp.where(_iota((half, 1), 0) == 0, 0.5 / n, 1.0 / n)
    ce, co = _mm(tab_ref[0], hs[0:half]), _mm(tab_ref[2], hs[half:])
    se, so = _mm(tab_ref[1], hd[0:half]), _mm(tab_ref[3], hd[half:])
    pk_ref[0, 0] = (ce + co) * wgt
    pk_ref[0, 1] = (ce - co) * wgt
    pk_ref[0, 2] = -(se + so) * wgt
    pk_ref[0, 3] = (se - so) * wgt


def _hy_filter_spectra(n, w1p, b1, w2, b2, freq, w3, tabs, d, td):
    zpad, tcol = _hy_positional(n)
    max_decay = math.log(HY_TARGET) / HY_FAST_DECAY
    min_decay = math.log(HY_TARGET) / HY_SLOW_DECAY
    deltas = jnp.asarray(np.abs(np.linspace(min_decay, max_decay, d)), dtype=F32)[None, :]
    nd = d // td
    ff = w1p.shape[1]
    half = n // 2
    h = pl.pallas_call(_hy_mlp_body, out_shape=S((n, ff), F32), name=f"hy_filter_mlp_{n}")(
        zpad, w1p, b1.reshape(1, ff), w2, b2.reshape(1, ff), freq)
    body = functools.partial(_hy_filter_body, n=n)
    const = lambda o, j: (0, 0)
    return pl.pallas_call(
        body, out_shape=(S((HY_ORDER, 4, half, d), F32), S((HY_ORDER, 8, d), F32)), grid=(HY_ORDER, nd),
        in_specs=[pl.BlockSpec((n, ff), const),
                  pl.BlockSpec((ff, td), lambda o, j: (0, (2 * o) * nd + j)),
                  pl.BlockSpec((ff, td), lambda o, j: (0, (2 * o + 1) * nd + j)),
                  pl.BlockSpec((n, 1), const), pl.BlockSpec((1, td), lambda o, j: (0, j)),
                  pl.BlockSpec((6, half, half), lambda o, j: (0, 0, 0), pipeline_mode=pl.Buffered(1))],
        out_specs=(pl.BlockSpec((1, 4, half, td), lambda o, j: (o, 0, 0, j)),
                   pl.BlockSpec((1, 8, td), lambda o, j: (o, 0, j))),
        compiler_params=_cp(("parallel", "parallel")), name=f"hy_filter_{n}",
    )(h, w3, w3, tcol, deltas, tabs)


def _short_conv(x, w_ref, b_ref):
    n = x.shape[0]
    row = _iota((n, 1), 0)
    xp = jnp.where(row == 0, 0.0, pltpu.roll(x, 1, 0))
    xn = jnp.where(row == n - 1, 0.0, pltpu.roll(x, n - 1, 0))
    return w_ref[0:1, :] * xp + w_ref[1:2, :] * x + w_ref[2:3, :] * xn + b_ref[...]


def _hy_conv_body(a_ref, g_ref, wa_ref, ba_ref, wg_ref, bg_ref, tab_ref, pk_ref, pm_ref, fb_ref, o_ref,
                  a_s, y_s, *, n, conv_a):
    half = n // 2
    nt = a_s.shape[0]
    a = a_ref[0].astype(F32)
    if conv_a:
        a = _short_conv(a, wa_ref, ba_ref)
    for c in range(nt):
        a_s[c] = a[:, c * LANES:(c + 1) * LANES]
    ve = jnp.concatenate([a_s[c, pl.ds(0, half, stride=2), :] for c in range(nt)], axis=1)
    vo = jnp.concatenate([a_s[c, pl.ds(1, half, stride=2), :] for c in range(nt)], axis=1)
    ec, es = _mm(tab_ref[0], ve), _mm(tab_ref[1], ve)
    oc, os_ = _mm(tab_ref[2], vo), _mm(tab_ref[3], vo)
    alt = _alt_sign(half)
    xr_mid = jnp.sum(alt * ve, axis=0, keepdims=True)
    xi_mid = -jnp.sum(alt * vo, axis=0, keepdims=True)
    kr_lo, kr_hi, ki_lo, ki_hi = pk_ref[0, 0], pk_ref[0, 1], pk_ref[0, 2], pk_ref[0, 3]
    xr_lo, xr_hi, xi_lo, xi_hi = ec + oc, ec - oc, -(es + os_), es - os_
    yr_lo = xr_lo * kr_lo - xi_lo * ki_lo
    yi_lo = xr_lo * ki_lo + xi_lo * kr_lo
    yr_hi = xr_hi * kr_hi - xi_hi * ki_hi
    yi_hi = xr_hi * ki_hi + xi_hi * kr_hi
    kr_mid, ki_mid = pm_ref[0, 0:1, :], pm_ref[0, 1:2, :]
    yr_mid = xr_mid * kr_mid - xi_mid * ki_mid
    yi_mid = xr_mid * ki_mid + xi_mid * kr_mid
    y_even = _mm(tab_ref[0], yr_lo + yr_hi) - _mm(tab_ref[1], yi_lo - yi_hi) + alt * yr_mid
    y_odd = _mm(tab_ref[4], yr_lo - yr_hi) - _mm(tab_ref[5], yi_lo + yi_hi) - alt * yi_mid
    for c in range(nt):
        y_s[c, pl.ds(0, half, stride=2), :] = y_even[:, c * LANES:(c + 1) * LANES]
        y_s[c, pl.ds(1, half, stride=2), :] = y_odd[:, c * LANES:(c + 1) * LANES]
    y = jnp.concatenate([y_s[c] for c in range(nt)], axis=1)
    g = _short_conv(g_ref[0].astype(F32), wg_ref, bg_ref)
    o_ref[0] = g * (y + a * fb_ref[0])


def _hy_conv(a, a_col0, a_row, g, g_col0, g_row, n, conv_w, conv_b, tabs, pk, pm, fbias, order, conv_a, d, td, name):
    bsz = a.shape[0]
    nd = d // td
    half = n // 2
    body = functools.partial(_hy_conv_body, n=n, conv_a=conv_a)
    cw = conv_w
    cb = conv_b.reshape(1, -1)
    return pl.pallas_call(
        body, out_shape=S((bsz, n, d), F32), grid=(nd, bsz),
        in_specs=[pl.BlockSpec((1, n, td), lambda j, b: (b, a_row, a_col0 * nd + j)),
                  pl.BlockSpec((1, n, td), lambda j, b: (b, g_row, g_col0 * nd + j)),
                  pl.BlockSpec((3, td), lambda j, b: (0, a_col0 * nd + j if conv_a else j)),
                  pl.BlockSpec((1, td), lambda j, b: (0, a_col0 * nd + j if conv_a else j)),
                  pl.BlockSpec((3, td), lambda j, b: (0, g_col0 * nd + j)),
                  pl.BlockSpec((1, td), lambda j, b: (0, g_col0 * nd + j)),
                  pl.BlockSpec((6, half, half), lambda j, b: (0, 0, 0), pipeline_mode=pl.Buffered(1)),
                  pl.BlockSpec((1, 4, half, td), lambda j, b: (order, 0, 0, j), pipeline_mode=pl.Buffered(1)),
                  pl.BlockSpec((1, 8, td), lambda j, b: (order, 0, j)),
                  pl.BlockSpec((1, 1, td), lambda j, b: (order, 0, j))],
        out_specs=pl.BlockSpec((1, n, td), lambda j, b: (b, 0, j)),
        scratch_shapes=[pltpu.VMEM((td // LANES, n, LANES), F32), pltpu.VMEM((td // LANES, n, LANES), F32)],
        compiler_params=_cp(("parallel", "parallel"), 56), name=name,
    )(a, g, cw, cb, cw, cb, tabs, pk, pm, fbias.reshape(HY_ORDER, 1, d))


def _hyena_mixer(u, n_lat, n_ctx, short_w, short_b, w1, b1, w2, b2, w3, freq, fbias):
    d = u.shape[2] // 3
    ff = w1.shape[1]
    w1p = jnp.zeros((LANES, ff), F32).at[:w1.shape[0]].set(w1)
    outs = []
    for n, row in ((n_lat, 0), (n_ctx, n_lat // n_ctx)):
        td = 256 if n > 512 else 512
        tabs = _dft_tables(n)
        pk, pm = _hy_filter_spectra(n, w1p, b1, w2, b2, freq, w3, tabs, d, td)
        z1 = _hy_conv(u, 0, row, u, 1, row, n, short_w, short_b, tabs, pk, pm, fbias, 0, True, d, td, f"hy_conv1_{n}")
        z2 = _hy_conv(z1, 0, 0, u, 2, row, n, short_w, short_b, tabs, pk, pm, fbias, 1, False, d, td, f"hy_conv2_{n}")
        outs.append(z2)
    return jnp.concatenate(outs, axis=1)


def _rope_tables(n_lat, width, rot_heads):
    hd = SW_HD
    rows = n_lat // GRID_W
    row = np.repeat(np.arange(rows, dtype=np.float64), GRID_W)
    col = np.tile(np.arange(GRID_W, dtype=np.float64), rows)
    nf = hd // 4
    inv = ROPE_BASE ** (-np.arange(nf, dtype=np.float64) / nf)
    ang = np.concatenate([row[:, None] * inv, col[:, None] * inv], axis=-1)
    cos, sin = np.cos(ang), np.sin(ang)
    zero = np.zeros_like(sin)
    c = np.ones((n_lat, width), np.float32)
    sa = np.zeros((n_lat, width), np.float32)
    sb = np.zeros((n_lat, width), np.float32)
    for h in range(rot_heads):
        c[:, h * hd:(h + 1) * hd] = np.concatenate([cos, cos], axis=-1)
        sa[:, h * hd:(h + 1) * hd] = np.concatenate([-sin, zero], axis=-1)
        sb[:, h * hd:(h + 1) * hd] = np.concatenate([zero, sin], axis=-1)
    return jnp.asarray(c), jnp.asarray(sa), jnp.asarray(sb)


def _rope(x, c, sa, sb):
    w = x.shape[1]
    half = SW_HD // 2
    return x * c + pltpu.roll(x, w - half, 1) * sa + pltpu.roll(x, half, 1) * sb


def _sink_attend(q, kvs, sink):
    ss = []
    m = None
    for k, _, mask in kvs:
        s = _mm_nt(q, k)
        if mask is not None:
            s = jnp.where(mask, s, NEG_INF)
        ss.append(s)
        sm = jnp.max(s, axis=-1, keepdims=True)
        m = sm if m is None else jnp.maximum(m, sm)
    m = jnp.maximum(m, sink)
    den = jnp.exp(sink - m)
    o = None
    for s, (_, v, _) in zip(ss, kvs):
        p = jnp.exp(s - m)
        den = den + jnp.sum(p, axis=-1, keepdims=True)
        pv = _mm(p, v)
        o = pv if o is None else o + pv
    return o / den


def _swa_body(u_ref, cq_ref, saq_ref, sbq_ref, ck_ref, sak_ref, sbk_ref, sink_ref, o_ref, kv_scr, s_a, s_b,
              *, n_lat, n_ctx, blk):
    g = pl.program_id(1)
    hd, rep = SW_HD, SW_REP
    qw = rep * hd
    scale = hd ** -0.5
    span = 3 * blk
    nb = n_lat // blk
    kv_scr[0:n_lat, :] = _rope(u_ref[0, 0:n_lat, qw:qw + 2 * hd].astype(F32), ck_ref[...], sak_ref[...],
                               sbk_ref[...]).astype(BF16)
    kv_scr[n_lat:, :] = u_ref[0, n_lat:, qw:qw + 2 * hd]
    kc = kv_scr[n_lat:, 0:hd]
    vc = kv_scr[n_lat:, hd:2 * hd]

    def key_start(i):
        return pl.multiple_of(jnp.clip((i - 1) * blk, 0, n_lat - span), blk)

    def scores(i, s_ref):
        rows = pl.ds(pl.multiple_of(i * blk, blk), blk)
        q = _rope(u_ref[0, rows, 0:qw].astype(F32), cq_ref[rows, :], saq_ref[rows, :], sbq_ref[rows, :]) * scale
        q = q.astype(BF16)
        kl = kv_scr[pl.ds(key_start(i), span), 0:hd]
        for r in range(rep):
            s_ref[r, :, 0:n_ctx] = _mm_nt(q[:, r * hd:(r + 1) * hd], kc)
            s_ref[r, :, n_ctx:] = _mm_nt(q[:, r * hd:(r + 1) * hd], kl)

    def attend(i, s_ref):
        r0 = pl.multiple_of(i * blk, blk)
        ks = key_start(i)
        vl = kv_scr[pl.ds(ks, span), hd:2 * hd]
        mask = jnp.abs((r0 + _iota((blk, span), 0)) - (ks + _iota((blk, span), 1))) <= SW_WINDOW
        for r in range(rep):
            sink = sink_ref[g * rep + r]
            sc = s_ref[r, :, 0:n_ctx]
            sl = jnp.where(mask, s_ref[r, :, n_ctx:], NEG_INF)
            m = jnp.maximum(jnp.maximum(jnp.max(sc, axis=-1, keepdims=True), jnp.max(sl, axis=-1, keepdims=True)), sink)
            pc = jnp.exp(sc - m)
            pl_ = jnp.exp(sl - m)
            den = jnp.exp(sink - m) + jnp.sum(pc, axis=-1, keepdims=True) + jnp.sum(pl_, axis=-1, keepdims=True)
            o_ref[0, pl.ds(r0, blk), r * hd:(r + 1) * hd] = (_mm(pc, vc) + _mm(pl_, vl)) / den

    scores(0, s_a)

    def pair(j, carry):
        scores(2 * j + 1, s_b)
        attend(2 * j, s_a)
        scores(jnp.minimum(2 * j + 2, nb - 1), s_a)
        attend(2 * j + 1, s_b)
        return carry

    lax.fori_loop(0, nb // 2, pair, 0)
    qc = (u_ref[0, n_lat:, 0:qw].astype(F32) * scale).astype(BF16)
    for r in range(rep):
        o_ref[0, n_lat:, r * hd:(r + 1) * hd] = _sink_attend(qc[:, r * hd:(r + 1) * hd], [(kc, vc, None)], sink_ref[g * rep + r])


def _swa_mixer(u, n_lat, n_ctx, sink):
    bsz, t, _ = u.shape
    hd, rep = SW_HD, SW_REP
    gw = rep * hd + 2 * hd
    blk = SW_WINDOW
    assert n_lat % (2 * blk) == 0 and n_lat >= 3 * blk and n_lat % GRID_W == 0
    cq, saq, sbq = _rope_tables(n_lat, rep * hd, rep)
    ck, sak, sbk = _rope_tables(n_lat, 2 * hd, 1)
    body = functools.partial(_swa_body, n_lat=n_lat, n_ctx=n_ctx, blk=blk)
    tab = lambda w: pl.BlockSpec((n_lat, w), lambda b, g: (0, 0))
    return pl.pallas_call(
        body, out_shape=S((bsz, t, SW_HQ * hd), F32), grid=(bsz, SW_HKV),
        in_specs=[pl.BlockSpec((1, t, gw), lambda b, g: (b, 0, g)),
                  tab(rep * hd), tab(rep * hd), tab(rep * hd), tab(2 * hd), tab(2 * hd), tab(2 * hd),
                  pl.BlockSpec(memory_space=pltpu.SMEM)],
        out_specs=pl.BlockSpec((1, t, rep * hd), lambda b, g: (b, 0, g)),
        scratch_shapes=[pltpu.VMEM((t, 2 * hd), BF16), pltpu.VMEM((rep, blk, n_ctx + 3 * blk), F32),
                        pltpu.VMEM((rep, blk, n_ctx + 3 * blk), F32)],
        compiler_params=_cp(("parallel", "parallel")), name="swa_attention",
    )(u, cq, saq, sbq, ck, sak, sbk, sink)


def _seq_conv(x, w_ref, n_lat):
    t = x.shape[0]
    row = _iota((t, 1), 0)
    first = (row == 0) | (row == n_lat)
    last = (row == n_lat - 1) | (row == t - 1)
    xp = jnp.where(first, 0.0, pltpu.roll(x, 1, 0))
    xn = jnp.where(last, 0.0, pltpu.roll(x, t - 1, 0))
    return w_ref[0:1, :] * xp + w_ref[1:2, :] * x + w_ref[2:3, :] * xn


def _chunk_scan(x, reverse):
    t = x.shape[0]
    pos = _iota((t, 1), 0) & (CHUNK - 1)
    s = 1
    while s < CHUNK:
        if reverse:
            x = x + jnp.where(pos < CHUNK - s, pltpu.roll(x, t - s, 0), 0.0)
        else:
            x = x + jnp.where(pos >= s, pltpu.roll(x, s, 0), 0.0)
        s *= 2
    return x


def _chunk_scan_lanes(x, reverse):
    t = x.shape[1]
    pos = _iota((1, t), 1) & (CHUNK - 1)
    s = 1
    while s < CHUNK:
        if reverse:
            x = x + jnp.where(pos < CHUNK - s, pltpu.roll(x, t - s, 1), 0.0)
        else:
            x = x + jnp.where(pos >= s, pltpu.roll(x, s, 1), 0.0)
        s *= 2
    return x


def _chunk_order(s, n_lat_chunks, n_ctx_chunks, direction):
    if direction == 0:
        return jnp.where(s < n_ctx_chunks, n_lat_chunks + s, s - n_ctx_chunks)
    return n_lat_chunks + n_ctx_chunks - 1 - s


def _intra_unroll(n_super, sup=SUPER):
    return next(u for u in range(INTRA_ROWS // sup, 0, -1) if n_super % u == 0)


def _row_sumsq(x):
    return jnp.dot((x * x).astype(BF16), jnp.ones((LANES, LANES), BF16), preferred_element_type=F32)


def _gated_rms(o, nw, z):
    return o * lax.rsqrt(_row_sumsq(o) * (1.0 / LANES) + EPS) * nw * _silu(z)


def _bmm(a, b):
    return jnp.einsum('bij,bjk->bik', a.astype(BF16), b.astype(BF16), preferred_element_type=F32)


def _bmm_nt(a, b):
    return jnp.einsum('bik,bjk->bij', a.astype(BF16), b.astype(BF16), preferred_element_type=F32)


def _unit_tri_inverse(a, ii, jj):
    eye = (ii == jj).astype(F32)
    a8 = jnp.where((ii >> 3) == (jj >> 3), a, 0.0)
    a8_2 = _bmm(a8, a8)
    a8_4 = _bmm(a8_2, a8_2)
    x = _bmm(_bmm(eye - a8, eye + a8_2), eye + a8_4)
    sh = 3
    while (1 << sh) < CHUNK:
        e = jnp.where(((ii >> (sh + 1)) == (jj >> (sh + 1))) & ((ii >> sh) != (jj >> sh)), a, 0.0)
        x = x - _bmm(_bmm(x, e), x)
        sh += 1
    return x


def _gdn_body(q_ref, k_ref, v_ref, z_ref, ba_ref, cwq_ref, cwk_ref, cwv_ref, par_ref, nw_ref, o_ref,
              kn_s, qn_s, qe_s, kk_s, egl_s, kb_s, kbe_s, vb_s, gc_s, gt_s, qp_s, k2_s, n_s, oacc,
              *, n_lat, n_ctx, hp):
    t = n_lat + n_ctx
    for hh in range(hp):
        _gdn_head_local(hh, pl.program_id(1) * hp + hh, q_ref, k_ref, v_ref, ba_ref, cwq_ref, cwk_ref, cwv_ref, par_ref,
                        kn_s, qn_s, qe_s, kk_s, egl_s, kb_s, kbe_s, vb_s, gc_s, gt_s, qp_s, k2_s, n_s, oacc,
                        n_lat=n_lat, t=t)

    nl, nc = n_lat // CHUNK, n_ctx // CHUNK

    def step(s, states):
        new = []
        for hh in range(hp):
            for d in range(2):
                st = states[2 * hh + d]
                sb = st.astype(BF16)
                c = _chunk_order(s, nl, nc, d)
                rows = pl.ds(pl.multiple_of(c * CHUNK, CHUNK), CHUNK)
                crow = pl.ds(pl.multiple_of(c * GD_DK, GD_DK), GD_DK)
                oacc[hh, rows, :] += jnp.dot(qp_s[hh, d, rows, :], sb, preferred_element_type=F32)
                new.append(st * egl_s[hh, d, pl.ds(c, 1), :] + n_s[hh, d, crow, :]
                           - jnp.dot(k2_s[hh, d, crow, :], sb, preferred_element_type=F32))
        return tuple(new)

    zero = jnp.zeros((GD_DK, LANES), F32)
    lax.fori_loop(0, nl + nc, step, (zero,) * (2 * hp))
    for hh in range(hp):
        cs = slice(hh * LANES, (hh + 1) * LANES)
        o_ref[0, :, cs] = _gated_rms(oacc[hh], nw_ref[...], z_ref[0, :, cs].astype(F32))


def _gdn_head_local(hh, h, q_ref, k_ref, v_ref, ba_ref, cwq_ref, cwk_ref, cwv_ref, par_ref,
                    kn_s, qn_s, qe_s, kk_s, egl_s, kb_s, kbe_s, vb_s, gc_s, gt_s, qp_s, k2_s, n_s, oacc, *, n_lat, t):
    cs = slice(hh * LANES, (hh + 1) * LANES)
    lane = _iota((1, LANES), 1)
    q = _silu(_seq_conv(q_ref[0, :, cs].astype(F32), cwq_ref[:, cs], n_lat))
    k = _silu(_seq_conv(k_ref[0, :, cs].astype(F32), cwk_ref[:, cs], n_lat))
    v = _silu(_seq_conv(v_ref[0, :, cs].astype(F32), cwv_ref[:, cs], n_lat))
    qn = q * lax.rsqrt(_row_sumsq(q) + EPS) * (GD_DK ** -0.5)
    kn = k * lax.rsqrt(_row_sumsq(k) + EPS)
    kn_s[...] = kn.astype(BF16)
    qn_s[...] = qn.astype(BF16)
    par = par_ref[...]

    def pick(x, c):
        return jnp.sum(jnp.where(lane == c, x, 0.0), axis=-1, keepdims=True)

    src = _iota((LANES, LANES), 0)
    dst = _iota((LANES, LANES), 1)
    want = jnp.where(dst < 2, 2 * GD_H + dst * GD_H + h, (dst - 2) * GD_H + h)
    onehot = jnp.where((src == want) & (dst < 4), 1.0, 0.0)
    raw_t = jnp.dot(ba_ref[0].astype(F32), onehot, preferred_element_type=F32).T[0:8, :]
    row8 = _iota((8, 1), 0)
    a_log = jnp.where(row8 == 0, pick(par[0:1, :], h), pick(par[1:2, :], h))
    dt_b = jnp.where(row8 == 0, pick(par[2:3, :], h), pick(par[3:4, :], h))
    g_t = -jnp.exp(a_log) * _softplus(raw_t + dt_b)
    beta_t = _sigmoid(raw_t)
    pre = _chunk_scan_lanes(g_t, False)
    suf = _chunk_scan_lanes(g_t, True)
    gcum_t = jnp.where(row8 == 1, suf, pre)
    glast_t = pre + suf - g_t
    gt_s[...] = gcum_t
    up2 = lambda x: pltpu.roll(x, 2, 0)
    packed = jnp.concatenate(
        [jnp.where(row8 < 2, gcum_t, up2(jnp.exp(gcum_t))),
         jnp.where(row8 < 2, jnp.exp(glast_t - gcum_t), jnp.where(row8 < 4, up2(jnp.exp(glast_t)), up2(beta_t))),
         jnp.zeros((LANES - 16, t), F32)], axis=0)
    cols = packed.T
    gc_s[...] = cols
    for d in range(2):
        beta = cols[:, 12 + d:13 + d]
        eg = jnp.broadcast_to(cols[:, 2 + d:3 + d], (t, LANES))
        kb = kn * beta
        qe_s[d] = (qn * eg).astype(BF16)
        kk_s[d] = (kn * cols[:, 8 + d:9 + d]).astype(BF16)
        ends = gc_s[pl.ds(0, t // CHUNK, stride=CHUNK), :]
        egl_s[hh, d] = jnp.broadcast_to(ends[:, 10 + d:11 + d], (t // CHUNK, LANES))
        kb_s[d] = kb.astype(BF16)
        kbe_s[d] = (kb * eg).astype(BF16)
        vb_s[d] = (v * beta).astype(BF16)

    sup = GD_SUPER
    ii = _iota((sup, sup), 0)
    jj = _iota((sup, sup), 1)
    same = (ii >> 6) == (jj >> 6)
    per = sup // CHUNK
    unroll = _intra_unroll(t // sup, sup)

    def intra(it, carry):
        r0s = [pl.multiple_of((it * unroll + kq) * sup, sup) for kq in range(unroll)]
        kcs = [kn_s[pl.ds(r0, sup), :] for r0 in r0s]
        pairs = [(kq, d) for kq in range(unroll) for d in range(2)]
        decs = []
        for kq, d in pairs:
            rows = pl.ds(r0s[kq], sup)
            incl = same & ((jj >= ii) if d else (jj <= ii))
            dif = jnp.broadcast_to(gc_s[rows, d:d + 1], (sup, sup)) - gt_s[d:d + 1, rows]
            decs.append(jnp.where(incl, jnp.exp(jnp.where(incl, dif, 0.0)), 0.0))
        dec = jnp.stack(decs)
        kc2 = jnp.stack([kcs[kq] for kq, _ in pairs])
        kb = jnp.stack([kb_s[d, pl.ds(r0s[kq], sup), :] for kq, d in pairs])
        rhs = jnp.stack([jnp.concatenate([vb_s[d, pl.ds(r0s[kq], sup), :], kbe_s[d, pl.ds(r0s[kq], sup), :]], axis=1)
                         for kq, d in pairs])
        a = jnp.where(ii == jj, 0.0, _bmm_nt(kb, kc2) * dec)
        uw = _bmm(_unit_tri_inverse(a, ii, jj), rhs).astype(BF16)
        qk = _bmm_nt(jnp.stack([qn_s[pl.ds(r0, sup), :] for r0 in r0s]), jnp.stack(kcs))
        auw = _bmm(jnp.stack([qk[kq] for kq, _ in pairs]) * dec, uw)
        for i, (kq, d) in enumerate(pairs):
            rows = pl.ds(r0s[kq], sup)
            qp_s[hh, d, rows, :] = (qe_s[d, rows, :].astype(F32) - auw[i, :, LANES:]).astype(BF16)
            if d == 1:
                oacc[hh, rows, :] = auw[i - 1, :, 0:LANES] + auw[i, :, 0:LANES]
            for c4 in range(per):
                kuw = _mm_tn(kk_s[d, pl.ds(r0s[kq] + c4 * CHUNK, CHUNK), :], uw[i, c4 * CHUNK:(c4 + 1) * CHUNK])
                crow = pl.ds(pl.multiple_of(((it * unroll + kq) * per + c4) * GD_DK, GD_DK), GD_DK)
                n_s[hh, d, crow, :] = kuw[:, 0:LANES]
                k2_s[hh, d, crow, :] = kuw[:, LANES:].astype(BF16)
        return carry

    lax.fori_loop(0, t // sup // unroll, intra, 0)


def _gdn_mixer(u, n_lat, n_ctx, conv_w, a_log, dt_bias, norm_w):
    bsz, t, _ = u.shape
    assert t % SUPER == 0 and n_lat % SUPER == 0
    par = jnp.zeros((8, LANES), F32).at[0:2, :GD_H].set(a_log).at[2:4, :GD_H].set(dt_bias)
    hp = GD_HEADS_PER_STEP
    body = functools.partial(_gdn_body, n_lat=n_lat, n_ctx=n_ctx, hp=hp)
    sec = lambda s: pl.BlockSpec((1, t, hp * LANES), lambda b, h: (b, 0, s * (GD_H // hp) + h))
    cw = lambda s: pl.BlockSpec((3, hp * LANES), lambda b, h: (0, s * (GD_H // hp) + h))
    both = lambda dt: pltpu.VMEM((2, t, LANES), dt)
    state_rows = (t // CHUNK) * GD_DK
    return pl.pallas_call(
        body, out_shape=S((bsz, t, GD_H * LANES), F32), grid=(bsz, GD_H // hp),
        in_specs=[sec(0), sec(1), sec(2), sec(3),
                  pl.BlockSpec((1, t, LANES), lambda b, h: (b, 0, 4 * GD_H)),
                  cw(0), cw(1), cw(2),
                  pl.BlockSpec((8, LANES), lambda b, h: (0, 0)),
                  pl.BlockSpec((1, LANES), lambda b, h: (0, 0))],
        out_specs=pl.BlockSpec((1, t, hp * LANES), lambda b, h: (b, 0, h)),
        scratch_shapes=[pltpu.VMEM((t, LANES), BF16), pltpu.VMEM((t, LANES), BF16),
                        both(BF16), both(BF16), pltpu.VMEM((hp, 2, t // CHUNK, LANES), F32),
                        both(BF16), both(BF16), both(BF16),
                        pltpu.VMEM((t, LANES), F32),
                        pltpu.VMEM((8, t), F32), pltpu.VMEM((hp, 2, t, LANES), BF16),
                        pltpu.VMEM((hp, 2, state_rows, LANES), BF16),
                        pltpu.VMEM((hp, 2, state_rows, LANES), F32), pltpu.VMEM((hp, t, LANES), F32)],
        compiler_params=_cp(("parallel", "parallel"), 56), name="gdn_mixer",
    )(u, u, u, u, u, conv_w, conv_w, conv_w, par, norm_w.reshape(1, LANES))


def _hgrn_body(q_ref, ff_ref, fb_ref, i_ref, g_ref, lbp_ref, nw_ref, o_ref,
               qe_s, ke_s, kk_s, egl_s, v_s, n_s, st_s, oacc, *, n_lat, n_ctx, layer):
    t = n_lat + n_ctx
    e = jnp.exp(lbp_ref[...] - jnp.max(lbp_ref[...], axis=0, keepdims=True))
    lb = jnp.sum(e[1:layer + 1, :], axis=0, keepdims=True) / jnp.sum(e, axis=0, keepdims=True)
    q = _silu(q_ref[0].astype(F32))
    v_s[...] = i_ref[0]
    for d, f_ref in enumerate((ff_ref, fb_ref)):
        f = f_ref[0].astype(F32)
        sig = _sigmoid(f)
        logf = jnp.log(lb + (1.0 - lb) * sig)
        kin = (1.0 - lb) * (1.0 - sig)
        gc = _chunk_scan(logf, bool(d))
        ends = gc.reshape(t // CHUNK, CHUNK, LANES)[:, 0:1, :] if d else gc.reshape(t // CHUNK, CHUNK, LANES)[:, CHUNK - 1:, :]
        glast = jnp.broadcast_to(ends, (t // CHUNK, CHUNK, LANES)).reshape(t, LANES)
        qe_s[d] = (q * jnp.exp(gc)).astype(BF16)
        ke_s[d] = (kin * jnp.exp(-gc)).astype(BF16)
        kk_s[d] = (kin * jnp.exp(glast - gc)).astype(BF16)
        egl_s[d] = jnp.exp(glast)

    ii = _iota((SUPER, SUPER), 0)
    jj = _iota((SUPER, SUPER), 1)
    same = (ii >> 6) == (jj >> 6)
    per = SUPER // CHUNK
    unroll = _intra_unroll(t // SUPER)

    def chunk_rows(sc, c4):
        rows = pl.ds(pl.multiple_of(sc * SUPER + c4 * CHUNK, CHUNK), CHUNK)
        crow = pl.ds(pl.multiple_of((sc * per + c4) * LANES, LANES), LANES)
        return rows, crow

    def intra(it, carry):
        pairs = [(kq, d) for kq in range(unroll) for d in range(2)]
        rows = [pl.ds(pl.multiple_of((it * unroll + kq) * SUPER, SUPER), SUPER) for kq in range(unroll)]
        incl = jnp.stack([same & ((jj >= ii) if d else (jj <= ii)) for _, d in pairs])
        at = jnp.where(incl, _bmm_nt(jnp.stack([qe_s[d, rows[kq], :] for kq, d in pairs]),
                                     jnp.stack([ke_s[d, rows[kq], :] for kq, d in pairs])), 0.0)
        part = _bmm(at, jnp.stack([v_s[rows[kq], :] for kq, _ in pairs]))
        for kq in range(unroll):
            oacc[rows[kq], :] = part[2 * kq] + part[2 * kq + 1]
            for d in range(2):
                for c4 in range(per):
                    r64, crow = chunk_rows(it * unroll + kq, c4)
                    n_s[d, crow, :] = _mm_tn(v_s[r64, :], kk_s[d, r64, :])
        return carry

    lax.fori_loop(0, t // SUPER // unroll, intra, 0)
    nl, nc = n_lat // CHUNK, n_ctx // CHUNK

    def scan(s, states):
        new = []
        for d in range(2):
            c = _chunk_order(s, nl, nc, d)
            crow = pl.ds(pl.multiple_of(c * LANES, LANES), LANES)
            st_s[d, crow, :] = states[d].astype(BF16)
            new.append(states[d] * egl_s[d, pl.ds(c * CHUNK, 1), :] + n_s[d, crow, :])
        return tuple(new)

    zero = jnp.zeros((LANES, HG_DK), F32)
    lax.fori_loop(0, nl + nc, scan, (zero, zero))

    def inter(it, carry):
        for kq in range(unroll):
            sc = it * unroll + kq
            for c4 in range(per):
                r64, crow = chunk_rows(sc, c4)
                oacc[r64, :] += (_mm_nt(qe_s[0, r64, :], st_s[0, crow, :]) + _mm_nt(qe_s[1, r64, :], st_s[1, crow, :]))
        return carry

    lax.fori_loop(0, t // SUPER // unroll, inter, 0)
    o_ref[0] = _gated_rms(oacc[...], nw_ref[...], g_ref[0].astype(F32))


def _hgrn_mixer(u, n_lat, n_ctx, hg_lb, norm_w, layer):
    bsz, t, n5 = u.shape
    d = n5 // 5
    nh = d // HG_DK
    depth = hg_lb.shape[0]
    assert t % SUPER == 0 and n_lat % SUPER == 0
    body = functools.partial(_hgrn_body, n_lat=n_lat, n_ctx=n_ctx, layer=layer)
    sec = lambda s: pl.BlockSpec((1, t, LANES), lambda b, h: (b, 0, s * nh + h))
    both = lambda dt: pltpu.VMEM((2, t, LANES), dt)
    state_rows = (t // CHUNK) * LANES
    return pl.pallas_call(
        body, out_shape=S((bsz, t, d), F32), grid=(bsz, nh),
        in_specs=[sec(0), sec(1), sec(2), sec(3), sec(4),
                  pl.BlockSpec((depth, LANES), lambda b, h: (0, h)),
                  pl.BlockSpec((1, LANES), lambda b, h: (0, 0))],
        out_specs=pl.BlockSpec((1, t, LANES), lambda b, h: (b, 0, h)),
        scratch_shapes=[both(BF16), both(BF16), both(BF16), both(F32), pltpu.VMEM((t, LANES), BF16),
                        pltpu.VMEM((2, state_rows, LANES), F32), pltpu.VMEM((2, state_rows, LANES), BF16),
                        pltpu.VMEM((t, LANES), F32)],
        compiler_params=_cp(("parallel", "parallel")), name="hgrn2_mixer",
    )(u, u, u, u, u, hg_lb, norm_w.reshape(1, LANES))


def _excl_count_lanes(x):
    n = x.shape[1]
    blk = min(n, 256)
    tri = jnp.where(_iota((blk, blk), 0) < _iota((blk, blk), 1), 1.0, 0.0).astype(BF16)
    run = jnp.zeros((x.shape[0], 1), F32)
    outs = []
    for r in range(n // blk):
        xb = x[:, r * blk:(r + 1) * blk]
        outs.append(jnp.dot(xb.astype(BF16), tri, preferred_element_type=F32) + run)
        run = run + jnp.sum(xb, axis=1, keepdims=True)
    return jnp.concatenate(outs, axis=1) if len(outs) > 1 else outs[0]


def _topcap_slots(groups):
    def count_ge(v, thr):
        return jnp.sum(jnp.where(v >= thr, 1.0, 0.0), axis=1, keepdims=True)

    def bisect(_, carry):
        new = []
        for (v, cap), (lo, hi) in zip(groups, carry):
            mid = jnp.sqrt(jnp.maximum(lo, 1e-37)) * jnp.sqrt(hi)
            ok = count_ge(v, mid) >= float(cap)
            new.append((jnp.where(ok, mid, lo), jnp.where(ok, hi, mid)))
        return tuple(new)

    init = tuple((jnp.zeros((v.shape[0], 1), F32), jnp.full((v.shape[0], 1), 2.0, F32)) for v, _ in groups)
    brackets = lax.fori_loop(0, 34, bisect, init)
    codes = []
    for (v, cap), (lo, hi) in zip(groups, brackets):
        thr, found, upper = lo, jnp.zeros_like(lo), hi
        for _ in range(4):
            m = jnp.max(jnp.where(v < upper, v, -1.0), axis=1, keepdims=True)
            ok = jnp.where(count_ge(v, m) >= float(cap), 1.0, 0.0) * (1.0 - found)
            thr = jnp.where(ok > 0, m, thr)
            found = jnp.maximum(found, ok)
            upper = jnp.where(found > 0, upper, m)
        gt = jnp.where(v > thr, 1.0, 0.0)
        eq = jnp.where(v == thr, 1.0, 0.0)
        need = float(cap) - jnp.sum(gt, axis=1, keepdims=True)
        sel = jnp.maximum(gt, eq * jnp.where(_excl_count_lanes(eq) < need, 1.0, 0.0))
        codes.append(jnp.where(sel > 0, _excl_count_lanes(sel), -1.0))
    return codes


def _route_body(lg_ref, aff_ref, code_ref, codet_ref, *, n_lat, cap_l, cap_c):
    t = lg_ref.shape[1]
    lane_ok = _iota((1, LANES), 1) < N_EXPERTS
    lg = jnp.where(lane_ok, lg_ref[0], NEG_INF)
    e = jnp.exp(lg - jnp.max(lg, axis=-1, keepdims=True))
    aff = e / jnp.sum(e, axis=-1, keepdims=True)
    aff_ref[0] = aff
    aff_t = aff.T[0:N_EXPERTS, :]
    code_l, code_c = _topcap_slots([(aff_t[:, 0:n_lat], cap_l), (aff_t[:, n_lat:], cap_c)])
    code_t = jnp.concatenate([code_l, code_c], axis=1)
    codet_ref[0] = code_t
    code_ref[0] = jnp.concatenate([code_t, jnp.full((LANES - N_EXPERTS, t), -1.0, F32)], axis=0).T


def _route(logits, n_lat, cap_l, cap_c):
    bsz, t, _ = logits.shape
    body = functools.partial(_route_body, n_lat=n_lat, cap_l=cap_l, cap_c=cap_c)
    blk = pl.BlockSpec((1, t, LANES), lambda b: (b, 0, 0))
    return pl.pallas_call(
        body, out_shape=(S((bsz, t, LANES), F32), S((bsz, t, LANES), F32), S((bsz, N_EXPERTS, t), F32)), grid=(bsz,),
        in_specs=[blk], out_specs=(blk, blk, pl.BlockSpec((1, N_EXPERTS, t), lambda b: (b, 0, 0))),
        compiler_params=_cp(("parallel",)), name="moe_route",
    )(logits)


def _gather_body(codet_ref, h_ref, xl_ref, xc_ref, *, n_lat, n_ctx, cap_l, cap_c, grp):
    hl = h_ref[0, 0:n_lat, :]
    hc = h_ref[0, n_lat:, :]
    il = _iota((cap_l, n_lat), 0).astype(F32)
    ic = _iota((cap_c, n_ctx), 0).astype(F32)
    for e0 in range(0, N_EXPERTS, grp):
        onehot = jnp.concatenate([jnp.where(il == codet_ref[0, e:e + 1, 0:n_lat], 1.0, 0.0).astype(BF16)
                                  for e in range(e0, e0 + grp)], axis=0)
        xe = jnp.dot(onehot, hl, preferred_element_type=F32).astype(BF16)
        for r in range(grp):
            xl_ref[0, e0 + r] = xe[r * cap_l:(r + 1) * cap_l]
    onehot = jnp.concatenate([jnp.where(ic == codet_ref[0, e:e + 1, n_lat:], 1.0, 0.0).astype(BF16)
                              for e in range(N_EXPERTS)], axis=0)
    xc_ref[0] = jnp.dot(onehot, hc, preferred_element_type=F32).astype(BF16)


def _gather(codet, h, n_lat, cap_l, cap_c):
    bsz, t, d = h.shape
    body = functools.partial(_gather_body, n_lat=n_lat, n_ctx=t - n_lat, cap_l=cap_l, cap_c=cap_c, grp=4)
    return pl.pallas_call(
        body, out_shape=(S((bsz, N_EXPERTS, cap_l, d), BF16), S((bsz, N_EXPERTS * cap_c, d), BF16)), grid=(bsz,),
        in_specs=[pl.BlockSpec((1, N_EXPERTS, t), lambda b: (b, 0, 0)), pl.BlockSpec((1, t, d), lambda b: (b, 0, 0))],
        out_specs=(pl.BlockSpec((1, N_EXPERTS, cap_l, d), lambda b: (b, 0, 0, 0)),
                   pl.BlockSpec((1, N_EXPERTS * cap_c, d), lambda b: (b, 0, 0))),
        compiler_params=_cp(("parallel",)), name="moe_gather",
    )(codet, h)


def _ffn_body(xl_ref, xc_ref, wg_ref, wu_ref, wd_ref, yl_ref, yc_ref, x_scr, acc, *, nb, cap_l, cap_c):
    f = pl.program_id(1)
    d = x_scr.shape[1]

    @pl.when(f == 0)
    def _():
        x_scr[0:nb * cap_l, :] = xl_ref[:, 0].reshape(nb * cap_l, d)
        x_scr[nb * cap_l:, :] = xc_ref[:, 0].reshape(nb * cap_c, d)
        acc[...] = jnp.zeros_like(acc)

    x = x_scr[...]
    a = jnp.dot(x, wg_ref[0, 0].astype(BF16), preferred_element_type=F32)
    u = jnp.dot(x, wu_ref[0, 0].astype(BF16), preferred_element_type=F32)
    acc[...] += _mm(_silu(a) * u, wd_ref[0, 0])

    @pl.when(f == pl.num_programs(1) - 1)
    def _():
        y = acc[...].astype(BF16)
        yl_ref[:, 0] = y[0:nb * cap_l].reshape(nb, cap_l, d)
        yc_ref[:, 0] = y[nb * cap_l:].reshape(nb, cap_c, d)


def _expert_ffn(xl, xc, w_gate, w_up, w_down, layer):
    bsz, ne, cap_l, d = xl.shape
    cap_c = xc.shape[2]
    ffd = w_gate.shape[-1]
    tf = 256
    rows = bsz * (cap_l + cap_c)
    body = functools.partial(_ffn_body, nb=bsz, cap_l=cap_l, cap_c=cap_c)
    return pl.pallas_call(
        body, out_shape=(S(xl.shape, BF16), S(xc.shape, BF16)), grid=(ne, ffd // tf),
        in_specs=[pl.BlockSpec((bsz, 1, cap_l, d), lambda e, f: (0, e, 0, 0)),
                  pl.BlockSpec((bsz, 1, cap_c, d), lambda e, f: (0, e, 0, 0)),
                  pl.BlockSpec((1, 1, d, tf), lambda e, f: (layer, e, 0, f)),
                  pl.BlockSpec((1, 1, d, tf), lambda e, f: (layer, e, 0, f)),
                  pl.BlockSpec((1, 1, tf, d), lambda e, f: (layer, e, f, 0))],
        out_specs=(pl.BlockSpec((bsz, 1, cap_l, d), lambda e, f: (0, e, 0, 0)),
                   pl.BlockSpec((bsz, 1, cap_c, d), lambda e, f: (0, e, 0, 0))),
        scratch_shapes=[pltpu.VMEM((rows, d), BF16), pltpu.VMEM((rows, d), F32)],
        compiler_params=_cp(("parallel", "arbitrary")), name="moe_expert_ffn",
    )(xl, xc, w_gate, w_up, w_down)


def _combine_body(code_ref, aff_ref, yl_ref, yc_ref, res_ref, gate_ref, fw_ref, o_ref,
                  *, n_lat, n_b, cap_l, cap_c, tm, final):
    b, i = pl.program_id(0), pl.program_id(1)
    code = code_ref[0]
    aff = aff_ref[0]

    def scatter(cap, y):
        slot = _iota((tm, cap), 1).astype(F32)
        q = jnp.concatenate([jnp.where(code[:, e:e + 1] == slot, aff[:, e:e + 1], 0.0) for e in range(N_EXPERTS)], axis=1)
        return jnp.dot(q.astype(BF16), y, preferred_element_type=F32)

    @pl.when(i * tm < n_lat)
    def _():
        x = res_ref[0] + gate_ref[pl.ds(b, 1), :] * scatter(cap_l, yl_ref[0])
        if final:
            x = x * lax.rsqrt(jnp.mean(x * x, axis=-1, keepdims=True) + EPS) * fw_ref[...]
        o_ref[0] = x

    @pl.when(i * tm >= n_lat)
    def _():
        o_ref[0] = res_ref[0] + gate_ref[n_b:n_b + 1, :] * scatter(cap_c, yc_ref[0])


def _combine(code, aff, yl, yc, res, mod, n_lat, final_w, final):
    bsz, t, d = res.shape
    cap_l, cap_c = yl.shape[2], yc.shape[2]
    tm = t - n_lat
    assert n_lat % tm == 0
    rows = n_lat if final else t
    body = functools.partial(_combine_body, n_lat=n_lat, n_b=bsz, cap_l=cap_l, cap_c=cap_c, tm=tm, final=final)
    tok = lambda w: pl.BlockSpec((1, tm, w), lambda b, i: (b, i, 0))
    return pl.pallas_call(
        body, out_shape=S((bsz, rows, d), F32), grid=(bsz, rows // tm),
        in_specs=[tok(LANES), tok(LANES),
                  pl.BlockSpec((1, N_EXPERTS * cap_l, d), lambda b, i: (b, 0, 0)),
                  pl.BlockSpec((1, N_EXPERTS * cap_c, d), lambda b, i: (b, 0, 0)),
                  tok(d), pl.BlockSpec((MOD_ROWS, d), lambda b, i: (0, 5)), pl.BlockSpec((1, d), lambda b, i: (0, 0))],
        out_specs=tok(d),
        compiler_params=_cp(("parallel", "parallel")), name="moe_combine",
    )(code, aff, yl.reshape(bsz, N_EXPERTS * cap_l, d), yc.reshape(bsz, N_EXPERTS * cap_c, d), res, mod,
      final_w.reshape(1, d))


def _moe_layer(xs, h, logits, mod, w_gate, w_up, w_down, layer, n_lat, final_w, final):
    bsz, t, d = xs.shape
    n_ctx = t - n_lat
    cap_l = EC_CAPACITY * n_lat // N_EXPERTS
    cap_c = EC_CAPACITY * n_ctx // N_EXPERTS
    aff, code, codet = _route(logits, n_lat, cap_l, cap_c)
    xl, xc = _gather(codet, h, n_lat, cap_l, cap_c)
    yl, yc = _expert_ffn(xl, xc.reshape(bsz, N_EXPERTS, cap_c, d), w_gate, w_up, w_down, layer)
    return _combine(code, aff, yl, yc, xs, mod, n_lat, final_w, final)


def _swa_group_columns():
    hd, rep = SW_HD, SW_REP
    cols = []
    for g in range(SW_HKV):
        cols += list(range(g * rep * hd, (g + 1) * rep * hd))
        cols += list(range(SW_HQ * hd + g * hd, SW_HQ * hd + (g + 1) * hd))
        cols += list(range((SW_HQ + SW_HKV) * hd + g * hd, (SW_HQ + SW_HKV) * hd + (g + 1) * hd))
    return np.asarray(cols, np.int32)


def kernel(x, c, ctx, c_ctx, ada_w, ada_b, norm1_w, norm2_w, final_norm_w, hy_w_in, hy_b_in, hy_short_w, hy_short_b, hy_ffn_w1, hy_ffn_b1, hy_ffn_w2, hy_ffn_b2, hy_ffn_w3, hy_sin_freq, hy_filter_bias, hy_w_out, hy_b_out, sw_w_in, sw_sink, sw_w_out, gd_w_in, gd_conv_w, gd_a_log, gd_dt_bias, gd_norm_w, gd_w_out, hg_w_in, hg_lb, hg_norm_w, hg_w_out, moe_router, moe_w_gate, moe_w_up, moe_w_down):
    bsz, n_lat, d = x.shape
    n_ctx = ctx.shape[1]
    depth = ada_w.shape[0]
    assert bsz < MOD_ROWS and n_lat % n_ctx == 0
    xs = jnp.concatenate([x, ctx], axis=1)
    c16 = jnp.zeros((MOD_ROWS, d), F32).at[:bsz].set(c).at[bsz].set(c_ctx)
    mod = _modulation(c16, ada_w, ada_b)
    zero_bias = jnp.zeros((d,), F32)
    gd_pad = (-gd_w_in.shape[1]) % IN_PROJ_TN
    gd_w = jnp.pad(gd_w_in, ((0, 0), (0, gd_pad))).astype(BF16)
    sw_w = sw_w_in[:, _swa_group_columns()].astype(BF16)
    for layer in range(depth):
        m = mod[layer]
        nw = norm1_w[layer]
        kind = layer % 4
        if kind == 0:
            u = _in_proj(xs, nw, m, 0, 1, hy_w_in.astype(BF16), hy_b_in, n_lat, "hy_in_proj")
            y = _hyena_mixer(u, n_lat, n_ctx, hy_short_w, hy_short_b, hy_ffn_w1, hy_ffn_b1, hy_ffn_w2, hy_ffn_b2,
                             hy_ffn_w3, hy_sin_freq, hy_filter_bias)
            w_out, b_out, name = hy_w_out, hy_b_out, "hy_out_proj"
        elif kind == 1:
            u = _in_proj(xs, nw, m, 0, 1, sw_w, jnp.zeros((sw_w.shape[1],), F32), n_lat, "sw_in_proj")
            y = _swa_mixer(u, n_lat, n_ctx, sw_sink)
            w_out, b_out, name = sw_w_out, zero_bias, "sw_out_proj"
        elif kind == 2:
            u = _in_proj(xs, nw, m, 0, 1, gd_w, jnp.zeros((gd_w.shape[1],), F32), n_lat, "gd_in_proj")
            y = _gdn_mixer(u, n_lat, n_ctx, gd_conv_w, gd_a_log, gd_dt_bias, gd_norm_w)
            w_out, b_out, name = gd_w_out, zero_bias, "gd_out_proj"
        else:
            u = _in_proj(xs, nw, m, 0, 1, hg_w_in.astype(BF16), jnp.zeros((hg_w_in.shape[1],), F32), n_lat, "hg_in_proj")
            y = _hgrn_mixer(u, n_lat, n_ctx, hg_lb, hg_norm_w, layer)
            w_out, b_out, name = hg_w_out, zero_bias, "hg_out_proj"
        router_w = jnp.pad(moe_router[layer], ((0, 0), (0, LANES - N_EXPERTS)))
        router_hi = router_w.astype(BF16)
        router_p = jnp.concatenate([router_hi, (router_w - router_hi.astype(F32)).astype(BF16)], axis=1)
        xs, h, logits = _out_proj(y, w_out.astype(BF16), b_out, xs, m, n_lat, norm2_w[layer], router_p, name)
        xs = _moe_layer(xs, h, logits, m, moe_w_gate, moe_w_up, moe_w_down, layer, n_lat, final_norm_w,
                        layer == depth - 1)
    return xs
```

```python
import functools
import math

import jax
import jax.numpy as jnp
import numpy as np
from jax import lax
from jax.experimental import pallas as pl
from jax.experimental.pallas import tpu as pltpu

F32 = jnp.float32
BF16 = jnp.bfloat16
HIGHEST = lax.Precision.HIGHEST
EPS = 1e-6
NEG_INF = -1e30
LANES = 128
BF16_ROWS = 16
MOD_ROWS = 16

GRID_W = 64
HY_ORDER = 2
HY_EMB = 33
HY_FAST_DECAY = 0.3
HY_SLOW_DECAY = 1.5
HY_TARGET = 1e-2
HY_SHIFT = 0.05
SW_HQ, SW_HKV, SW_HD, SW_WINDOW = 16, 4, 64, 128
SW_REP = SW_HQ // SW_HKV
ROPE_BASE = 10000.0
GD_H, GD_DK = 8, 128
HG_DK = 128
CHUNK = 64
SUPER = 256
GD_SUPER = 128
GD_HEADS_PER_STEP = 2
INTRA_ROWS = 768
IN_PROJ_TN = 512
N_EXPERTS = 16
EC_CAPACITY = 2

S = jax.ShapeDtypeStruct


def _cp(sem, vmem_mb=48):
    return pltpu.CompilerParams(dimension_semantics=sem, vmem_limit_bytes=vmem_mb * 2**20)


def _iota(shape, dim):
    return lax.broadcasted_iota(jnp.int32, shape, dim)


def _sigmoid(x):
    return 0.5 * jnp.tanh(0.5 * x) + 0.5


def _silu(x):
    return x * _sigmoid(x)


def _softplus(x):
    return jnp.maximum(x, 0.0) + jnp.log(1.0 + jnp.exp(-jnp.abs(x)))


def _mm(a, b):
    return jnp.dot(a.astype(BF16), b.astype(BF16), preferred_element_type=F32)


def _mm_nt(a, b):
    return lax.dot_general(a.astype(BF16), b.astype(BF16), (((1,), (1,)), ((), ())), preferred_element_type=F32)


def _mm_tn(a, b):
    return lax.dot_general(a.astype(BF16), b.astype(BF16), (((0,), (0,)), ((), ())), preferred_element_type=F32)


def _mm_f32(a, b):
    return jnp.dot(a, b, precision=HIGHEST, preferred_element_type=F32)


def _mod_body(c_ref, w_ref, b_ref, o_ref):
    o_ref[0] = _mm_f32(_silu(c_ref[...]), w_ref[0]) + b_ref[0]


def _modulation(c16, ada_w, ada_b):
    depth, d, n = ada_w.shape
    tn = 1024
    return pl.pallas_call(
        _mod_body, out_shape=S((depth, MOD_ROWS, n), F32), grid=(depth, n // tn),
        in_specs=[pl.BlockSpec((MOD_ROWS, d), lambda l, j: (0, 0)),
                  pl.BlockSpec((1, d, tn), lambda l, j: (l, 0, j)),
                  pl.BlockSpec((1, 1, tn), lambda l, j: (l, 0, j))],
        out_specs=pl.BlockSpec((1, MOD_ROWS, tn), lambda l, j: (l, 0, j)),
        compiler_params=_cp(("parallel", "parallel")), name="adaln_mod",
    )(c16, ada_w, ada_b.reshape(depth, 1, n))


def _norm_mod(x, nw, shift_ref, scale_ref, b, row0, n_lat, n_b):
    tm = x.shape[0]
    y = x * lax.rsqrt(jnp.mean(x * x, axis=-1, keepdims=True) + EPS) * nw
    is_ctx = (row0 + _iota((tm, 1), 0)) >= n_lat
    shift = jnp.where(is_ctx, shift_ref[n_b:n_b + 1, :], shift_ref[pl.ds(b, 1), :])
    scale = jnp.where(is_ctx, scale_ref[n_b:n_b + 1, :], scale_ref[pl.ds(b, 1), :])
    return y * (1.0 + scale) + shift


def _row_gate(gate_ref, b, row0, tm, n_lat, n_b):
    is_ctx = (row0 + _iota((tm, 1), 0)) >= n_lat
    return jnp.where(is_ctx, gate_ref[n_b:n_b + 1, :], gate_ref[pl.ds(b, 1), :])


def _in_proj_body(x_ref, nw_ref, sh_ref, sc_ref, w_ref, bias_ref, o_ref, h_scr, *, n_lat, n_b, tm):
    b = pl.program_id(0)

    @pl.when(pl.program_id(2) == 0)
    def _():
        for rows, mrow in ((slice(0, n_lat), pl.ds(b, 1)), (slice(n_lat, tm), pl.ds(n_b, 1))):
            x = x_ref[0, rows, :]
            y = x * lax.rsqrt(jnp.mean(x * x, axis=-1, keepdims=True) + EPS)
            h_scr[rows, :] = (y * (nw_ref[...] * (1.0 + sc_ref[mrow, :])) + sh_ref[mrow, :]).astype(BF16)

    o_ref[0] = (jnp.dot(h_scr[...], w_ref[...], preferred_element_type=F32) + bias_ref[...]).astype(o_ref.dtype)


def _in_proj(xs, norm_w, mod, shift_idx, scale_idx, w, bias, n_lat, name):
    bsz, t, d = xs.shape
    n = w.shape[1]
    tm = t
    tn = IN_PROJ_TN
    assert n % tn == 0
    body = functools.partial(_in_proj_body, n_lat=n_lat, n_b=bsz, tm=tm)
    return pl.pallas_call(
        body, out_shape=S((bsz, t, n), BF16), grid=(bsz, t // tm, n // tn),
        in_specs=[pl.BlockSpec((1, tm, d), lambda b, i, j: (b, i, 0)),
                  pl.BlockSpec((1, d), lambda b, i, j: (0, 0)),
                  pl.BlockSpec((MOD_ROWS, d), lambda b, i, j: (0, shift_idx)),
                  pl.BlockSpec((MOD_ROWS, d), lambda b, i, j: (0, scale_idx)),
                  pl.BlockSpec((d, tn), lambda b, i, j: (0, j)),
                  pl.BlockSpec((1, tn), lambda b, i, j: (0, j))],
        out_specs=pl.BlockSpec((1, tm, tn), lambda b, i, j: (b, i, j)),
        scratch_shapes=[pltpu.VMEM((tm, d), BF16)],
        compiler_params=_cp(("parallel", "parallel", "arbitrary")), name=name,
    )(xs, norm_w.reshape(1, d), mod, mod, w, bias.reshape(1, n))


def _out_proj_body(y_ref, w_ref, bias_ref, res_ref, gate_ref, nw_ref, sh_ref, sc_ref, rw_ref, o_ref, h_ref, lg_ref,
                   *, n_lat, n_b, tm):
    b, i = pl.program_id(0), pl.program_id(1)
    y = jnp.dot(y_ref[0].astype(BF16), w_ref[...], preferred_element_type=F32) + bias_ref[...]
    x = res_ref[0] + _row_gate(gate_ref, b, i * tm, tm, n_lat, n_b) * y
    o_ref[0] = x
    h = _norm_mod(x, nw_ref[...], sh_ref, sc_ref, b, i * tm, n_lat, n_b)
    hi = h.astype(BF16)
    lo = (h - hi.astype(F32)).astype(BF16)
    h_ref[0] = hi
    r = jnp.dot(jnp.concatenate([hi, lo], axis=0), rw_ref[...], preferred_element_type=F32)
    lg_ref[0] = (r[0:tm, 0:LANES] + r[0:tm, LANES:]) + (r[tm:, 0:LANES] + r[tm:, LANES:])


def _out_proj(y, w, bias, res, mod, n_lat, norm2_w, router_p, name):
    bsz, t, dy = y.shape
    d = w.shape[1]
    tm = t // 3 if t % 3 == 0 and (t // 3) % BF16_ROWS == 0 else t
    body = functools.partial(_out_proj_body, n_lat=n_lat, n_b=bsz, tm=tm)
    tok = lambda width: pl.BlockSpec((1, tm, width), lambda b, i: (b, i, 0))
    modc = lambda idx: pl.BlockSpec((MOD_ROWS, d), lambda b, i: (0, idx))
    return pl.pallas_call(
        body, out_shape=(S((bsz, t, d), F32), S((bsz, t, d), BF16), S((bsz, t, LANES), F32)), grid=(bsz, t // tm),
        in_specs=[tok(dy), pl.BlockSpec((dy, d), lambda b, i: (0, 0)), pl.BlockSpec((1, d), lambda b, i: (0, 0)),
                  tok(d), modc(2), pl.BlockSpec((1, d), lambda b, i: (0, 0)), modc(3), modc(4),
                  pl.BlockSpec((d, 2 * LANES), lambda b, i: (0, 0))],
        out_specs=(tok(d), tok(d), tok(LANES)),
        compiler_params=_cp(("parallel", "parallel")), name=name,
    )(y, w, bias.reshape(1, d), res, mod, norm2_w.reshape(1, d), mod, mod, router_p)


def _dft_tables(n):
    half = n // 2
    k = np.arange(half, dtype=np.int64)[:, None]
    m = np.arange(half, dtype=np.int64)[None, :]
    ang_e = ((k * 2 * m) % (2 * n)).astype(np.float64) * (math.pi / n)
    ang_o = ((k * (2 * m + 1)) % (2 * n)).astype(np.float64) * (math.pi / n)
    ce, se, co, so = np.cos(ang_e), np.sin(ang_e), np.cos(ang_o), np.sin(ang_o)
    return jnp.asarray(np.stack([ce, se, co, so, co.T, so.T]), dtype=BF16)


def _hy_positional(n):
    t = np.linspace(0.0, 1.0, n)[:, None]
    bands = (HY_EMB - 1) // 2
    w = (2.0 * math.pi * np.arange(n) / n)[:, None]
    f = np.linspace(1e-4, bands - 1, bands)[None, :]
    z = np.concatenate([t, np.cos(f * w), -np.sin(f * w)], axis=-1)
    zp = np.zeros((n, LANES), np.float32)
    zp[:, :HY_EMB] = z
    order = np.concatenate([np.arange(0, n, 2), np.arange(1, n, 2)])
    return jnp.asarray(zp[order]), jnp.asarray(t.astype(np.float32)[order])


def _hy_mlp_body(z_ref, w1_ref, b1_ref, w2_ref, b2_ref, fr_ref, h_ref):
    h = jnp.sin(fr_ref[0:1, :] * (_mm_f32(z_ref[...], w1_ref[...]) + b1_ref[...]))
    h_ref[...] = jnp.sin(fr_ref[1:2, :] * (_mm_f32(h, w2_ref[...]) + b2_ref[...]))


def _alt_sign(rows):
    return (1 - 2 * (_iota((rows, 1), 0) & 1)).astype(F32)


def _hy_filter_body(h_ref, w3f_ref, w3b_ref, t_ref, dl_ref, tab_ref, pk_ref, pm_ref, *, n):
    half = n // 2
    win = jnp.exp(-t_ref[...] * dl_ref[...]) + HY_SHIFT
    hf = _mm_f32(h_ref[...], w3f_ref[...]) * win
    hb = _mm_f32(h_ref[...], w3b_ref[...]) * win
    hb = jnp.where(_iota((n, 1), 0) == 0, 0.0, hb)
    hs, hd = hf + hb, hf - hb
    alt = _alt_sign(half)
    pm_ref[0] = jnp.concatenate([jnp.sum(alt * hs[0:half], axis=0, keepdims=True),
                                 -jnp.sum(alt * hd[half:], axis=0, keepdims=True),
                                 jnp.zeros((6, hs.shape[1]), F32)], axis=0) * (1.0 / n)
    wgt = jnp.where(_iota((half, 1), 0) == 0, 0.5 / n, 1.0 / n)
    ce, co = _mm(tab_ref[0], hs[0:half]), _mm(tab_ref[2], hs[half:])
    se, so = _mm(tab_ref[1], hd[0:half]), _mm(tab_ref[3], hd[half:])
    pk_ref[0, 0] = (ce + co) * wgt
    pk_ref[0, 1] = (ce - co) * wgt
    pk_ref[0, 2] = -(se + so) * wgt
    pk_ref[0, 3] = (se - so) * wgt


def _hy_filter_spectra(n, w1p, b1, w2, b2, freq, w3, tabs, d, td):
    zpad, tcol = _hy_positional(n)
    max_decay = math.log(HY_TARGET) / HY_FAST_DECAY
    min_decay = math.log(HY_TARGET) / HY_SLOW_DECAY
    deltas = jnp.asarray(np.abs(np.linspace(min_decay, max_decay, d)), dtype=F32)[None, :]
    nd = d // td
    ff = w1p.shape[1]
    half = n // 2
    h = pl.pallas_call(_hy_mlp_body, out_shape=S((n, ff), F32), name=f"hy_filter_mlp_{n}")(
        zpad, w1p, b1.reshape(1, ff), w2, b2.reshape(1, ff), freq)
    body = functools.partial(_hy_filter_body, n=n)
    const = lambda o, j: (0, 0)
    return pl.pallas_call(
        body, out_shape=(S((HY_ORDER, 4, half, d), F32), S((HY_ORDER, 8, d), F32)), grid=(HY_ORDER, nd),
        in_specs=[pl.BlockSpec((n, ff), const),
                  pl.BlockSpec((ff, td), lambda o, j: (0, (2 * o) * nd + j)),
                  pl.BlockSpec((ff, td), lambda o, j: (0, (2 * o + 1) * nd + j)),
                  pl.BlockSpec((n, 1), const), pl.BlockSpec((1, td), lambda o, j: (0, j)),
                  pl.BlockSpec((6, half, half), lambda o, j: (0, 0, 0), pipeline_mode=pl.Buffered(1))],
        out_specs=(pl.BlockSpec((1, 4, half, td), lambda o, j: (o, 0, 0, j)),
                   pl.BlockSpec((1, 8, td), lambda o, j: (o, 0, j))),
        compiler_params=_cp(("parallel", "parallel")), name=f"hy_filter_{n}",
    )(h, w3, w3, tcol, deltas, tabs)


def _short_conv(x, w_ref, b_ref):
    n = x.shape[0]
    row = _iota((n, 1), 0)
    xp = jnp.where(row == 0, 0.0, pltpu.roll(x, 1, 0))
    xn = jnp.where(row == n - 1, 0.0, pltpu.roll(x, n - 1, 0))
    return w_ref[0:1, :] * xp + w_ref[1:2, :] * x + w_ref[2:3, :] * xn + b_ref[...]


def _hy_conv_body(a_ref, g_ref, wa_ref, ba_ref, wg_ref, bg_ref, tab_ref, pk_ref, pm_ref, fb_ref, o_ref,
                  a_s, y_s, *, n, conv_a):
    half = n // 2
    nt = a_s.shape[0]
    a = a_ref[0].astype(F32)
    if conv_a:
        a = _short_conv(a, wa_ref, ba_ref)
    for c in range(nt):
        a_s[c] = a[:, c * LANES:(c + 1) * LANES]
    ve = jnp.concatenate([a_s[c, pl.ds(0, half, stride=2), :] for c in range(nt)], axis=1)
    vo = jnp.concatenate([a_s[c, pl.ds(1, half, stride=2), :] for c in range(nt)], axis=1)
    ec, es = _mm(tab_ref[0], ve), _mm(tab_ref[1], ve)
    oc, os_ = _mm(tab_ref[2], vo), _mm(tab_ref[3], vo)
    alt = _alt_sign(half)
    xr_mid = jnp.sum(alt * ve, axis=0, keepdims=True)
    xi_mid = -jnp.sum(alt * vo, axis=0, keepdims=True)
    kr_lo, kr_hi, ki_lo, ki_hi = pk_ref[0, 0], pk_ref[0, 1], pk_ref[0, 2], pk_ref[0, 3]
    xr_lo, xr_hi, xi_lo, xi_hi = ec + oc, ec - oc, -(es + os_), es - os_
    yr_lo = xr_lo * kr_lo - xi_lo * ki_lo
    yi_lo = xr_lo * ki_lo + xi_lo * kr_lo
    yr_hi = xr_hi * kr_hi - xi_hi * ki_hi
    yi_hi = xr_hi * ki_hi + xi_hi * kr_hi
    kr_mid, ki_mid = pm_ref[0, 0:1, :], pm_ref[0, 1:2, :]
    yr_mid = xr_mid * kr_mid - xi_mid * ki_mid
    yi_mid = xr_mid * ki_mid + xi_mid * kr_mid
    y_even = _mm(tab_ref[0], yr_lo + yr_hi) - _mm(tab_ref[1], yi_lo - yi_hi) + alt * yr_mid
    y_odd = _mm(tab_ref[4], yr_lo - yr_hi) - _mm(tab_ref[5], yi_lo + yi_hi) - alt * yi_mid
    for c in range(nt):
        y_s[c, pl.ds(0, half, stride=2), :] = y_even[:, c * LANES:(c + 1) * LANES]
        y_s[c, pl.ds(1, half, stride=2), :] = y_odd[:, c * LANES:(c + 1) * LANES]
    y = jnp.concatenate([y_s[c] for c in range(nt)], axis=1)
    g = _short_conv(g_ref[0].astype(F32), wg_ref, bg_ref)
    o_ref[0] = g * (y + a * fb_ref[0])


def _hy_conv(a, a_col0, a_row, g, g_col0, g_row, n, conv_w, conv_b, tabs, pk, pm, fbias, order, conv_a, d, td, name):
    bsz = a.shape[0]
    nd = d // td
    half = n // 2
    body = functools.partial(_hy_conv_body, n=n, conv_a=conv_a)
    cw = conv_w
    cb = conv_b.reshape(1, -1)
    return pl.pallas_call(
        body, out_shape=S((bsz, n, d), F32), grid=(nd, bsz),
        in_specs=[pl.BlockSpec((1, n, td), lambda j, b: (b, a_row, a_col0 * nd + j)),
                  pl.BlockSpec((1, n, td), lambda j, b: (b, g_row, g_col0 * nd + j)),
                  pl.BlockSpec((3, td), lambda j, b: (0, a_col0 * nd + j if conv_a else j)),
                  pl.BlockSpec((1, td), lambda j, b: (0, a_col0 * nd + j if conv_a else j)),
                  pl.BlockSpec((3, td), lambda j, b: (0, g_col0 * nd + j)),
                  pl.BlockSpec((1, td), lambda j, b: (0, g_col0 * nd + j)),
                  pl.BlockSpec((6, half, half), lambda j, b: (0, 0, 0), pipeline_mode=pl.Buffered(1)),
                  pl.BlockSpec((1, 4, half, td), lambda j, b: (order, 0, 0, j), pipeline_mode=pl.Buffered(1)),
                  pl.BlockSpec((1, 8, td), lambda j, b: (order, 0, j)),
                  pl.BlockSpec((1, 1, td), lambda j, b: (order, 0, j))],
        out_specs=pl.BlockSpec((1, n, td), lambda j, b: (b, 0, j)),
        scratch_shapes=[pltpu.VMEM((td // LANES, n, LANES), F32), pltpu.VMEM((td // LANES, n, LANES), F32)],
        compiler_params=_cp(("parallel", "parallel"), 56), name=name,
    )(a, g, cw, cb, cw, cb, tabs, pk, pm, fbias.reshape(HY_ORDER, 1, d))


def _hyena_mixer(u, n_lat, n_ctx, short_w, short_b, w1, b1, w2, b2, w3, freq, fbias):
    d = u.shape[2] // 3
    ff = w1.shape[1]
    w1p = jnp.zeros((LANES, ff), F32).at[:w1.shape[0]].set(w1)
    outs = []
    for n, row in ((n_lat, 0), (n_ctx, n_lat // n_ctx)):
        td = 256 if n > 512 else 512
        tabs = _dft_tables(n)
        pk, pm = _hy_filter_spectra(n, w1p, b1, w2, b2, freq, w3, tabs, d, td)
        z1 = _hy_conv(u, 0, row, u, 1, row, n, short_w, short_b, tabs, pk, pm, fbias, 0, True, d, td, f"hy_conv1_{n}")
        z2 = _hy_conv(z1, 0, 0, u, 2, row, n, short_w, short_b, tabs, pk, pm, fbias, 1, False, d, td, f"hy_conv2_{n}")
        outs.append(z2)
    return jnp.concatenate(outs, axis=1)


def _rope_tables(n_lat, width, rot_heads):
    hd = SW_HD
    rows = n_lat // GRID_W
    row = np.repeat(np.arange(rows, dtype=np.float64), GRID_W)
    col = np.tile(np.arange(GRID_W, dtype=np.float64), rows)
    nf = hd // 4
    inv = ROPE_BASE ** (-np.arange(nf, dtype=np.float64) / nf)
    ang = np.concatenate([row[:, None] * inv, col[:, None] * inv], axis=-1)
    cos, sin = np.cos(ang), np.sin(ang)
    zero = np.zeros_like(sin)
    c = np.ones((n_lat, width), np.float32)
    sa = np.zeros((n_lat, width), np.float32)
    sb = np.zeros((n_lat, width), np.float32)
    for h in range(rot_heads):
        c[:, h * hd:(h + 1) * hd] = np.concatenate([cos, cos], axis=-1)
        sa[:, h * hd:(h + 1) * hd] = np.concatenate([-sin, zero], axis=-1)
        sb[:, h * hd:(h + 1) * hd] = np.concatenate([zero, sin], axis=-1)
    return jnp.asarray(c), jnp.asarray(sa), jnp.asarray(sb)


def _rope(x, c, sa, sb):
    w = x.shape[1]
    half = SW_HD // 2
    return x * c + pltpu.roll(x, w - half, 1) * sa + pltpu.roll(x, half, 1) * sb


def _sink_attend(q, kvs, sink):
    ss = []
    m = None
    for k, _, mask in kvs:
        s = _mm_nt(q, k)
        if mask is not None:
            s = jnp.where(mask, s, NEG_INF)
        ss.append(s)
        sm = jnp.max(s, axis=-1, keepdims=True)
        m = sm if m is None else jnp.maximum(m, sm)
    m = jnp.maximum(m, sink)
    den = jnp.exp(sink - m)
    o = None
    for s, (_, v, _) in zip(ss, kvs):
        p = jnp.exp(s - m)
        den = den + jnp.sum(p, axis=-1, keepdims=True)
        pv = _mm(p, v)
        o = pv if o is None else o + pv
    return o / den


def _swa_body(u_ref, cq_ref, saq_ref, sbq_ref, ck_ref, sak_ref, sbk_ref, sink_ref, o_ref, kv_scr, s_a, s_b,
              *, n_lat, n_ctx, blk):
    g = pl.program_id(1)
    hd, rep = SW_HD, SW_REP
    qw = rep * hd
    scale = hd ** -0.5
    span = 3 * blk
    nb = n_lat // blk
    kv_scr[0:n_lat, :] = _rope(u_ref[0, 0:n_lat, qw:qw + 2 * hd].astype(F32), ck_ref[...], sak_ref[...],
                               sbk_ref[...]).astype(BF16)
    kv_scr[n_lat:, :] = u_ref[0, n_lat:, qw:qw + 2 * hd]
    kc = kv_scr[n_lat:, 0:hd]
    vc = kv_scr[n_lat:, hd:2 * hd]

    def key_start(i):
        return pl.multiple_of(jnp.clip((i - 1) * blk, 0, n_lat - span), blk)

    def scores(i, s_ref):
        rows = pl.ds(pl.multiple_of(i * blk, blk), blk)
        q = _rope(u_ref[0, rows, 0:qw].astype(F32), cq_ref[rows, :], saq_ref[rows, :], sbq_ref[rows, :]) * scale
        q = q.astype(BF16)
        kl = kv_scr[pl.ds(key_start(i), span), 0:hd]
        for r in range(rep):
            s_ref[r, :, 0:n_ctx] = _mm_nt(q[:, r * hd:(r + 1) * hd], kc)
            s_ref[r, :, n_ctx:] = _mm_nt(q[:, r * hd:(r + 1) * hd], kl)

    def attend(i, s_ref):
        r0 = pl.multiple_of(i * blk, blk)
        ks = key_start(i)
        vl = kv_scr[pl.ds(ks, span), hd:2 * hd]
        mask = jnp.abs((r0 + _iota((blk, span), 0)) - (ks + _iota((blk, span), 1))) <= SW_WINDOW
        for r in range(rep):
            sink = sink_ref[g * rep + r]
            sc = s_ref[r, :, 0:n_ctx]
            sl = jnp.where(mask, s_ref[r, :, n_ctx:], NEG_INF)
            m = jnp.maximum(jnp.maximum(jnp.max(sc, axis=-1, keepdims=True), jnp.max(sl, axis=-1, keepdims=True)), sink)
            pc = jnp.exp(sc - m)
            pl_ = jnp.exp(sl - m)
            den = jnp.exp(sink - m) + jnp.sum(pc, axis=-1, keepdims=True) + jnp.sum(pl_, axis=-1, keepdims=True)
            o_ref[0, pl.ds(r0, blk), r * hd:(r + 1) * hd] = (_mm(pc, vc) + _mm(pl_, vl)) / den

    scores(0, s_a)

    def pair(j, carry):
        scores(2 * j + 1, s_b)
        attend(2 * j, s_a)
        scores(jnp.minimum(2 * j + 2, nb - 1), s_a)
        attend(2 * j + 1, s_b)
        return carry

    lax.fori_loop(0, nb // 2, pair, 0)
    qc = (u_ref[0, n_lat:, 0:qw].astype(F32) * scale).astype(BF16)
    for r in range(rep):
        o_ref[0, n_lat:, r * hd:(r + 1) * hd] = _sink_attend(qc[:, r * hd:(r + 1) * hd], [(kc, vc, None)], sink_ref[g * rep + r])


def _swa_mixer(u, n_lat, n_ctx, sink):
    bsz, t, _ = u.shape
    hd, rep = SW_HD, SW_REP
    gw = rep * hd + 2 * hd
    blk = SW_WINDOW
    assert n_lat % (2 * blk) == 0 and n_lat >= 3 * blk and n_lat % GRID_W == 0
    cq, saq, sbq = _rope_tables(n_lat, rep * hd, rep)
    ck, sak, sbk = _rope_tables(n_lat, 2 * hd, 1)
    body = functools.partial(_swa_body, n_lat=n_lat, n_ctx=n_ctx, blk=blk)
    tab = lambda w: pl.BlockSpec((n_lat, w), lambda b, g: (0, 0))
    return pl.pallas_call(
        body, out_shape=S((bsz, t, SW_HQ * hd), F32), grid=(bsz, SW_HKV),
        in_specs=[pl.BlockSpec((1, t, gw), lambda b, g: (b, 0, g)),
                  tab(rep * hd), tab(rep * hd), tab(rep * hd), tab(2 * hd), tab(2 * hd), tab(2 * hd),
                  pl.BlockSpec(memory_space=pltpu.SMEM)],
        out_specs=pl.BlockSpec((1, t, rep * hd), lambda b, g: (b, 0, g)),
        scratch_shapes=[pltpu.VMEM((t, 2 * hd), BF16), pltpu.VMEM((rep, blk, n_ctx + 3 * blk), F32),
                        pltpu.VMEM((rep, blk, n_ctx + 3 * blk), F32)],
        compiler_params=_cp(("parallel", "parallel")), name="swa_attention",
    )(u, cq, saq, sbq, ck, sak, sbk, sink)


def _seq_conv(x, w_ref, n_lat):
    t = x.shape[0]
    row = _iota((t, 1), 0)
    first = (row == 0) | (row == n_lat)
    last = (row == n_lat - 1) | (row == t - 1)
    xp = jnp.where(first, 0.0, pltpu.roll(x, 1, 0))
    xn = jnp.where(last, 0.0, pltpu.roll(x, t - 1, 0))
    return w_ref[0:1, :] * xp + w_ref[1:2, :] * x + w_ref[2:3, :] * xn


def _chunk_scan(x, reverse):
    t = x.shape[0]
    pos = _iota((t, 1), 0) & (CHUNK - 1)
    s = 1
    while s < CHUNK:
        if reverse:
            x = x + jnp.where(pos < CHUNK - s, pltpu.roll(x, t - s, 0), 0.0)
        else:
            x = x + jnp.where(pos >= s, pltpu.roll(x, s, 0), 0.0)
        s *= 2
    return x


def _chunk_scan_lanes(x, reverse):
    t = x.shape[1]
    pos = _iota((1, t), 1) & (CHUNK - 1)
    s = 1
    while s < CHUNK:
        if reverse:
            x = x + jnp.where(pos < CHUNK - s, pltpu.roll(x, t - s, 1), 0.0)
        else:
            x = x + jnp.where(pos >= s, pltpu.roll(x, s, 1), 0.0)
        s *= 2
    return x


def _chunk_order(s, n_lat_chunks, n_ctx_chunks, direction):
    if direction == 0:
        return jnp.where(s < n_ctx_chunks, n_lat_chunks + s, s - n_ctx_chunks)
    return n_lat_chunks + n_ctx_chunks - 1 - s


def _intra_unroll(n_super, sup=SUPER):
    return next(u for u in range(INTRA_ROWS // sup, 0, -1) if n_super % u == 0)


def _row_sumsq(x):
    return jnp.dot((x * x).astype(BF16), jnp.ones((LANES, LANES), BF16), preferred_element_type=F32)


def _gated_rms(o, nw, z):
    return o * lax.rsqrt(_row_sumsq(o) * (1.0 / LANES) + EPS) * nw * _silu(z)


def _bmm(a, b):
    return jnp.einsum('bij,bjk->bik', a.astype(BF16), b.astype(BF16), preferred_element_type=F32)


def _bmm_nt(a, b):
    return jnp.einsum('bik,bjk->bij', a.astype(BF16), b.astype(BF16), preferred_element_type=F32)


def _unit_tri_inverse(a, ii, jj):
    eye = (ii == jj).astype(F32)
    a8 = jnp.where((ii >> 3) == (jj >> 3), a, 0.0)
    a8_2 = _bmm(a8, a8)
    a8_4 = _bmm(a8_2, a8_2)
    x = _bmm(_bmm(eye - a8, eye + a8_2), eye + a8_4)
    sh = 3
    while (1 << sh) < CHUNK:
        e = jnp.where(((ii >> (sh + 1)) == (jj >> (sh + 1))) & ((ii >> sh) != (jj >> sh)), a, 0.0)
        x = x - _bmm(_bmm(x, e), x)
        sh += 1
    return x


def _gdn_body(q_ref, k_ref, v_ref, z_ref, ba_ref, cwq_ref, cwk_ref, cwv_ref, par_ref, nw_ref, o_ref,
              kn_s, qn_s, qe_s, kk_s, egl_s, kb_s, kbe_s, vb_s, gc_s, gt_s, qp_s, k2_s, n_s, oacc,
              *, n_lat, n_ctx, hp):
    t = n_lat + n_ctx
    for hh in range(hp):
        _gdn_head_local(hh, pl.program_id(1) * hp + hh, q_ref, k_ref, v_ref, ba_ref, cwq_ref, cwk_ref, cwv_ref, par_ref,
                        kn_s, qn_s, qe_s, kk_s, egl_s, kb_s, kbe_s, vb_s, gc_s, gt_s, qp_s, k2_s, n_s, oacc,
                        n_lat=n_lat, t=t)

    nl, nc = n_lat // CHUNK, n_ctx // CHUNK

    def step(s, states):
        new = []
        for hh in range(hp):
            for d in range(2):
                st = states[2 * hh + d]
                sb = st.astype(BF16)
                c = _chunk_order(s, nl, nc, d)
                rows = pl.ds(pl.multiple_of(c * CHUNK, CHUNK), CHUNK)
                crow = pl.ds(pl.multiple_of(c * GD_DK, GD_DK), GD_DK)
                oacc[hh, rows, :] += jnp.dot(qp_s[hh, d, rows, :], sb, preferred_element_type=F32)
                new.append(st * egl_s[hh, d, pl.ds(c, 1), :] + n_s[hh, d, crow, :]
                           - jnp.dot(k2_s[hh, d, crow, :], sb, preferred_element_type=F32))
        return tuple(new)

    zero = jnp.zeros((GD_DK, LANES), F32)
    lax.fori_loop(0, nl + nc, step, (zero,) * (2 * hp))
    for hh in range(hp):
        cs = slice(hh * LANES, (hh + 1) * LANES)
        o_ref[0, :, cs] = _gated_rms(oacc[hh], nw_ref[...], z_ref[0, :, cs].astype(F32))


def _gdn_head_local(hh, h, q_ref, k_ref, v_ref, ba_ref, cwq_ref, cwk_ref, cwv_ref, par_ref,
                    kn_s, qn_s, qe_s, kk_s, egl_s, kb_s, kbe_s, vb_s, gc_s, gt_s, qp_s, k2_s, n_s, oacc, *, n_lat, t):
    cs = slice(hh * LANES, (hh + 1) * LANES)
    lane = _iota((1, LANES), 1)
    q = _silu(_seq_conv(q_ref[0, :, cs].astype(F32), cwq_ref[:, cs], n_lat))
    k = _silu(_seq_conv(k_ref[0, :, cs].astype(F32), cwk_ref[:, cs], n_lat))
    v = _silu(_seq_conv(v_ref[0, :, cs].astype(F32), cwv_ref[:, cs], n_lat))
    qn = q * lax.rsqrt(_row_sumsq(q) + EPS) * (GD_DK ** -0.5)
    kn = k * lax.rsqrt(_row_sumsq(k) + EPS)
    kn_s[...] = kn.astype(BF16)
    qn_s[...] = qn.astype(BF16)
    par = par_ref[...]

    def pick(x, c):
        return jnp.sum(jnp.where(lane == c, x, 0.0), axis=-1, keepdims=True)

    src = _iota((LANES, LANES), 0)
    dst = _iota((LANES, LANES), 1)
    want = jnp.where(dst < 2, 2 * GD_H + dst * GD_H + h, (dst - 2) * GD_H + h)
    onehot = jnp.where((src == want) & (dst < 4), 1.0, 0.0)
    raw_t = jnp.dot(ba_ref[0].astype(F32), onehot, preferred_element_type=F32).T[0:8, :]
    row8 = _iota((8, 1), 0)
    a_log = jnp.where(row8 == 0, pick(par[0:1, :], h), pick(par[1:2, :], h))
    dt_b = jnp.where(row8 == 0, pick(par[2:3, :], h), pick(par[3:4, :], h))
    g_t = -jnp.exp(a_log) * _softplus(raw_t + dt_b)
    beta_t = _sigmoid(raw_t)
    pre = _chunk_scan_lanes(g_t, False)
    suf = _chunk_scan_lanes(g_t, True)
    gcum_t = jnp.where(row8 == 1, suf, pre)
    glast_t = pre + suf - g_t
    gt_s[...] = gcum_t
    up2 = lambda x: pltpu.roll(x, 2, 0)
    packed = jnp.concatenate(
        [jnp.where(row8 < 2, gcum_t, up2(jnp.exp(gcum_t))),
         jnp.where(row8 < 2, jnp.exp(glast_t - gcum_t), jnp.where(row8 < 4, up2(jnp.exp(glast_t)), up2(beta_t))),
         jnp.zeros((LANES - 16, t), F32)], axis=0)
    cols = packed.T
    gc_s[...] = cols
    for d in range(2):
        beta = cols[:, 12 + d:13 + d]
        eg = jnp.broadcast_to(cols[:, 2 + d:3 + d], (t, LANES))
        kb = kn * beta
        qe_s[d] = (qn * eg).astype(BF16)
        kk_s[d] = (kn * cols[:, 8 + d:9 + d]).astype(BF16)
        ends = gc_s[pl.ds(0, t // CHUNK, stride=CHUNK), :]
        egl_s[hh, d] = jnp.broadcast_to(ends[:, 10 + d:11 + d], (t // CHUNK, LANES))
        kb_s[d] = kb.astype(BF16)
        kbe_s[d] = (kb * eg).astype(BF16)
        vb_s[d] = (v * beta).astype(BF16)

    sup = GD_SUPER
    ii = _iota((sup, sup), 0)
    jj = _iota((sup, sup), 1)
    same = (ii >> 6) == (jj >> 6)
    per = sup // CHUNK
    unroll = _intra_unroll(t // sup, sup)

    def intra(it, carry):
        r0s = [pl.multiple_of((it * unroll + kq) * sup, sup) for kq in range(unroll)]
        kcs = [kn_s[pl.ds(r0, sup), :] for r0 in r0s]
        pairs = [(kq, d) for kq in range(unroll) for d in range(2)]
        decs = []
        for kq, d in pairs:
            rows = pl.ds(r0s[kq], sup)
            incl = same & ((jj >= ii) if d else (jj <= ii))
            dif = jnp.broadcast_to(gc_s[rows, d:d + 1], (sup, sup)) - gt_s[d:d + 1, rows]
            decs.append(jnp.where(incl, jnp.exp(jnp.where(incl, dif, 0.0)), 0.0))
        dec = jnp.stack(decs)
        kc2 = jnp.stack([kcs[kq] for kq, _ in pairs])
        kb = jnp.stack([kb_s[d, pl.ds(r0s[kq], sup), :] for kq, d in pairs])
        rhs = jnp.stack([jnp.concatenate([vb_s[d, pl.ds(r0s[kq], sup), :], kbe_s[d, pl.ds(r0s[kq], sup), :]], axis=1)
                         for kq, d in pairs])
        a = jnp.where(ii == jj, 0.0, _bmm_nt(kb, kc2) * dec)
        uw = _bmm(_unit_tri_inverse(a, ii, jj), rhs).astype(BF16)
        qk = _bmm_nt(jnp.stack([qn_s[pl.ds(r0, sup), :] for r0 in r0s]), jnp.stack(kcs))
        auw = _bmm(jnp.stack([qk[kq] for kq, _ in pairs]) * dec, uw)
        for i, (kq, d) in enumerate(pairs):
            rows = pl.ds(r0s[kq], sup)
            qp_s[hh, d, rows, :] = (qe_s[d, rows, :].astype(F32) - auw[i, :, LANES:]).astype(BF16)
            if d == 1:
                oacc[hh, rows, :] = auw[i - 1, :, 0:LANES] + auw[i, :, 0:LANES]
            for c4 in range(per):
                kuw = _mm_tn(kk_s[d, pl.ds(r0s[kq] + c4 * CHUNK, CHUNK), :], uw[i, c4 * CHUNK:(c4 + 1) * CHUNK])
                crow = pl.ds(pl.multiple_of(((it * unroll + kq) * per + c4) * GD_DK, GD_DK), GD_DK)
                n_s[hh, d, crow, :] = kuw[:, 0:LANES]
                k2_s[hh, d, crow, :] = kuw[:, LANES:].astype(BF16)
        return carry

    lax.fori_loop(0, t // sup // unroll, intra, 0)


def _gdn_mixer(u, n_lat, n_ctx, conv_w, a_log, dt_bias, norm_w):
    bsz, t, _ = u.shape
    assert t % SUPER == 0 and n_lat % SUPER == 0
    par = jnp.zeros((8, LANES), F32).at[0:2, :GD_H].set(a_log).at[2:4, :GD_H].set(dt_bias)
    hp = GD_HEADS_PER_STEP
    body = functools.partial(_gdn_body, n_lat=n_lat, n_ctx=n_ctx, hp=hp)
    sec = lambda s: pl.BlockSpec((1, t, hp * LANES), lambda b, h: (b, 0, s * (GD_H // hp) + h))
    cw = lambda s: pl.BlockSpec((3, hp * LANES), lambda b, h: (0, s * (GD_H // hp) + h))
    both = lambda dt: pltpu.VMEM((2, t, LANES), dt)
    state_rows = (t // CHUNK) * GD_DK
    return pl.pallas_call(
        body, out_shape=S((bsz, t, GD_H * LANES), F32), grid=(bsz, GD_H // hp),
        in_specs=[sec(0), sec(1), sec(2), sec(3),
                  pl.BlockSpec((1, t, LANES), lambda b, h: (b, 0, 4 * GD_H)),
                  cw(0), cw(1), cw(2),
                  pl.BlockSpec((8, LANES), lambda b, h: (0, 0)),
                  pl.BlockSpec((1, LANES), lambda b, h: (0, 0))],
        out_specs=pl.BlockSpec((1, t, hp * LANES), lambda b, h: (b, 0, h)),
        scratch_shapes=[pltpu.VMEM((t, LANES), BF16), pltpu.VMEM((t, LANES), BF16),
                        both(BF16), both(BF16), pltpu.VMEM((hp, 2, t // CHUNK, LANES), F32),
                        both(BF16), both(BF16), both(BF16),
                        pltpu.VMEM((t, LANES), F32),
                        pltpu.VMEM((8, t), F32), pltpu.VMEM((hp, 2, t, LANES), BF16),
                        pltpu.VMEM((hp, 2, state_rows, LANES), BF16),
                        pltpu.VMEM((hp, 2, state_rows, LANES), F32), pltpu.VMEM((hp, t, LANES), F32)],
        compiler_params=_cp(("parallel", "parallel"), 56), name="gdn_mixer",
    )(u, u, u, u, u, conv_w, conv_w, conv_w, par, norm_w.reshape(1, LANES))


def _hgrn_body(q_ref, ff_ref, fb_ref, i_ref, g_ref, lbp_ref, nw_ref, o_ref,
               qe_s, ke_s, kk_s, egl_s, v_s, n_s, st_s, oacc, *, n_lat, n_ctx, layer):
    t = n_lat + n_ctx
    e = jnp.exp(lbp_ref[...] - jnp.max(lbp_ref[...], axis=0, keepdims=True))
    lb = jnp.sum(e[1:layer + 1, :], axis=0, keepdims=True) / jnp.sum(e, axis=0, keepdims=True)
    q = _silu(q_ref[0].astype(F32))
    v_s[...] = i_ref[0]
    for d, f_ref in enumerate((ff_ref, fb_ref)):
        f = f_ref[0].astype(F32)
        sig = _sigmoid(f)
        logf = jnp.log(lb + (1.0 - lb) * sig)
        kin = (1.0 - lb) * (1.0 - sig)
        gc = _chunk_scan(logf, bool(d))
        ends = gc.reshape(t // CHUNK, CHUNK, LANES)[:, 0:1, :] if d else gc.reshape(t // CHUNK, CHUNK, LANES)[:, CHUNK - 1:, :]
        glast = jnp.broadcast_to(ends, (t // CHUNK, CHUNK, LANES)).reshape(t, LANES)
        egl = jnp.exp(glast)
        ke = kin * jnp.exp(-gc)
        qe_s[d] = (q * jnp.exp(gc)).astype(BF16)
        ke_s[d] = ke.astype(BF16)
        kk_s[d] = (ke * egl).astype(BF16)
        egl_s[d] = egl

    ii = _iota((SUPER, SUPER), 0)
    jj = _iota((SUPER, SUPER), 1)
    same = (ii >> 6) == (jj >> 6)
    per = SUPER // CHUNK
    unroll = _intra_unroll(t // SUPER)

    def chunk_rows(sc, c4):
        rows = pl.ds(pl.multiple_of(sc * SUPER + c4 * CHUNK, CHUNK), CHUNK)
        crow = pl.ds(pl.multiple_of((sc * per + c4) * LANES, LANES), LANES)
        return rows, crow

    def intra(it, carry):
        pairs = [(kq, d) for kq in range(unroll) for d in range(2)]
        rows = [pl.ds(pl.multiple_of((it * unroll + kq) * SUPER, SUPER), SUPER) for kq in range(unroll)]
        incl = jnp.stack([same & ((jj >= ii) if d else (jj <= ii)) for _, d in pairs])
        at = jnp.where(incl, _bmm_nt(jnp.stack([qe_s[d, rows[kq], :] for kq, d in pairs]),
                                     jnp.stack([ke_s[d, rows[kq], :] for kq, d in pairs])), 0.0)
        part = _bmm(at, jnp.stack([v_s[rows[kq], :] for kq, _ in pairs]))
        for kq in range(unroll):
            oacc[rows[kq], :] = part[2 * kq] + part[2 * kq + 1]
            for d in range(2):
                for c4 in range(per):
                    r64, crow = chunk_rows(it * unroll + kq, c4)
                    n_s[d, crow, :] = _mm_tn(v_s[r64, :], kk_s[d, r64, :])
        return carry

    lax.fori_loop(0, t // SUPER // unroll, intra, 0)
    nl, nc = n_lat // CHUNK, n_ctx // CHUNK

    def scan(s, states):
        new = []
        for d in range(2):
            c = _chunk_order(s, nl, nc, d)
            crow = pl.ds(pl.multiple_of(c * LANES, LANES), LANES)
            st_s[d, crow, :] = states[d].astype(BF16)
            new.append(states[d] * egl_s[d, pl.ds(c * CHUNK, 1), :] + n_s[d, crow, :])
        return tuple(new)

    zero = jnp.zeros((LANES, HG_DK), F32)
    lax.fori_loop(0, nl + nc, scan, (zero, zero))

    def inter(it, carry):
        for kq in range(unroll):
            sc = it * unroll + kq
            for c4 in range(per):
                r64, crow = chunk_rows(sc, c4)
                oacc[r64, :] += (_mm_nt(qe_s[0, r64, :], st_s[0, crow, :]) + _mm_nt(qe_s[1, r64, :], st_s[1, crow, :]))
        return carry

    lax.fori_loop(0, t // SUPER // unroll, inter, 0)
    o_ref[0] = _gated_rms(oacc[...], nw_ref[...], g_ref[0].astype(F32))


def _hgrn_mixer(u, n_lat, n_ctx, hg_lb, norm_w, layer):
    bsz, t, n5 = u.shape
    d = n5 // 5
    nh = d // HG_DK
    depth = hg_lb.shape[0]
    assert t % SUPER == 0 and n_lat % SUPER == 0
    body = functools.partial(_hgrn_body, n_lat=n_lat, n_ctx=n_ctx, layer=layer)
    sec = lambda s: pl.BlockSpec((1, t, LANES), lambda b, h: (b, 0, s * nh + h))
    both = lambda dt: pltpu.VMEM((2, t, LANES), dt)
    state_rows = (t // CHUNK) * LANES
    return pl.pallas_call(
        body, out_shape=S((bsz, t, d), F32), grid=(bsz, nh),
        in_specs=[sec(0), sec(1), sec(2), sec(3), sec(4),
                  pl.BlockSpec((depth, LANES), lambda b, h: (0, h)),
                  pl.BlockSpec((1, LANES), lambda b, h: (0, 0))],
        out_specs=pl.BlockSpec((1, t, LANES), lambda b, h: (b, 0, h)),
        scratch_shapes=[both(BF16), both(BF16), both(BF16), both(F32), pltpu.VMEM((t, LANES), BF16),
                        pltpu.VMEM((2, state_rows, LANES), F32), pltpu.VMEM((2, state_rows, LANES), BF16),
                        pltpu.VMEM((t, LANES), F32)],
        compiler_params=_cp(("parallel", "parallel")), name="hgrn2_mixer",
    )(u, u, u, u, u, hg_lb, norm_w.reshape(1, LANES))


def _excl_count_lanes(x):
    n = x.shape[1]
    blk = min(n, 256)
    tri = jnp.where(_iota((blk, blk), 0) < _iota((blk, blk), 1), 1.0, 0.0).astype(BF16)
    run = jnp.zeros((x.shape[0], 1), F32)
    outs = []
    for r in range(n // blk):
        xb = x[:, r * blk:(r + 1) * blk]
        outs.append(jnp.dot(xb.astype(BF16), tri, preferred_element_type=F32) + run)
        run = run + jnp.sum(xb, axis=1, keepdims=True)
    return jnp.concatenate(outs, axis=1) if len(outs) > 1 else outs[0]


def _topcap_slots(groups):
    def count_ge(v, thr):
        return jnp.sum(jnp.where(v >= thr, 1.0, 0.0), axis=1, keepdims=True)

    def bisect(_, carry):
        new = []
        for (v, cap), (lo, hi) in zip(groups, carry):
            mid = jnp.sqrt(jnp.maximum(lo, 1e-37)) * jnp.sqrt(hi)
            ok = count_ge(v, mid) >= float(cap)
            new.append((jnp.where(ok, mid, lo), jnp.where(ok, hi, mid)))
        return tuple(new)

    init = tuple((jnp.zeros((v.shape[0], 1), F32), jnp.full((v.shape[0], 1), 2.0, F32)) for v, _ in groups)
    brackets = lax.fori_loop(0, 34, bisect, init)
    codes = []
    for (v, cap), (lo, hi) in zip(groups, brackets):
        thr, found, upper = lo, jnp.zeros_like(lo), hi
        for _ in range(4):
            m = jnp.max(jnp.where(v < upper, v, -1.0), axis=1, keepdims=True)
            ok = jnp.where(count_ge(v, m) >= float(cap), 1.0, 0.0) * (1.0 - found)
            thr = jnp.where(ok > 0, m, thr)
            found = jnp.maximum(found, ok)
            upper = jnp.where(found > 0, upper, m)
        gt = jnp.where(v > thr, 1.0, 0.0)
        eq = jnp.where(v == thr, 1.0, 0.0)
        need = float(cap) - jnp.sum(gt, axis=1, keepdims=True)
        sel = jnp.maximum(gt, eq * jnp.where(_excl_count_lanes(eq) < need, 1.0, 0.0))
        codes.append(jnp.where(sel > 0, _excl_count_lanes(sel), -1.0))
    return codes


def _route_body(lg_ref, aff_ref, code_ref, codet_ref, *, n_lat, cap_l, cap_c):
    t = lg_ref.shape[1]
    lane_ok = _iota((1, LANES), 1) < N_EXPERTS
    lg = jnp.where(lane_ok, lg_ref[0], NEG_INF)
    e = jnp.exp(lg - jnp.max(lg, axis=-1, keepdims=True))
    aff = e / jnp.sum(e, axis=-1, keepdims=True)
    aff_ref[0] = aff
    aff_t = aff.T[0:N_EXPERTS, :]
    code_l, code_c = _topcap_slots([(aff_t[:, 0:n_lat], cap_l), (aff_t[:, n_lat:], cap_c)])
    code_t = jnp.concatenate([code_l, code_c], axis=1)
    codet_ref[0] = code_t
    code_ref[0] = jnp.concatenate([code_t, jnp.full((LANES - N_EXPERTS, t), -1.0, F32)], axis=0).T


def _route(logits, n_lat, cap_l, cap_c):
    bsz, t, _ = logits.shape
    body = functools.partial(_route_body, n_lat=n_lat, cap_l=cap_l, cap_c=cap_c)
    blk = pl.BlockSpec((1, t, LANES), lambda b: (b, 0, 0))
    return pl.pallas_call(
        body, out_shape=(S((bsz, t, LANES), F32), S((bsz, t, LANES), F32), S((bsz, N_EXPERTS, t), F32)), grid=(bsz,),
        in_specs=[blk], out_specs=(blk, blk, pl.BlockSpec((1, N_EXPERTS, t), lambda b: (b, 0, 0))),
        compiler_params=_cp(("parallel",)), name="moe_route",
    )(logits)


def _gather_body(codet_ref, h_ref, xl_ref, xc_ref, *, n_lat, n_ctx, cap_l, cap_c, grp):
    hl = h_ref[0, 0:n_lat, :]
    hc = h_ref[0, n_lat:, :]
    il = _iota((cap_l, n_lat), 0).astype(F32)
    ic = _iota((cap_c, n_ctx), 0).astype(F32)
    for e0 in range(0, N_EXPERTS, grp):
        onehot = jnp.concatenate([jnp.where(il == codet_ref[0, e:e + 1, 0:n_lat], 1.0, 0.0).astype(BF16)
                                  for e in range(e0, e0 + grp)], axis=0)
        xe = jnp.dot(onehot, hl, preferred_element_type=F32).astype(BF16)
        for r in range(grp):
            xl_ref[0, e0 + r] = xe[r * cap_l:(r + 1) * cap_l]
    onehot = jnp.concatenate([jnp.where(ic == codet_ref[0, e:e + 1, n_lat:], 1.0, 0.0).astype(BF16)
                              for e in range(N_EXPERTS)], axis=0)
    xc_ref[0] = jnp.dot(onehot, hc, preferred_element_type=F32).astype(BF16)


def _gather(codet, h, n_lat, cap_l, cap_c):
    bsz, t, d = h.shape
    body = functools.partial(_gather_body, n_lat=n_lat, n_ctx=t - n_lat, cap_l=cap_l, cap_c=cap_c, grp=4)
    return pl.pallas_call(
        body, out_shape=(S((bsz, N_EXPERTS, cap_l, d), BF16), S((bsz, N_EXPERTS * cap_c, d), BF16)), grid=(bsz,),
        in_specs=[pl.BlockSpec((1, N_EXPERTS, t), lambda b: (b, 0, 0)), pl.BlockSpec((1, t, d), lambda b: (b, 0, 0))],
        out_specs=(pl.BlockSpec((1, N_EXPERTS, cap_l, d), lambda b: (b, 0, 0, 0)),
                   pl.BlockSpec((1, N_EXPERTS * cap_c, d), lambda b: (b, 0, 0))),
        compiler_params=_cp(("parallel",)), name="moe_gather",
    )(codet, h)


def _ffn_body(xl_ref, xc_ref, wg_ref, wu_ref, wd_ref, yl_ref, yc_ref, x_scr, acc, *, nb, cap_l, cap_c):
    f = pl.program_id(1)
    d = x_scr.shape[1]

    @pl.when(f == 0)
    def _():
        x_scr[0:nb * cap_l, :] = xl_ref[:, 0].reshape(nb * cap_l, d)
        x_scr[nb * cap_l:, :] = xc_ref[:, 0].reshape(nb * cap_c, d)
        acc[...] = jnp.zeros_like(acc)

    x = x_scr[...]
    a = jnp.dot(x, wg_ref[0, 0].astype(BF16), preferred_element_type=F32)
    u = jnp.dot(x, wu_ref[0, 0].astype(BF16), preferred_element_type=F32)
    acc[...] += _mm(_silu(a) * u, wd_ref[0, 0])

    @pl.when(f == pl.num_programs(1) - 1)
    def _():
        y = acc[...].astype(BF16)
        yl_ref[:, 0] = y[0:nb * cap_l].reshape(nb, cap_l, d)
        yc_ref[:, 0] = y[nb * cap_l:].reshape(nb, cap_c, d)


def _expert_ffn(xl, xc, w_gate, w_up, w_down, layer):
    bsz, ne, cap_l, d = xl.shape
    cap_c = xc.shape[2]
    ffd = w_gate.shape[-1]
    tf = 256
    rows = bsz * (cap_l + cap_c)
    body = functools.partial(_ffn_body, nb=bsz, cap_l=cap_l, cap_c=cap_c)
    return pl.pallas_call(
        body, out_shape=(S(xl.shape, BF16), S(xc.shape, BF16)), grid=(ne, ffd // tf),
        in_specs=[pl.BlockSpec((bsz, 1, cap_l, d), lambda e, f: (0, e, 0, 0)),
                  pl.BlockSpec((bsz, 1, cap_c, d), lambda e, f: (0, e, 0, 0)),
                  pl.BlockSpec((1, 1, d, tf), lambda e, f: (layer, e, 0, f)),
                  pl.BlockSpec((1, 1, d, tf), lambda e, f: (layer, e, 0, f)),
                  pl.BlockSpec((1, 1, tf, d), lambda e, f: (layer, e, f, 0))],
        out_specs=(pl.BlockSpec((bsz, 1, cap_l, d), lambda e, f: (0, e, 0, 0)),
                   pl.BlockSpec((bsz, 1, cap_c, d), lambda e, f: (0, e, 0, 0))),
        scratch_shapes=[pltpu.VMEM((rows, d), BF16), pltpu.VMEM((rows, d), F32)],
        compiler_params=_cp(("parallel", "arbitrary")), name="moe_expert_ffn",
    )(xl, xc, w_gate, w_up, w_down)


def _combine_body(code_ref, aff_ref, yl_ref, yc_ref, res_ref, gate_ref, fw_ref, o_ref,
                  *, n_lat, n_b, cap_l, cap_c, tm, final):
    b, i = pl.program_id(0), pl.program_id(1)
    code = code_ref[0]
    aff = aff_ref[0]

    def scatter(cap, y):
        slot = _iota((tm, cap), 1).astype(F32)
        q = jnp.concatenate([jnp.where(code[:, e:e + 1] == slot, aff[:, e:e + 1], 0.0) for e in range(N_EXPERTS)], axis=1)
        return jnp.dot(q.astype(BF16), y, preferred_element_type=F32)

    @pl.when(i * tm < n_lat)
    def _():
        x = res_ref[0] + gate_ref[pl.ds(b, 1), :] * scatter(cap_l, yl_ref[0])
        if final:
            x = x * lax.rsqrt(jnp.mean(x * x, axis=-1, keepdims=True) + EPS) * fw_ref[...]
        o_ref[0] = x

    @pl.when(i * tm >= n_lat)
    def _():
        o_ref[0] = res_ref[0] + gate_ref[n_b:n_b + 1, :] * scatter(cap_c, yc_ref[0])


def _combine(code, aff, yl, yc, res, mod, n_lat, final_w, final):
    bsz, t, d = res.shape
    cap_l, cap_c = yl.shape[2], yc.shape[2]
    tm = t - n_lat
    assert n_lat % tm == 0
    rows = n_lat if final else t
    body = functools.partial(_combine_body, n_lat=n_lat, n_b=bsz, cap_l=cap_l, cap_c=cap_c, tm=tm, final=final)
    tok = lambda w: pl.BlockSpec((1, tm, w), lambda b, i: (b, i, 0))
    return pl.pallas_call(
        body, out_shape=S((bsz, rows, d), F32), grid=(bsz, rows // tm),
        in_specs=[tok(LANES), tok(LANES),
                  pl.BlockSpec((1, N_EXPERTS * cap_l, d), lambda b, i: (b, 0, 0)),
                  pl.BlockSpec((1, N_EXPERTS * cap_c, d), lambda b, i: (b, 0, 0)),
                  tok(d), pl.BlockSpec((MOD_ROWS, d), lambda b, i: (0, 5)), pl.BlockSpec((1, d), lambda b, i: (0, 0))],
        out_specs=tok(d),
        compiler_params=_cp(("parallel", "parallel")), name="moe_combine",
    )(code, aff, yl.reshape(bsz, N_EXPERTS * cap_l, d), yc.reshape(bsz, N_EXPERTS * cap_c, d), res, mod,
      final_w.reshape(1, d))


def _moe_layer(xs, h, logits, mod, w_gate, w_up, w_down, layer, n_lat, final_w, final):
    bsz, t, d = xs.shape
    n_ctx = t - n_lat
    cap_l = EC_CAPACITY * n_lat // N_EXPERTS
    cap_c = EC_CAPACITY * n_ctx // N_EXPERTS
    aff, code, codet = _route(logits, n_lat, cap_l, cap_c)
    xl, xc = _gather(codet, h, n_lat, cap_l, cap_c)
    yl, yc = _expert_ffn(xl, xc.reshape(bsz, N_EXPERTS, cap_c, d), w_gate, w_up, w_down, layer)
    return _combine(code, aff, yl, yc, xs, mod, n_lat, final_w, final)


def _swa_group_columns():
    hd, rep = SW_HD, SW_REP
    cols = []
    for g in range(SW_HKV):
        cols += list(range(g * rep * hd, (g + 1) * rep * hd))
        cols += list(range(SW_HQ * hd + g * hd, SW_HQ * hd + (g + 1) * hd))
        cols += list(range((SW_HQ + SW_HKV) * hd + g * hd, (SW_HQ + SW_HKV) * hd + (g + 1) * hd))
    return np.asarray(cols, np.int32)


def kernel(x, c, ctx, c_ctx, ada_w, ada_b, norm1_w, norm2_w, final_norm_w, hy_w_in, hy_b_in, hy_short_w, hy_short_b, hy_ffn_w1, hy_ffn_b1, hy_ffn_w2, hy_ffn_b2, hy_ffn_w3, hy_sin_freq, hy_filter_bias, hy_w_out, hy_b_out, sw_w_in, sw_sink, sw_w_out, gd_w_in, gd_conv_w, gd_a_log, gd_dt_bias, gd_norm_w, gd_w_out, hg_w_in, hg_lb, hg_norm_w, hg_w_out, moe_router, moe_w_gate, moe_w_up, moe_w_down):
    bsz, n_lat, d = x.shape
    n_ctx = ctx.shape[1]
    depth = ada_w.shape[0]
    assert bsz < MOD_ROWS and n_lat % n_ctx == 0
    xs = jnp.concatenate([x, ctx], axis=1)
    c16 = jnp.zeros((MOD_ROWS, d), F32).at[:bsz].set(c).at[bsz].set(c_ctx)
    mod = _modulation(c16, ada_w, ada_b)
    zero_bias = jnp.zeros((d,), F32)
    gd_pad = (-gd_w_in.shape[1]) % IN_PROJ_TN
    gd_w = jnp.pad(gd_w_in, ((0, 0), (0, gd_pad))).astype(BF16)
    sw_w = sw_w_in[:, _swa_group_columns()].astype(BF16)
    for layer in range(depth):
        m = mod[layer]
        nw = norm1_w[layer]
        kind = layer % 4
        if kind == 0:
            u = _in_proj(xs, nw, m, 0, 1, hy_w_in.astype(BF16), hy_b_in, n_lat, "hy_in_proj")
            y = _hyena_mixer(u, n_lat, n_ctx, hy_short_w, hy_short_b, hy_ffn_w1, hy_ffn_b1, hy_ffn_w2, hy_ffn_b2,
                             hy_ffn_w3, hy_sin_freq, hy_filter_bias)
            w_out, b_out, name = hy_w_out, hy_b_out, "hy_out_proj"
        elif kind == 1:
            u = _in_proj(xs, nw, m, 0, 1, sw_w, jnp.zeros((sw_w.shape[1],), F32), n_lat, "sw_in_proj")
            y = _swa_mixer(u, n_lat, n_ctx, sw_sink)
            w_out, b_out, name = sw_w_out, zero_bias, "sw_out_proj"
        elif kind == 2:
            u = _in_proj(xs, nw, m, 0, 1, gd_w, jnp.zeros((gd_w.shape[1],), F32), n_lat, "gd_in_proj")
            y = _gdn_mixer(u, n_lat, n_ctx, gd_conv_w, gd_a_log, gd_dt_bias, gd_norm_w)
            w_out, b_out, name = gd_w_out, zero_bias, "gd_out_proj"
        else:
            u = _in_proj(xs, nw, m, 0, 1, hg_w_in.astype(BF16), jnp.zeros((hg_w_in.shape[1],), F32), n_lat, "hg_in_proj")
            y = _hgrn_mixer(u, n_lat, n_ctx, hg_lb, hg_norm_w, layer)
            w_out, b_out, name = hg_w_out, zero_bias, "hg_out_proj"
        router_w = jnp.pad(moe_router[layer], ((0, 0), (0, LANES - N_EXPERTS)))
        router_hi = router_w.astype(BF16)
        router_p = jnp.concatenate([router_hi, (router_w - router_hi.astype(F32)).astype(BF16)], axis=1)
        xs, h, logits = _out_proj(y, w_out.astype(BF16), b_out, xs, m, n_lat, norm2_w[layer], router_p, name)
        xs = _moe_layer(xs, h, logits, m, moe_w_gate, moe_w_up, moe_w_down, layer, n_lat, final_norm_w,
                        layer == depth - 1)
    return xs
```

```python
import functools
import math

import jax
import jax.numpy as jnp
import numpy as np
from jax import lax
from jax.experimental import pallas as pl
from jax.experimental.pallas import tpu as pltpu

F32 = jnp.float32
BF16 = jnp.bfloat16
HIGHEST = lax.Precision.HIGHEST
EPS = 1e-6
NEG_INF = -1e30
LANES = 128
BF16_ROWS = 16
MOD_ROWS = 16

GRID_W = 64
HY_ORDER = 2
HY_EMB = 33
HY_FAST_DECAY = 0.3
HY_SLOW_DECAY = 1.5
HY_TARGET = 1e-2
HY_SHIFT = 0.05
SW_HQ, SW_HKV, SW_HD, SW_WINDOW = 16, 4, 64, 128
SW_REP = SW_HQ // SW_HKV
ROPE_BASE = 10000.0
GD_H, GD_DK = 8, 128
HG_DK = 128
CHUNK = 64
SUPER = 256
GD_SUPER = 128
GD_HEADS_PER_STEP = 2
INTRA_ROWS = 1152
IN_PROJ_TN = 512
N_EXPERTS = 16
EC_CAPACITY = 2

S = jax.ShapeDtypeStruct


def _cp(sem, vmem_mb=48):
    return pltpu.CompilerParams(dimension_semantics=sem, vmem_limit_bytes=vmem_mb * 2**20)


def _iota(shape, dim):
    return lax.broadcasted_iota(jnp.int32, shape, dim)


def _sigmoid(x):
    return 0.5 * jnp.tanh(0.5 * x) + 0.5


def _silu(x):
    return x * _sigmoid(x)


def _softplus(x):
    return jnp.maximum(x, 0.0) + jnp.log(1.0 + jnp.exp(-jnp.abs(x)))


def _mm(a, b):
    return jnp.dot(a.astype(BF16), b.astype(BF16), preferred_element_type=F32)


def _mm_nt(a, b):
    return lax.dot_general(a.astype(BF16), b.astype(BF16), (((1,), (1,)), ((), ())), preferred_element_type=F32)


def _mm_tn(a, b):
    return lax.dot_general(a.astype(BF16), b.astype(BF16), (((0,), (0,)), ((), ())), preferred_element_type=F32)


def _mm_f32(a, b):
    return jnp.dot(a, b, precision=HIGHEST, preferred_element_type=F32)


def _mod_body(c_ref, w_ref, b_ref, o_ref):
    o_ref[0] = _mm_f32(_silu(c_ref[...]), w_ref[0]) + b_ref[0]


def _modulation(c16, ada_w, ada_b):
    depth, d, n = ada_w.shape
    tn = 1024
    return pl.pallas_call(
        _mod_body, out_shape=S((depth, MOD_ROWS, n), F32), grid=(depth, n // tn),
        in_specs=[pl.BlockSpec((MOD_ROWS, d), lambda l, j: (0, 0)),
                  pl.BlockSpec((1, d, tn), lambda l, j: (l, 0, j)),
                  pl.BlockSpec((1, 1, tn), lambda l, j: (l, 0, j))],
        out_specs=pl.BlockSpec((1, MOD_ROWS, tn), lambda l, j: (l, 0, j)),
        compiler_params=_cp(("parallel", "parallel")), name="adaln_mod",
    )(c16, ada_w, ada_b.reshape(depth, 1, n))


def _norm_mod(x, nw, shift_ref, scale_ref, b, row0, n_lat, n_b):
    tm = x.shape[0]
    y = x * lax.rsqrt(jnp.mean(x * x, axis=-1, keepdims=True) + EPS) * nw
    is_ctx = (row0 + _iota((tm, 1), 0)) >= n_lat
    shift = jnp.where(is_ctx, shift_ref[n_b:n_b + 1, :], shift_ref[pl.ds(b, 1), :])
    scale = jnp.where(is_ctx, scale_ref[n_b:n_b + 1, :], scale_ref[pl.ds(b, 1), :])
    return y * (1.0 + scale) + shift


def _row_gate(gate_ref, b, row0, tm, n_lat, n_b):
    is_ctx = (row0 + _iota((tm, 1), 0)) >= n_lat
    return jnp.where(is_ctx, gate_ref[n_b:n_b + 1, :], gate_ref[pl.ds(b, 1), :])


def _in_proj_body(x_ref, nw_ref, sh_ref, sc_ref, w_ref, bias_ref, o_ref, h_scr, *, n_lat, n_b, tm):
    b = pl.program_id(0)

    @pl.when(pl.program_id(2) == 0)
    def _():
        for rows, mrow in ((slice(0, n_lat), pl.ds(b, 1)), (slice(n_lat, tm), pl.ds(n_b, 1))):
            x = x_ref[0, rows, :]
            y = x * lax.rsqrt(jnp.mean(x * x, axis=-1, keepdims=True) + EPS)
            h_scr[rows, :] = (y * (nw_ref[...] * (1.0 + sc_ref[mrow, :])) + sh_ref[mrow, :]).astype(BF16)

    o_ref[0] = (jnp.dot(h_scr[...], w_ref[...], preferred_element_type=F32) + bias_ref[...]).astype(o_ref.dtype)


def _in_proj(xs, norm_w, mod, shift_idx, scale_idx, w, bias, n_lat, name):
    bsz, t, d = xs.shape
    n = w.shape[1]
    tm = t
    tn = IN_PROJ_TN
    assert n % tn == 0
    body = functools.partial(_in_proj_body, n_lat=n_lat, n_b=bsz, tm=tm)
    return pl.pallas_call(
        body, out_shape=S((bsz, t, n), BF16), grid=(bsz, t // tm, n // tn),
        in_specs=[pl.BlockSpec((1, tm, d), lambda b, i, j: (b, i, 0)),
                  pl.BlockSpec((1, d), lambda b, i, j: (0, 0)),
                  pl.BlockSpec((MOD_ROWS, d), lambda b, i, j: (0, shift_idx)),
                  pl.BlockSpec((MOD_ROWS, d), lambda b, i, j: (0, scale_idx)),
                  pl.BlockSpec((d, tn), lambda b, i, j: (0, j)),
                  pl.BlockSpec((1, tn), lambda b, i, j: (0, j))],
        out_specs=pl.BlockSpec((1, tm, tn), lambda b, i, j: (b, i, j)),
        scratch_shapes=[pltpu.VMEM((tm, d), BF16)],
        compiler_params=_cp(("parallel", "parallel", "arbitrary")), name=name,
    )(xs, norm_w.reshape(1, d), mod, mod, w, bias.reshape(1, n))


def _out_proj_body(y_ref, w_ref, bias_ref, res_ref, gate_ref, nw_ref, sh_ref, sc_ref, rw_ref, o_ref, h_ref, lg_ref,
                   *, n_lat, n_b, tm):
    b, i = pl.program_id(0), pl.program_id(1)
    y = jnp.dot(y_ref[0].astype(BF16), w_ref[...], preferred_element_type=F32) + bias_ref[...]
    x = res_ref[0] + _row_gate(gate_ref, b, i * tm, tm, n_lat, n_b) * y
    o_ref[0] = x
    h = _norm_mod(x, nw_ref[...], sh_ref, sc_ref, b, i * tm, n_lat, n_b)
    hi = h.astype(BF16)
    lo = (h - hi.astype(F32)).astype(BF16)
    h_ref[0] = hi
    r = jnp.dot(jnp.concatenate([hi, lo], axis=0), rw_ref[...], preferred_element_type=F32)
    lg_ref[0] = (r[0:tm, 0:LANES] + r[0:tm, LANES:]) + (r[tm:, 0:LANES] + r[tm:, LANES:])


def _out_proj(y, w, bias, res, mod, n_lat, norm2_w, router_p, name):
    bsz, t, dy = y.shape
    d = w.shape[1]
    tm = t // 3 if t % 3 == 0 and (t // 3) % BF16_ROWS == 0 else t
    body = functools.partial(_out_proj_body, n_lat=n_lat, n_b=bsz, tm=tm)
    tok = lambda width: pl.BlockSpec((1, tm, width), lambda b, i: (b, i, 0))
    modc = lambda idx: pl.BlockSpec((MOD_ROWS, d), lambda b, i: (0, idx))
    return pl.pallas_call(
        body, out_shape=(S((bsz, t, d), F32), S((bsz, t, d), BF16), S((bsz, t, LANES), F32)), grid=(bsz, t // tm),
        in_specs=[tok(dy), pl.BlockSpec((dy, d), lambda b, i: (0, 0)), pl.BlockSpec((1, d), lambda b, i: (0, 0)),
                  tok(d), modc(2), pl.BlockSpec((1, d), lambda b, i: (0, 0)), modc(3), modc(4),
                  pl.BlockSpec((d, 2 * LANES), lambda b, i: (0, 0))],
        out_specs=(tok(d), tok(d), tok(LANES)),
        compiler_params=_cp(("parallel", "parallel")), name=name,
    )(y, w, bias.reshape(1, d), res, mod, norm2_w.reshape(1, d), mod, mod, router_p)


def _dft_tables(n):
    half = n // 2
    k = np.arange(half, dtype=np.int64)[:, None]
    m = np.arange(half, dtype=np.int64)[None, :]
    ang_e = ((k * 2 * m) % (2 * n)).astype(np.float64) * (math.pi / n)
    ang_o = ((k * (2 * m + 1)) % (2 * n)).astype(np.float64) * (math.pi / n)
    ce, se, co, so = np.cos(ang_e), np.sin(ang_e), np.cos(ang_o), np.sin(ang_o)
    return jnp.asarray(np.stack([ce, se, co, so, co.T, so.T]), dtype=BF16)


def _hy_positional(n):
    t = np.linspace(0.0, 1.0, n)[:, None]
    bands = (HY_EMB - 1) // 2
    w = (2.0 * math.pi * np.arange(n) / n)[:, None]
    f = np.linspace(1e-4, bands - 1, bands)[None, :]
    z = np.concatenate([t, np.cos(f * w), -np.sin(f * w)], axis=-1)
    zp = np.zeros((n, LANES), np.float32)
    zp[:, :HY_EMB] = z
    order = np.concatenate([np.arange(0, n, 2), np.arange(1, n, 2)])
    return jnp.asarray(zp[order]), jnp.asarray(t.astype(np.float32)[order])


def _hy_mlp_body(z_ref, w1_ref, b1_ref, w2_ref, b2_ref, fr_ref, h_ref):
    h = jnp.sin(fr_ref[0:1, :] * (_mm_f32(z_ref[...], w1_ref[...]) + b1_ref[...]))
    h_ref[...] = jnp.sin(fr_ref[1:2, :] * (_mm_f32(h, w2_ref[...]) + b2_ref[...]))


def _alt_sign(rows):
    return (1 - 2 * (_iota((rows, 1), 0) & 1)).astype(F32)


def _hy_filter_body(h_ref, w3f_ref, w3b_ref, t_ref, dl_ref, tab_ref, pk_ref, pm_ref, *, n):
    half = n // 2
    win = jnp.exp(-t_ref[...] * dl_ref[...]) + HY_SHIFT
    hf = _mm_f32(h_ref[...], w3f_ref[...]) * win
    hb = _mm_f32(h_ref[...], w3b_ref[...]) * win
    hb = jnp.where(_iota((n, 1), 0) == 0, 0.0, hb)
    hs, hd = hf + hb, hf - hb
    alt = _alt_sign(half)
    pm_ref[0] = jnp.concatenate([jnp.sum(alt * hs[0:half], axis=0, keepdims=True),
                                 -jnp.sum(alt * hd[half:], axis=0, keepdims=True),
                                 jnp.zeros((6, hs.shape[1]), F32)], axis=0) * (1.0 / n)
    wgt = jnp.where(_iota((half, 1), 0) == 0, 0.5 / n, 1.0 / n)
    ce, co = _mm(tab_ref[0], hs[0:half]), _mm(tab_ref[2], hs[half:])
    se, so = _mm(tab_ref[1], hd[0:half]), _mm(tab_ref[3], hd[half:])
    pk_ref[0, 0] = (ce + co) * wgt
    pk_ref[0, 1] = (ce - co) * wgt
    pk_ref[0, 2] = -(se + so) * wgt
    pk_ref[0, 3] = (se - so) * wgt


def _hy_filter_spectra(n, w1p, b1, w2, b2, freq, w3, tabs, d, td):
    zpad, tcol = _hy_positional(n)
    max_decay = math.log(HY_TARGET) / HY_FAST_DECAY
    min_decay = math.log(HY_TARGET) / HY_SLOW_DECAY
    deltas = jnp.asarray(np.abs(np.linspace(min_decay, max_decay, d)), dtype=F32)[None, :]
    nd = d // td
    ff = w1p.shape[1]
    half = n // 2
    h = pl.pallas_call(_hy_mlp_body, out_shape=S((n, ff), F32), name=f"hy_filter_mlp_{n}")(
        zpad, w1p, b1.reshape(1, ff), w2, b2.reshape(1, ff), freq)
    body = functools.partial(_hy_filter_body, n=n)
    const = lambda o, j: (0, 0)
    return pl.pallas_call(
        body, out_shape=(S((HY_ORDER, 4, half, d), F32), S((HY_ORDER, 8, d), F32)), grid=(HY_ORDER, nd),
        in_specs=[pl.BlockSpec((n, ff), const),
                  pl.BlockSpec((ff, td), lambda o, j: (0, (2 * o) * nd + j)),
                  pl.BlockSpec((ff, td), lambda o, j: (0, (2 * o + 1) * nd + j)),
                  pl.BlockSpec((n, 1), const), pl.BlockSpec((1, td), lambda o, j: (0, j)),
                  pl.BlockSpec((6, half, half), lambda o, j: (0, 0, 0), pipeline_mode=pl.Buffered(1))],
        out_specs=(pl.BlockSpec((1, 4, half, td), lambda o, j: (o, 0, 0, j)),
                   pl.BlockSpec((1, 8, td), lambda o, j: (o, 0, j))),
        compiler_params=_cp(("parallel", "parallel")), name=f"hy_filter_{n}",
    )(h, w3, w3, tcol, deltas, tabs)


def _short_conv(x, w_ref, b_ref):
    n = x.shape[0]
    row = _iota((n, 1), 0)
    xp = jnp.where(row == 0, 0.0, pltpu.roll(x, 1, 0))
    xn = jnp.where(row == n - 1, 0.0, pltpu.roll(x, n - 1, 0))
    return w_ref[0:1, :] * xp + w_ref[1:2, :] * x + w_ref[2:3, :] * xn + b_ref[...]


def _hy_conv_body(a_ref, g_ref, wa_ref, ba_ref, wg_ref, bg_ref, tab_ref, pk_ref, pm_ref, fb_ref, o_ref,
                  a_s, y_s, *, n, conv_a):
    half = n // 2
    nt = a_s.shape[0]
    a = a_ref[0].astype(F32)
    if conv_a:
        a = _short_conv(a, wa_ref, ba_ref)
    for c in range(nt):
        a_s[c] = a[:, c * LANES:(c + 1) * LANES]
    ve = jnp.concatenate([a_s[c, pl.ds(0, half, stride=2), :] for c in range(nt)], axis=1)
    vo = jnp.concatenate([a_s[c, pl.ds(1, half, stride=2), :] for c in range(nt)], axis=1)
    ec, es = _mm(tab_ref[0], ve), _mm(tab_ref[1], ve)
    oc, os_ = _mm(tab_ref[2], vo), _mm(tab_ref[3], vo)
    alt = _alt_sign(half)
    xr_mid = jnp.sum(alt * ve, axis=0, keepdims=True)
    xi_mid = -jnp.sum(alt * vo, axis=0, keepdims=True)
    kr_lo, kr_hi, ki_lo, ki_hi = pk_ref[0, 0], pk_ref[0, 1], pk_ref[0, 2], pk_ref[0, 3]
    xr_lo, xr_hi, xi_lo, xi_hi = ec + oc, ec - oc, -(es + os_), es - os_
    yr_lo = xr_lo * kr_lo - xi_lo * ki_lo
    yi_lo = xr_lo * ki_lo + xi_lo * kr_lo
    yr_hi = xr_hi * kr_hi - xi_hi * ki_hi
    yi_hi = xr_hi * ki_hi + xi_hi * kr_hi
    kr_mid, ki_mid = pm_ref[0, 0:1, :], pm_ref[0, 1:2, :]
    yr_mid = xr_mid * kr_mid - xi_mid * ki_mid
    yi_mid = xr_mid * ki_mid + xi_mid * kr_mid
    y_even = _mm(tab_ref[0], yr_lo + yr_hi) - _mm(tab_ref[1], yi_lo - yi_hi) + alt * yr_mid
    y_odd = _mm(tab_ref[4], yr_lo - yr_hi) - _mm(tab_ref[5], yi_lo + yi_hi) - alt * yi_mid
    for c in range(nt):
        y_s[c, pl.ds(0, half, stride=2), :] = y_even[:, c * LANES:(c + 1) * LANES]
        y_s[c, pl.ds(1, half, stride=2), :] = y_odd[:, c * LANES:(c + 1) * LANES]
    y = jnp.concatenate([y_s[c] for c in range(nt)], axis=1)
    g = _short_conv(g_ref[0].astype(F32), wg_ref, bg_ref)
    o_ref[0] = g * (y + a * fb_ref[0])


def _hy_conv(a, a_col0, a_row, g, g_col0, g_row, n, conv_w, conv_b, tabs, pk, pm, fbias, order, conv_a, d, td, name):
    bsz = a.shape[0]
    nd = d // td
    half = n // 2
    body = functools.partial(_hy_conv_body, n=n, conv_a=conv_a)
    cw = conv_w
    cb = conv_b.reshape(1, -1)
    return pl.pallas_call(
        body, out_shape=S((bsz, n, d), F32), grid=(nd, bsz),
        in_specs=[pl.BlockSpec((1, n, td), lambda j, b: (b, a_row, a_col0 * nd + j)),
                  pl.BlockSpec((1, n, td), lambda j, b: (b, g_row, g_col0 * nd + j)),
                  pl.BlockSpec((3, td), lambda j, b: (0, a_col0 * nd + j if conv_a else j)),
                  pl.BlockSpec((1, td), lambda j, b: (0, a_col0 * nd + j if conv_a else j)),
                  pl.BlockSpec((3, td), lambda j, b: (0, g_col0 * nd + j)),
                  pl.BlockSpec((1, td), lambda j, b: (0, g_col0 * nd + j)),
                  pl.BlockSpec((6, half, half), lambda j, b: (0, 0, 0), pipeline_mode=pl.Buffered(1)),
                  pl.BlockSpec((1, 4, half, td), lambda j, b: (order, 0, 0, j), pipeline_mode=pl.Buffered(1)),
                  pl.BlockSpec((1, 8, td), lambda j, b: (order, 0, j)),
                  pl.BlockSpec((1, 1, td), lambda j, b: (order, 0, j))],
        out_specs=pl.BlockSpec((1, n, td), lambda j, b: (b, 0, j)),
        scratch_shapes=[pltpu.VMEM((td // LANES, n, LANES), F32), pltpu.VMEM((td // LANES, n, LANES), F32)],
        compiler_params=_cp(("parallel", "parallel"), 56), name=name,
    )(a, g, cw, cb, cw, cb, tabs, pk, pm, fbias.reshape(HY_ORDER, 1, d))


def _hyena_mixer(u, n_lat, n_ctx, short_w, short_b, w1, b1, w2, b2, w3, freq, fbias):
    d = u.shape[2] // 3
    ff = w1.shape[1]
    w1p = jnp.zeros((LANES, ff), F32).at[:w1.shape[0]].set(w1)
    outs = []
    for n, row in ((n_lat, 0), (n_ctx, n_lat // n_ctx)):
        td = 256 if n > 512 else 512
        tabs = _dft_tables(n)
        pk, pm = _hy_filter_spectra(n, w1p, b1, w2, b2, freq, w3, tabs, d, td)
        z1 = _hy_conv(u, 0, row, u, 1, row, n, short_w, short_b, tabs, pk, pm, fbias, 0, True, d, td, f"hy_conv1_{n}")
        z2 = _hy_conv(z1, 0, 0, u, 2, row, n, short_w, short_b, tabs, pk, pm, fbias, 1, False, d, td, f"hy_conv2_{n}")
        outs.append(z2)
    return jnp.concatenate(outs, axis=1)


def _rope_tables(n_lat, width, rot_heads):
    hd = SW_HD
    rows = n_lat // GRID_W
    row = np.repeat(np.arange(rows, dtype=np.float64), GRID_W)
    col = np.tile(np.arange(GRID_W, dtype=np.float64), rows)
    nf = hd // 4
    inv = ROPE_BASE ** (-np.arange(nf, dtype=np.float64) / nf)
    ang = np.concatenate([row[:, None] * inv, col[:, None] * inv], axis=-1)
    cos, sin = np.cos(ang), np.sin(ang)
    zero = np.zeros_like(sin)
    c = np.ones((n_lat, width), np.float32)
    sa = np.zeros((n_lat, width), np.float32)
    sb = np.zeros((n_lat, width), np.float32)
    for h in range(rot_heads):
        c[:, h * hd:(h + 1) * hd] = np.concatenate([cos, cos], axis=-1)
        sa[:, h * hd:(h + 1) * hd] = np.concatenate([-sin, zero], axis=-1)
        sb[:, h * hd:(h + 1) * hd] = np.concatenate([zero, sin], axis=-1)
    return jnp.asarray(c), jnp.asarray(sa), jnp.asarray(sb)


def _rope(x, c, sa, sb):
    w = x.shape[1]
    half = SW_HD // 2
    return x * c + pltpu.roll(x, w - half, 1) * sa + pltpu.roll(x, half, 1) * sb


def _sink_attend(q, kvs, sink):
    ss = []
    m = None
    for k, _, mask in kvs:
        s = _mm_nt(q, k)
        if mask is not None:
            s = jnp.where(mask, s, NEG_INF)
        ss.append(s)
        sm = jnp.max(s, axis=-1, keepdims=True)
        m = sm if m is None else jnp.maximum(m, sm)
    m = jnp.maximum(m, sink)
    den = jnp.exp(sink - m)
    o = None
    for s, (_, v, _) in zip(ss, kvs):
        p = jnp.exp(s - m)
        den = den + jnp.sum(p, axis=-1, keepdims=True)
        pv = _mm(p, v)
        o = pv if o is None else o + pv
    return o / den


def _swa_body(u_ref, cq_ref, saq_ref, sbq_ref, ck_ref, sak_ref, sbk_ref, sink_ref, o_ref, kv_scr, s_a, s_b,
              *, n_lat, n_ctx, blk):
    g = pl.program_id(1)
    hd, rep = SW_HD, SW_REP
    qw = rep * hd
    scale = hd ** -0.5
    span = 3 * blk
    nb = n_lat // blk
    kv_scr[0:n_lat, :] = _rope(u_ref[0, 0:n_lat, qw:qw + 2 * hd].astype(F32), ck_ref[...], sak_ref[...],
                               sbk_ref[...]).astype(BF16)
    kv_scr[n_lat:, :] = u_ref[0, n_lat:, qw:qw + 2 * hd]
    kc = kv_scr[n_lat:, 0:hd]
    vc = kv_scr[n_lat:, hd:2 * hd]

    def key_start(i):
        return pl.multiple_of(jnp.clip((i - 1) * blk, 0, n_lat - span), blk)

    def scores(i, s_ref):
        rows = pl.ds(pl.multiple_of(i * blk, blk), blk)
        q = _rope(u_ref[0, rows, 0:qw].astype(F32), cq_ref[rows, :], saq_ref[rows, :], sbq_ref[rows, :]) * scale
        q = q.astype(BF16)
        kl = kv_scr[pl.ds(key_start(i), span), 0:hd]
        for r in range(rep):
            s_ref[r, :, 0:n_ctx] = _mm_nt(q[:, r * hd:(r + 1) * hd], kc)
            s_ref[r, :, n_ctx:] = _mm_nt(q[:, r * hd:(r + 1) * hd], kl)

    def attend(i, s_ref):
        r0 = pl.multiple_of(i * blk, blk)
        ks = key_start(i)
        vl = kv_scr[pl.ds(ks, span), hd:2 * hd]
        mask = jnp.abs((r0 + _iota((blk, span), 0)) - (ks + _iota((blk, span), 1))) <= SW_WINDOW
        for r in range(rep):
            sink = sink_ref[g * rep + r]
            sc = s_ref[r, :, 0:n_ctx]
            sl = jnp.where(mask, s_ref[r, :, n_ctx:], NEG_INF)
            m = jnp.maximum(jnp.maximum(jnp.max(sc, axis=-1, keepdims=True), jnp.max(sl, axis=-1, keepdims=True)), sink)
            pc = jnp.exp(sc - m)
            pl_ = jnp.exp(sl - m)
            den = jnp.exp(sink - m) + jnp.sum(pc, axis=-1, keepdims=True) + jnp.sum(pl_, axis=-1, keepdims=True)
            o_ref[0, pl.ds(r0, blk), r * hd:(r + 1) * hd] = (_mm(pc, vc) + _mm(pl_, vl)) / den

    scores(0, s_a)

    def pair(j, carry):
        scores(2 * j + 1, s_b)
        attend(2 * j, s_a)
        scores(jnp.minimum(2 * j + 2, nb - 1), s_a)
        attend(2 * j + 1, s_b)
        return carry

    lax.fori_loop(0, nb // 2, pair, 0)
    qc = (u_ref[0, n_lat:, 0:qw].astype(F32) * scale).astype(BF16)
    for r in range(rep):
        o_ref[0, n_lat:, r * hd:(r + 1) * hd] = _sink_attend(qc[:, r * hd:(r + 1) * hd], [(kc, vc, None)], sink_ref[g * rep + r])


def _swa_mixer(u, n_lat, n_ctx, sink):
    bsz, t, _ = u.shape
    hd, rep = SW_HD, SW_REP
    gw = rep * hd + 2 * hd
    blk = SW_WINDOW
    assert n_lat % (2 * blk) == 0 and n_lat >= 3 * blk and n_lat % GRID_W == 0
    cq, saq, sbq = _rope_tables(n_lat, rep * hd, rep)
    ck, sak, sbk = _rope_tables(n_lat, 2 * hd, 1)
    body = functools.partial(_swa_body, n_lat=n_lat, n_ctx=n_ctx, blk=blk)
    tab = lambda w: pl.BlockSpec((n_lat, w), lambda b, g: (0, 0))
    return pl.pallas_call(
        body, out_shape=S((bsz, t, SW_HQ * hd), F32), grid=(bsz, SW_HKV),
        in_specs=[pl.BlockSpec((1, t, gw), lambda b, g: (b, 0, g)),
                  tab(rep * hd), tab(rep * hd), tab(rep * hd), tab(2 * hd), tab(2 * hd), tab(2 * hd),
                  pl.BlockSpec(memory_space=pltpu.SMEM)],
        out_specs=pl.BlockSpec((1, t, rep * hd), lambda b, g: (b, 0, g)),
        scratch_shapes=[pltpu.VMEM((t, 2 * hd), BF16), pltpu.VMEM((rep, blk, n_ctx + 3 * blk), F32),
                        pltpu.VMEM((rep, blk, n_ctx + 3 * blk), F32)],
        compiler_params=_cp(("parallel", "parallel")), name="swa_attention",
    )(u, cq, saq, sbq, ck, sak, sbk, sink)


def _seq_conv(x, w_ref, n_lat):
    t = x.shape[0]
    row = _iota((t, 1), 0)
    first = (row == 0) | (row == n_lat)
    last = (row == n_lat - 1) | (row == t - 1)
    xp = jnp.where(first, 0.0, pltpu.roll(x, 1, 0))
    xn = jnp.where(last, 0.0, pltpu.roll(x, t - 1, 0))
    return w_ref[0:1, :] * xp + w_ref[1:2, :] * x + w_ref[2:3, :] * xn


def _chunk_scan(x, reverse):
    t = x.shape[0]
    pos = _iota((t, 1), 0) & (CHUNK - 1)
    s = 1
    while s < CHUNK:
        if reverse:
            x = x + jnp.where(pos < CHUNK - s, pltpu.roll(x, t - s, 0), 0.0)
        else:
            x = x + jnp.where(pos >= s, pltpu.roll(x, s, 0), 0.0)
        s *= 2
    return x


def _chunk_scan_lanes(x, reverse):
    t = x.shape[1]
    pos = _iota((1, t), 1) & (CHUNK - 1)
    s = 1
    while s < CHUNK:
        if reverse:
            x = x + jnp.where(pos < CHUNK - s, pltpu.roll(x, t - s, 1), 0.0)
        else:
            x = x + jnp.where(pos >= s, pltpu.roll(x, s, 1), 0.0)
        s *= 2
    return x


def _chunk_order(s, n_lat_chunks, n_ctx_chunks, direction):
    if direction == 0:
        return jnp.where(s < n_ctx_chunks, n_lat_chunks + s, s - n_ctx_chunks)
    return n_lat_chunks + n_ctx_chunks - 1 - s


def _intra_unroll(n_super, sup=SUPER):
    return next(u for u in range(INTRA_ROWS // sup, 0, -1) if n_super % u == 0)


def _row_sumsq(x):
    return jnp.dot((x * x).astype(BF16), jnp.ones((LANES, LANES), BF16), preferred_element_type=F32)


def _gated_rms(o, nw, z):
    return o * lax.rsqrt(_row_sumsq(o) * (1.0 / LANES) + EPS) * nw * _silu(z)


def _bmm(a, b):
    return jnp.einsum('bij,bjk->bik', a.astype(BF16), b.astype(BF16), preferred_element_type=F32)


def _bmm_nt(a, b):
    return jnp.einsum('bik,bjk->bij', a.astype(BF16), b.astype(BF16), preferred_element_type=F32)


def _unit_tri_inverse(a, ii, jj):
    eye = (ii == jj).astype(F32)
    a8 = jnp.where((ii >> 3) == (jj >> 3), a, 0.0)
    a8_2 = _bmm(a8, a8)
    a8_4 = _bmm(a8_2, a8_2)
    x = _bmm(_bmm(eye - a8, eye + a8_2), eye + a8_4)
    sh = 3
    while (1 << sh) < CHUNK:
        e = jnp.where(((ii >> (sh + 1)) == (jj >> (sh + 1))) & ((ii >> sh) != (jj >> sh)), a, 0.0)
        x = x - _bmm(_bmm(x, e), x)
        sh += 1
    return x


def _gdn_body(q_ref, k_ref, v_ref, z_ref, ba_ref, cwq_ref, cwk_ref, cwv_ref, par_ref, nw_ref, o_ref,
              kn_s, qn_s, qe_s, kk_s, egl_s, kb_s, kbe_s, vb_s, gc_s, gt_s, qp_s, k2_s, n_s, oacc,
              *, n_lat, n_ctx, hp):
    t = n_lat + n_ctx
    for hh in range(hp):
        _gdn_head_local(hh, pl.program_id(1) * hp + hh, q_ref, k_ref, v_ref, ba_ref, cwq_ref, cwk_ref, cwv_ref, par_ref,
                        kn_s, qn_s, qe_s, kk_s, egl_s, kb_s, kbe_s, vb_s, gc_s, gt_s, qp_s, k2_s, n_s, oacc,
                        n_lat=n_lat, t=t)

    nl, nc = n_lat // CHUNK, n_ctx // CHUNK

    def step(s, states):
        new = []
        for hh in range(hp):
            for d in range(2):
                st = states[2 * hh + d]
                sb = st.astype(BF16)
                c = _chunk_order(s, nl, nc, d)
                rows = pl.ds(pl.multiple_of(c * CHUNK, CHUNK), CHUNK)
                crow = pl.ds(pl.multiple_of(c * GD_DK, GD_DK), GD_DK)
                oacc[hh, rows, :] += jnp.dot(qp_s[hh, d, rows, :], sb, preferred_element_type=F32)
                new.append(st * egl_s[hh, d, pl.ds(c, 1), :] + n_s[hh, d, crow, :]
                           - jnp.dot(k2_s[hh, d, crow, :], sb, preferred_element_type=F32))
        return tuple(new)

    zero = jnp.zeros((GD_DK, LANES), F32)
    lax.fori_loop(0, nl + nc, step, (zero,) * (2 * hp))
    for hh in range(hp):
        cs = slice(hh * LANES, (hh + 1) * LANES)
        o_ref[0, :, cs] = _gated_rms(oacc[hh], nw_ref[...], z_ref[0, :, cs].astype(F32))


def _gdn_head_local(hh, h, q_ref, k_ref, v_ref, ba_ref, cwq_ref, cwk_ref, cwv_ref, par_ref,
                    kn_s, qn_s, qe_s, kk_s, egl_s, kb_s, kbe_s, vb_s, gc_s, gt_s, qp_s, k2_s, n_s, oacc, *, n_lat, t):
    cs = slice(hh * LANES, (hh + 1) * LANES)
    lane = _iota((1, LANES), 1)
    q = _silu(_seq_conv(q_ref[0, :, cs].astype(F32), cwq_ref[:, cs], n_lat))
    k = _silu(_seq_conv(k_ref[0, :, cs].astype(F32), cwk_ref[:, cs], n_lat))
    v = _silu(_seq_conv(v_ref[0, :, cs].astype(F32), cwv_ref[:, cs], n_lat))
    qn = q * lax.rsqrt(_row_sumsq(q) + EPS) * (GD_DK ** -0.5)
    kn = k * lax.rsqrt(_row_sumsq(k) + EPS)
    kn_s[...] = kn.astype(BF16)
    qn_s[...] = qn.astype(BF16)
    par = par_ref[...]

    def pick(x, c):
        return jnp.sum(jnp.where(lane == c, x, 0.0), axis=-1, keepdims=True)

    src = _iota((LANES, LANES), 0)
    dst = _iota((LANES, LANES), 1)
    want = jnp.where(dst < 2, 2 * GD_H + dst * GD_H + h, (dst - 2) * GD_H + h)
    onehot = jnp.where((src == want) & (dst < 4), 1.0, 0.0)
    raw_t = jnp.dot(ba_ref[0].astype(F32), onehot, preferred_element_type=F32).T[0:8, :]
    row8 = _iota((8, 1), 0)
    a_log = jnp.where(row8 == 0, pick(par[0:1, :], h), pick(par[1:2, :], h))
    dt_b = jnp.where(row8 == 0, pick(par[2:3, :], h), pick(par[3:4, :], h))
    g_t = -jnp.exp(a_log) * _softplus(raw_t + dt_b)
    beta_t = _sigmoid(raw_t)
    pre = _chunk_scan_lanes(g_t, False)
    suf = _chunk_scan_lanes(g_t, True)
    gcum_t = jnp.where(row8 == 1, suf, pre)
    glast_t = pre + suf - g_t
    gt_s[...] = gcum_t
    up2 = lambda x: pltpu.roll(x, 2, 0)
    packed = jnp.concatenate(
        [jnp.where(row8 < 2, gcum_t, up2(jnp.exp(gcum_t))),
         jnp.where(row8 < 2, jnp.exp(glast_t - gcum_t), jnp.where(row8 < 4, up2(jnp.exp(glast_t)), up2(beta_t))),
         jnp.zeros((LANES - 16, t), F32)], axis=0)
    cols = packed.T
    gc_s[...] = cols
    for d in range(2):
        beta = cols[:, 12 + d:13 + d]
        eg = jnp.broadcast_to(cols[:, 2 + d:3 + d], (t, LANES))
        kb = kn * beta
        qe_s[d] = (qn * eg).astype(BF16)
        kk_s[d] = (kn * cols[:, 8 + d:9 + d]).astype(BF16)
        ends = gc_s[pl.ds(0, t // CHUNK, stride=CHUNK), :]
        egl_s[hh, d] = jnp.broadcast_to(ends[:, 10 + d:11 + d], (t // CHUNK, LANES))
        kb_s[d] = kb.astype(BF16)
        kbe_s[d] = (kb * eg).astype(BF16)
        vb_s[d] = (v * beta).astype(BF16)

    sup = GD_SUPER
    ii = _iota((sup, sup), 0)
    jj = _iota((sup, sup), 1)
    same = (ii >> 6) == (jj >> 6)
    per = sup // CHUNK
    unroll = _intra_unroll(t // sup, sup)

    def intra(it, carry):
        r0s = [pl.multiple_of((it * unroll + kq) * sup, sup) for kq in range(unroll)]
        kcs = [kn_s[pl.ds(r0, sup), :] for r0 in r0s]
        pairs = [(kq, d) for kq in range(unroll) for d in range(2)]
        decs = []
        for kq, d in pairs:
            rows = pl.ds(r0s[kq], sup)
            incl = same & ((jj >= ii) if d else (jj <= ii))
            dif = jnp.broadcast_to(gc_s[rows, d:d + 1], (sup, sup)) - gt_s[d:d + 1, rows]
            decs.append(jnp.where(incl, jnp.exp(jnp.where(incl, dif, 0.0)), 0.0))
        dec = jnp.stack(decs)
        kc2 = jnp.stack([kcs[kq] for kq, _ in pairs])
        kb = jnp.stack([kb_s[d, pl.ds(r0s[kq], sup), :] for kq, d in pairs])
        rhs = jnp.stack([jnp.concatenate([vb_s[d, pl.ds(r0s[kq], sup), :], kbe_s[d, pl.ds(r0s[kq], sup), :]], axis=1)
                         for kq, d in pairs])
        a = jnp.where(ii == jj, 0.0, _bmm_nt(kb, kc2) * dec)
        uw = _bmm(_unit_tri_inverse(a, ii, jj), rhs).astype(BF16)
        qk = _bmm_nt(jnp.stack([qn_s[pl.ds(r0, sup), :] for r0 in r0s]), jnp.stack(kcs))
        auw = _bmm(jnp.stack([qk[kq] for kq, _ in pairs]) * dec, uw)
        for i, (kq, d) in enumerate(pairs):
            rows = pl.ds(r0s[kq], sup)
            qp_s[hh, d, rows, :] = (qe_s[d, rows, :].astype(F32) - auw[i, :, LANES:]).astype(BF16)
            if d == 1:
                oacc[hh, rows, :] = auw[i - 1, :, 0:LANES] + auw[i, :, 0:LANES]
            for c4 in range(per):
                kuw = _mm_tn(kk_s[d, pl.ds(r0s[kq] + c4 * CHUNK, CHUNK), :], uw[i, c4 * CHUNK:(c4 + 1) * CHUNK])
                crow = pl.ds(pl.multiple_of(((it * unroll + kq) * per + c4) * GD_DK, GD_DK), GD_DK)
                n_s[hh, d, crow, :] = kuw[:, 0:LANES]
                k2_s[hh, d, crow, :] = kuw[:, LANES:].astype(BF16)
        return carry

    lax.fori_loop(0, t // sup // unroll, intra, 0)


def _gdn_mixer(u, n_lat, n_ctx, conv_w, a_log, dt_bias, norm_w):
    bsz, t, _ = u.shape
    assert t % SUPER == 0 and n_lat % SUPER == 0
    par = jnp.zeros((8, LANES), F32).at[0:2, :GD_H].set(a_log).at[2:4, :GD_H].set(dt_bias)
    hp = GD_HEADS_PER_STEP
    body = functools.partial(_gdn_body, n_lat=n_lat, n_ctx=n_ctx, hp=hp)
    sec = lambda s: pl.BlockSpec((1, t, hp * LANES), lambda b, h: (b, 0, s * (GD_H // hp) + h))
    cw = lambda s: pl.BlockSpec((3, hp * LANES), lambda b, h: (0, s * (GD_H // hp) + h))
    both = lambda dt: pltpu.VMEM((2, t, LANES), dt)
    state_rows = (t // CHUNK) * GD_DK
    return pl.pallas_call(
        body, out_shape=S((bsz, t, GD_H * LANES), F32), grid=(bsz, GD_H // hp),
        in_specs=[sec(0), sec(1), sec(2), sec(3),
                  pl.BlockSpec((1, t, LANES), lambda b, h: (b, 0, 4 * GD_H)),
                  cw(0), cw(1), cw(2),
                  pl.BlockSpec((8, LANES), lambda b, h: (0, 0)),
                  pl.BlockSpec((1, LANES), lambda b, h: (0, 0))],
        out_specs=pl.BlockSpec((1, t, hp * LANES), lambda b, h: (b, 0, h)),
        scratch_shapes=[pltpu.VMEM((t, LANES), BF16), pltpu.VMEM((t, LANES), BF16),
                        both(BF16), both(BF16), pltpu.VMEM((hp, 2, t // CHUNK, LANES), F32),
                        both(BF16), both(BF16), both(BF16),
                        pltpu.VMEM((t, LANES), F32),
                        pltpu.VMEM((8, t), F32), pltpu.VMEM((hp, 2, t, LANES), BF16),
                        pltpu.VMEM((hp, 2, state_rows, LANES), BF16),
                        pltpu.VMEM((hp, 2, state_rows, LANES), F32), pltpu.VMEM((hp, t, LANES), F32)],
        compiler_params=_cp(("parallel", "parallel"), 56), name="gdn_mixer",
    )(u, u, u, u, u, conv_w, conv_w, conv_w, par, norm_w.reshape(1, LANES))


def _hgrn_body(q_ref, ff_ref, fb_ref, i_ref, g_ref, lbp_ref, nw_ref, o_ref,
               qe_s, ke_s, kk_s, egl_s, v_s, n_s, st_s, oacc, *, n_lat, n_ctx, layer):
    t = n_lat + n_ctx
    e = jnp.exp(lbp_ref[...] - jnp.max(lbp_ref[...], axis=0, keepdims=True))
    lb = jnp.sum(e[1:layer + 1, :], axis=0, keepdims=True) / jnp.sum(e, axis=0, keepdims=True)
    q = _silu(q_ref[0].astype(F32))
    v_s[...] = i_ref[0]
    for d, f_ref in enumerate((ff_ref, fb_ref)):
        f = f_ref[0].astype(F32)
        sig = _sigmoid(f)
        logf = jnp.log(lb + (1.0 - lb) * sig)
        kin = (1.0 - lb) * (1.0 - sig)
        gc = _chunk_scan(logf, bool(d))
        ends = gc.reshape(t // CHUNK, CHUNK, LANES)[:, 0:1, :] if d else gc.reshape(t // CHUNK, CHUNK, LANES)[:, CHUNK - 1:, :]
        glast = jnp.broadcast_to(ends, (t // CHUNK, CHUNK, LANES)).reshape(t, LANES)
        egl = jnp.exp(glast)
        ke = kin * jnp.exp(-gc)
        qe_s[d] = (q * jnp.exp(gc)).astype(BF16)
        ke_s[d] = ke.astype(BF16)
        kk_s[d] = (ke * egl).astype(BF16)
        egl_s[d] = egl

    ii = _iota((SUPER, SUPER), 0)
    jj = _iota((SUPER, SUPER), 1)
    same = (ii >> 6) == (jj >> 6)
    per = SUPER // CHUNK
    unroll = _intra_unroll(t // SUPER)

    def chunk_rows(sc, c4):
        rows = pl.ds(pl.multiple_of(sc * SUPER + c4 * CHUNK, CHUNK), CHUNK)
        crow = pl.ds(pl.multiple_of((sc * per + c4) * LANES, LANES), LANES)
        return rows, crow

    def intra(it, carry):
        pairs = [(kq, d) for kq in range(unroll) for d in range(2)]
        rows = [pl.ds(pl.multiple_of((it * unroll + kq) * SUPER, SUPER), SUPER) for kq in range(unroll)]
        incl = jnp.stack([same & ((jj >= ii) if d else (jj <= ii)) for _, d in pairs])
        at = jnp.where(incl, _bmm_nt(jnp.stack([qe_s[d, rows[kq], :] for kq, d in pairs]),
                                     jnp.stack([ke_s[d, rows[kq], :] for kq, d in pairs])), 0.0)
        part = _bmm(at, jnp.stack([v_s[rows[kq], :] for kq, _ in pairs]))
        for kq in range(unroll):
            oacc[rows[kq], :] = part[2 * kq] + part[2 * kq + 1]
            for d in range(2):
                for c4 in range(per):
                    r64, crow = chunk_rows(it * unroll + kq, c4)
                    n_s[d, crow, :] = _mm_tn(v_s[r64, :], kk_s[d, r64, :])
        return carry

    lax.fori_loop(0, t // SUPER // unroll, intra, 0)
    nl, nc = n_lat // CHUNK, n_ctx // CHUNK

    def scan(s, states):
        new = []
        for d in range(2):
            c = _chunk_order(s, nl, nc, d)
            crow = pl.ds(pl.multiple_of(c * LANES, LANES), LANES)
            st_s[d, crow, :] = states[d].astype(BF16)
            new.append(states[d] * egl_s[d, pl.ds(c * CHUNK, 1), :] + n_s[d, crow, :])
        return tuple(new)

    zero = jnp.zeros((LANES, HG_DK), F32)
    lax.fori_loop(0, nl + nc, scan, (zero, zero))

    def inter(it, carry):
        for kq in range(unroll):
            sc = it * unroll + kq
            for c4 in range(per):
                r64, crow = chunk_rows(sc, c4)
                oacc[r64, :] += (_mm_nt(qe_s[0, r64, :], st_s[0, crow, :]) + _mm_nt(qe_s[1, r64, :], st_s[1, crow, :]))
        return carry

    lax.fori_loop(0, t // SUPER // unroll, inter, 0)
    o_ref[0] = _gated_rms(oacc[...], nw_ref[...], g_ref[0].astype(F32))


def _hgrn_mixer(u, n_lat, n_ctx, hg_lb, norm_w, layer):
    bsz, t, n5 = u.shape
    d = n5 // 5
    nh = d // HG_DK
    depth = hg_lb.shape[0]
    assert t % SUPER == 0 and n_lat % SUPER == 0
    body = functools.partial(_hgrn_body, n_lat=n_lat, n_ctx=n_ctx, layer=layer)
    sec = lambda s: pl.BlockSpec((1, t, LANES), lambda b, h: (b, 0, s * nh + h))
    both = lambda dt: pltpu.VMEM((2, t, LANES), dt)
    state_rows = (t // CHUNK) * LANES
    return pl.pallas_call(
        body, out_shape=S((bsz, t, d), F32), grid=(bsz, nh),
        in_specs=[sec(0), sec(1), sec(2), sec(3), sec(4),
                  pl.BlockSpec((depth, LANES), lambda b, h: (0, h)),
                  pl.BlockSpec((1, LANES), lambda b, h: (0, 0))],
        out_specs=pl.BlockSpec((1, t, LANES), lambda b, h: (b, 0, h)),
        scratch_shapes=[both(BF16), both(BF16), both(BF16), both(F32), pltpu.VMEM((t, LANES), BF16),
                        pltpu.VMEM((2, state_rows, LANES), F32), pltpu.VMEM((2, state_rows, LANES), BF16),
                        pltpu.VMEM((t, LANES), F32)],
        compiler_params=_cp(("parallel", "parallel")), name="hgrn2_mixer",
    )(u, u, u, u, u, hg_lb, norm_w.reshape(1, LANES))


def _excl_count_lanes(x):
    n = x.shape[1]
    blk = min(n, 256)
    tri = jnp.where(_iota((blk, blk), 0) < _iota((blk, blk), 1), 1.0, 0.0).astype(BF16)
    run = jnp.zeros((x.shape[0], 1), F32)
    outs = []
    for r in range(n // blk):
        xb = x[:, r * blk:(r + 1) * blk]
        outs.append(jnp.dot(xb.astype(BF16), tri, preferred_element_type=F32) + run)
        run = run + jnp.sum(xb, axis=1, keepdims=True)
    return jnp.concatenate(outs, axis=1) if len(outs) > 1 else outs[0]


def _topcap_slots(groups):
    def count_ge(v, thr):
        return jnp.sum(jnp.where(v >= thr, 1.0, 0.0), axis=1, keepdims=True)

    def bisect(_, carry):
        new = []
        for (v, cap), (lo, hi) in zip(groups, carry):
            mid = jnp.sqrt(jnp.maximum(lo, 1e-37)) * jnp.sqrt(hi)
            ok = count_ge(v, mid) >= float(cap)
            new.append((jnp.where(ok, mid, lo), jnp.where(ok, hi, mid)))
        return tuple(new)

    init = tuple((jnp.zeros((v.shape[0], 1), F32), jnp.full((v.shape[0], 1), 2.0, F32)) for v, _ in groups)
    brackets = lax.fori_loop(0, 34, bisect, init)
    codes = []
    for (v, cap), (lo, hi) in zip(groups, brackets):
        thr, found, upper = lo, jnp.zeros_like(lo), hi
        for _ in range(4):
            m = jnp.max(jnp.where(v < upper, v, -1.0), axis=1, keepdims=True)
            ok = jnp.where(count_ge(v, m) >= float(cap), 1.0, 0.0) * (1.0 - found)
            thr = jnp.where(ok > 0, m, thr)
            found = jnp.maximum(found, ok)
            upper = jnp.where(found > 0, upper, m)
        gt = jnp.where(v > thr, 1.0, 0.0)
        eq = jnp.where(v == thr, 1.0, 0.0)
        need = float(cap) - jnp.sum(gt, axis=1, keepdims=True)
        sel = jnp.maximum(gt, eq * jnp.where(_excl_count_lanes(eq) < need, 1.0, 0.0))
        codes.append(jnp.where(sel > 0, _excl_count_lanes(sel), -1.0))
    return codes


def _route_body(lg_ref, aff_ref, code_ref, codet_ref, *, n_lat, cap_l, cap_c):
    t = lg_ref.shape[1]
    lane_ok = _iota((1, LANES), 1) < N_EXPERTS
    lg = jnp.where(lane_ok, lg_ref[0], NEG_INF)
    e = jnp.exp(lg - jnp.max(lg, axis=-1, keepdims=True))
    aff = e / jnp.sum(e, axis=-1, keepdims=True)
    aff_ref[0] = aff
    aff_t = aff.T[0:N_EXPERTS, :]
    code_l, code_c = _topcap_slots([(aff_t[:, 0:n_lat], cap_l), (aff_t[:, n_lat:], cap_c)])
    code_t = jnp.concatenate([code_l, code_c], axis=1)
    codet_ref[0] = code_t
    code_ref[0] = jnp.concatenate([code_t, jnp.full((LANES - N_EXPERTS, t), -1.0, F32)], axis=0).T


def _route(logits, n_lat, cap_l, cap_c):
    bsz, t, _ = logits.shape
    body = functools.partial(_route_body, n_lat=n_lat, cap_l=cap_l, cap_c=cap_c)
    blk = pl.BlockSpec((1, t, LANES), lambda b: (b, 0, 0))
    return pl.pallas_call(
        body, out_shape=(S((bsz, t, LANES), F32), S((bsz, t, LANES), F32), S((bsz, N_EXPERTS, t), F32)), grid=(bsz,),
        in_specs=[blk], out_specs=(blk, blk, pl.BlockSpec((1, N_EXPERTS, t), lambda b: (b, 0, 0))),
        compiler_params=_cp(("parallel",)), name="moe_route",
    )(logits)


def _gather_body(codet_ref, h_ref, xl_ref, xc_ref, *, n_lat, n_ctx, cap_l, cap_c, grp):
    hl = h_ref[0, 0:n_lat, :]
    hc = h_ref[0, n_lat:, :]
    il = _iota((cap_l, n_lat), 0).astype(F32)
    ic = _iota((cap_c, n_ctx), 0).astype(F32)
    for e0 in range(0, N_EXPERTS, grp):
        onehot = jnp.concatenate([jnp.where(il == codet_ref[0, e:e + 1, 0:n_lat], 1.0, 0.0).astype(BF16)
                                  for e in range(e0, e0 + grp)], axis=0)
        xe = jnp.dot(onehot, hl, preferred_element_type=F32).astype(BF16)
        for r in range(grp):
            xl_ref[0, e0 + r] = xe[r * cap_l:(r + 1) * cap_l]
    onehot = jnp.concatenate([jnp.where(ic == codet_ref[0, e:e + 1, n_lat:], 1.0, 0.0).astype(BF16)
                              for e in range(N_EXPERTS)], axis=0)
    xc_ref[0] = jnp.dot(onehot, hc, preferred_element_type=F32).astype(BF16)


def _gather(codet, h, n_lat, cap_l, cap_c):
    bsz, t, d = h.shape
    body = functools.partial(_gather_body, n_lat=n_lat, n_ctx=t - n_lat, cap_l=cap_l, cap_c=cap_c, grp=4)
    return pl.pallas_call(
        body, out_shape=(S((bsz, N_EXPERTS, cap_l, d), BF16), S((bsz, N_EXPERTS * cap_c, d), BF16)), grid=(bsz,),
        in_specs=[pl.BlockSpec((1, N_EXPERTS, t), lambda b: (b, 0, 0)), pl.BlockSpec((1, t, d), lambda b: (b, 0, 0))],
        out_specs=(pl.BlockSpec((1, N_EXPERTS, cap_l, d), lambda b: (b, 0, 0, 0)),
                   pl.BlockSpec((1, N_EXPERTS * cap_c, d), lambda b: (b, 0, 0))),
        compiler_params=_cp(("parallel",)), name="moe_gather",
    )(codet, h)


def _ffn_body(xl_ref, xc_ref, wg_ref, wu_ref, wd_ref, yl_ref, yc_ref, x_scr, acc, *, nb, cap_l, cap_c):
    f = pl.program_id(1)
    d = x_scr.shape[1]

    @pl.when(f == 0)
    def _():
        x_scr[0:nb * cap_l, :] = xl_ref[:, 0].reshape(nb * cap_l, d)
        x_scr[nb * cap_l:, :] = xc_ref[:, 0].reshape(nb * cap_c, d)
        acc[...] = jnp.zeros_like(acc)

    x = x_scr[...]
    a = jnp.dot(x, wg_ref[0, 0].astype(BF16), preferred_element_type=F32)
    u = jnp.dot(x, wu_ref[0, 0].astype(BF16), preferred_element_type=F32)
    acc[...] += _mm(_silu(a) * u, wd_ref[0, 0])

    @pl.when(f == pl.num_programs(1) - 1)
    def _():
        y = acc[...].astype(BF16)
        yl_ref[:, 0] = y[0:nb * cap_l].reshape(nb, cap_l, d)
        yc_ref[:, 0] = y[nb * cap_l:].reshape(nb, cap_c, d)


def _expert_ffn(xl, xc, w_gate, w_up, w_down, layer):
    bsz, ne, cap_l, d = xl.shape
    cap_c = xc.shape[2]
    ffd = w_gate.shape[-1]
    tf = 256
    rows = bsz * (cap_l + cap_c)
    body = functools.partial(_ffn_body, nb=bsz, cap_l=cap_l, cap_c=cap_c)
    return pl.pallas_call(
        body, out_shape=(S(xl.shape, BF16), S(xc.shape, BF16)), grid=(ne, ffd // tf),
        in_specs=[pl.BlockSpec((bsz, 1, cap_l, d), lambda e, f: (0, e, 0, 0)),
                  pl.BlockSpec((bsz, 1, cap_c, d), lambda e, f: (0, e, 0, 0)),
                  pl.BlockSpec((1, 1, d, tf), lambda e, f: (layer, e, 0, f)),
                  pl.BlockSpec((1, 1, d, tf), lambda e, f: (layer, e, 0, f)),
                  pl.BlockSpec((1, 1, tf, d), lambda e, f: (layer, e, f, 0))],
        out_specs=(pl.BlockSpec((bsz, 1, cap_l, d), lambda e, f: (0, e, 0, 0)),
                   pl.BlockSpec((bsz, 1, cap_c, d), lambda e, f: (0, e, 0, 0))),
        scratch_shapes=[pltpu.VMEM((rows, d), BF16), pltpu.VMEM((rows, d), F32)],
        compiler_params=_cp(("parallel", "arbitrary")), name="moe_expert_ffn",
    )(xl, xc, w_gate, w_up, w_down)


def _combine_body(code_ref, aff_ref, yl_ref, yc_ref, res_ref, gate_ref, fw_ref, o_ref,
                  *, n_lat, n_b, cap_l, cap_c, tm, final):
    b, i = pl.program_id(0), pl.program_id(1)
    code = code_ref[0]
    aff = aff_ref[0]

    def scatter(cap, y):
        slot = _iota((tm, cap), 1).astype(F32)
        q = jnp.concatenate([jnp.where(code[:, e:e + 1] == slot, aff[:, e:e + 1], 0.0) for e in range(N_EXPERTS)], axis=1)
        return jnp.dot(q.astype(BF16), y, preferred_element_type=F32)

    @pl.when(i * tm < n_lat)
    def _():
        x = res_ref[0] + gate_ref[pl.ds(b, 1), :] * scatter(cap_l, yl_ref[0])
        if final:
            x = x * lax.rsqrt(jnp.mean(x * x, axis=-1, keepdims=True) + EPS) * fw_ref[...]
        o_ref[0] = x

    @pl.when(i * tm >= n_lat)
    def _():
        o_ref[0] = res_ref[0] + gate_ref[n_b:n_b + 1, :] * scatter(cap_c, yc_ref[0])


def _combine(code, aff, yl, yc, res, mod, n_lat, final_w, final):
    bsz, t, d = res.shape
    cap_l, cap_c = yl.shape[2], yc.shape[2]
    tm = t - n_lat
    assert n_lat % tm == 0
    rows = n_lat if final else t
    body = functools.partial(_combine_body, n_lat=n_lat, n_b=bsz, cap_l=cap_l, cap_c=cap_c, tm=tm, final=final)
    tok = lambda w: pl.BlockSpec((1, tm, w), lambda b, i: (b, i, 0))
    return pl.pallas_call(
        body, out_shape=S((bsz, rows, d), F32), grid=(bsz, rows // tm),
        in_specs=[tok(LANES), tok(LANES),
                  pl.BlockSpec((1, N_EXPERTS * cap_l, d), lambda b, i: (b, 0, 0)),
                  pl.BlockSpec((1, N_EXPERTS * cap_c, d), lambda b, i: (b, 0, 0)),
                  tok(d), pl.BlockSpec((MOD_ROWS, d), lambda b, i: (0, 5)), pl.BlockSpec((1, d), lambda b, i: (0, 0))],
        out_specs=tok(d),
        compiler_params=_cp(("parallel", "parallel")), name="moe_combine",
    )(code, aff, yl.reshape(bsz, N_EXPERTS * cap_l, d), yc.reshape(bsz, N_EXPERTS * cap_c, d), res, mod,
      final_w.reshape(1, d))


def _moe_layer(xs, h, logits, mod, w_gate, w_up, w_down, layer, n_lat, final_w, final):
    bsz, t, d = xs.shape
    n_ctx = t - n_lat
    cap_l = EC_CAPACITY * n_lat // N_EXPERTS
    cap_c = EC_CAPACITY * n_ctx // N_EXPERTS
    aff, code, codet = _route(logits, n_lat, cap_l, cap_c)
    xl, xc = _gather(codet, h, n_lat, cap_l, cap_c)
    yl, yc = _expert_ffn(xl, xc.reshape(bsz, N_EXPERTS, cap_c, d), w_gate, w_up, w_down, layer)
    return _combine(code, aff, yl, yc, xs, mod, n_lat, final_w, final)


def _swa_group_columns():
    hd, rep = SW_HD, SW_REP
    cols = []
    for g in range(SW_HKV):
        cols += list(range(g * rep * hd, (g + 1) * rep * hd))
        cols += list(range(SW_HQ * hd + g * hd, SW_HQ * hd + (g + 1) * hd))
        cols += list(range((SW_HQ + SW_HKV) * hd + g * hd, (SW_HQ + SW_HKV) * hd + (g + 1) * hd))
    return np.asarray(cols, np.int32)


def kernel(x, c, ctx, c_ctx, ada_w, ada_b, norm1_w, norm2_w, final_norm_w, hy_w_in, hy_b_in, hy_short_w, hy_short_b, hy_ffn_w1, hy_ffn_b1, hy_ffn_w2, hy_ffn_b2, hy_ffn_w3, hy_sin_freq, hy_filter_bias, hy_w_out, hy_b_out, sw_w_in, sw_sink, sw_w_out, gd_w_in, gd_conv_w, gd_a_log, gd_dt_bias, gd_norm_w, gd_w_out, hg_w_in, hg_lb, hg_norm_w, hg_w_out, moe_router, moe_w_gate, moe_w_up, moe_w_down):
    bsz, n_lat, d = x.shape
    n_ctx = ctx.shape[1]
    depth = ada_w.shape[0]
    assert bsz < MOD_ROWS and n_lat % n_ctx == 0
    xs = jnp.concatenate([x, ctx], axis=1)
    c16 = jnp.zeros((MOD_ROWS, d), F32).at[:bsz].set(c).at[bsz].set(c_ctx)
    mod = _modulation(c16, ada_w, ada_b)
    zero_bias = jnp.zeros((d,), F32)
    gd_pad = (-gd_w_in.shape[1]) % IN_PROJ_TN
    gd_w = jnp.pad(gd_w_in, ((0, 0), (0, gd_pad))).astype(BF16)
    sw_w = sw_w_in[:, _swa_group_columns()].astype(BF16)
    for layer in range(depth):
        m = mod[layer]
        nw = norm1_w[layer]
        kind = layer % 4
        if kind == 0:
            u = _in_proj(xs, nw, m, 0, 1, hy_w_in.astype(BF16), hy_b_in, n_lat, "hy_in_proj")
            y = _hyena_mixer(u, n_lat, n_ctx, hy_short_w, hy_short_b, hy_ffn_w1, hy_ffn_b1, hy_ffn_w2, hy_ffn_b2,
                             hy_ffn_w3, hy_sin_freq, hy_filter_bias)
            w_out, b_out, name = hy_w_out, hy_b_out, "hy_out_proj"
        elif kind == 1:
            u = _in_proj(xs, nw, m, 0, 1, sw_w, jnp.zeros((sw_w.shape[1],), F32), n_lat, "sw_in_proj")
            y = _swa_mixer(u, n_lat, n_ctx, sw_sink)
            w_out, b_out, name = sw_w_out, zero_bias, "sw_out_proj"
        elif kind == 2:
            u = _in_proj(xs, nw, m, 0, 1, gd_w, jnp.zeros((gd_w.shape[1],), F32), n_lat, "gd_in_proj")
            y = _gdn_mixer(u, n_lat, n_ctx, gd_conv_w, gd_a_log, gd_dt_bias, gd_norm_w)
            w_out, b_out, name = gd_w_out, zero_bias, "gd_out_proj"
        else:
            u = _in_proj(xs, nw, m, 0, 1, hg_w_in.astype(BF16), jnp.zeros((hg_w_in.shape[1],), F32), n_lat, "hg_in_proj")
            y = _hgrn_mixer(u, n_lat, n_ctx, hg_lb, hg_norm_w, layer)
            w_out, b_out, name = hg_w_out, zero_bias, "hg_out_proj"
        router_w = jnp.pad(moe_router[layer], ((0, 0), (0, LANES - N_EXPERTS)))
        router_hi = router_w.astype(BF16)
        router_p = jnp.concatenate([router_hi, (router_w - router_hi.astype(F32)).astype(BF16)], axis=1)
        xs, h, logits = _out_proj(y, w_out.astype(BF16), b_out, xs, m, n_lat, norm2_w[layer], router_p, name)
        xs = _moe_layer(xs, h, logits, m, moe_w_gate, moe_w_up, moe_w_down, layer, n_lat, final_norm_w,
                        layer == depth - 1)
    return xs
```

```python
import functools
import math

import jax
import jax.numpy as jnp
import numpy as np
from jax import lax
from jax.experimental import pallas as pl
from jax.experimental.pallas import tpu as pltpu

F32 = jnp.float32
BF16 = jnp.bfloat16
HIGHEST = lax.Precision.HIGHEST
EPS = 1e-6
NEG_INF = -1e30
LANES = 128
BF16_ROWS = 16
MOD_ROWS = 16

GRID_W = 64
HY_ORDER = 2
HY_EMB = 33
HY_FAST_DECAY = 0.3
HY_SLOW_DECAY = 1.5
HY_TARGET = 1e-2
HY_SHIFT = 0.05
SW_HQ, SW_HKV, SW_HD, SW_WINDOW = 16, 4, 64, 128
SW_REP = SW_HQ // SW_HKV
ROPE_BASE = 10000.0
GD_H, GD_DK = 8, 128
HG_DK = 128
CHUNK = 64
SUPER = 256
GD_SUPER = 128
GD_HEADS_PER_STEP = 2
INTRA_ROWS = 2304
IN_PROJ_TN = 512
N_EXPERTS = 16
EC_CAPACITY = 2

S = jax.ShapeDtypeStruct


def _cp(sem, vmem_mb=48):
    return pltpu.CompilerParams(dimension_semantics=sem, vmem_limit_bytes=vmem_mb * 2**20)


def _iota(shape, dim):
    return lax.broadcasted_iota(jnp.int32, shape, dim)


def _sigmoid(x):
    return 0.5 * jnp.tanh(0.5 * x) + 0.5


def _silu(x):
    return x * _sigmoid(x)


def _softplus(x):
    return jnp.maximum(x, 0.0) + jnp.log(1.0 + jnp.exp(-jnp.abs(x)))


def _mm(a, b):
    return jnp.dot(a.astype(BF16), b.astype(BF16), preferred_element_type=F32)


def _mm_nt(a, b):
    return lax.dot_general(a.astype(BF16), b.astype(BF16), (((1,), (1,)), ((), ())), preferred_element_type=F32)


def _mm_tn(a, b):
    return lax.dot_general(a.astype(BF16), b.astype(BF16), (((0,), (0,)), ((), ())), preferred_element_type=F32)


def _mm_f32(a, b):
    return jnp.dot(a, b, precision=HIGHEST, preferred_element_type=F32)


def _mod_body(c_ref, w_ref, b_ref, o_ref):
    o_ref[0] = _mm_f32(_silu(c_ref[...]), w_ref[0]) + b_ref[0]


def _modulation(c16, ada_w, ada_b):
    depth, d, n = ada_w.shape
    tn = 1024
    return pl.pallas_call(
        _mod_body, out_shape=S((depth, MOD_ROWS, n), F32), grid=(depth, n // tn),
        in_specs=[pl.BlockSpec((MOD_ROWS, d), lambda l, j: (0, 0)),
                  pl.BlockSpec((1, d, tn), lambda l, j: (l, 0, j)),
                  pl.BlockSpec((1, 1, tn), lambda l, j: (l, 0, j))],
        out_specs=pl.BlockSpec((1, MOD_ROWS, tn), lambda l, j: (l, 0, j)),
        compiler_params=_cp(("parallel", "parallel")), name="adaln_mod",
    )(c16, ada_w, ada_b.reshape(depth, 1, n))


def _norm_mod(x, nw, shift_ref, scale_ref, b, row0, n_lat, n_b):
    tm = x.shape[0]
    y = x * lax.rsqrt(jnp.mean(x * x, axis=-1, keepdims=True) + EPS) * nw
    is_ctx = (row0 + _iota((tm, 1), 0)) >= n_lat
    shift = jnp.where(is_ctx, shift_ref[n_b:n_b + 1, :], shift_ref[pl.ds(b, 1), :])
    scale = jnp.where(is_ctx, scale_ref[n_b:n_b + 1, :], scale_ref[pl.ds(b, 1), :])
    return y * (1.0 + scale) + shift


def _row_gate(gate_ref, b, row0, tm, n_lat, n_b):
    is_ctx = (row0 + _iota((tm, 1), 0)) >= n_lat
    return jnp.where(is_ctx, gate_ref[n_b:n_b + 1, :], gate_ref[pl.ds(b, 1), :])


def _in_proj_body(x_ref, nw_ref, sh_ref, sc_ref, w_ref, bias_ref, o_ref, h_scr, *, n_lat, n_b, tm):
    b = pl.program_id(0)

    @pl.when(pl.program_id(2) == 0)
    def _():
        for rows, mrow in ((slice(0, n_lat), pl.ds(b, 1)), (slice(n_lat, tm), pl.ds(n_b, 1))):
            x = x_ref[0, rows, :]
            y = x * lax.rsqrt(jnp.mean(x * x, axis=-1, keepdims=True) + EPS)
            h_scr[rows, :] = (y * (nw_ref[...] * (1.0 + sc_ref[mrow, :])) + sh_ref[mrow, :]).astype(BF16)

    o_ref[0] = (jnp.dot(h_scr[...], w_ref[...], preferred_element_type=F32) + bias_ref[...]).astype(o_ref.dtype)


def _in_proj(xs, norm_w, mod, shift_idx, scale_idx, w, bias, n_lat, name):
    bsz, t, d = xs.shape
    n = w.shape[1]
    tm = t
    tn = IN_PROJ_TN
    assert n % tn == 0
    body = functools.partial(_in_proj_body, n_lat=n_lat, n_b=bsz, tm=tm)
    return pl.pallas_call(
        body, out_shape=S((bsz, t, n), BF16), grid=(bsz, t // tm, n // tn),
        in_specs=[pl.BlockSpec((1, tm, d), lambda b, i, j: (b, i, 0)),
                  pl.BlockSpec((1, d), lambda b, i, j: (0, 0)),
                  pl.BlockSpec((MOD_ROWS, d), lambda b, i, j: (0, shift_idx)),
                  pl.BlockSpec((MOD_ROWS, d), lambda b, i, j: (0, scale_idx)),
                  pl.BlockSpec((d, tn), lambda b, i, j: (0, j)),
                  pl.BlockSpec((1, tn), lambda b, i, j: (0, j))],
        out_specs=pl.BlockSpec((1, tm, tn), lambda b, i, j: (b, i, j)),
        scratch_shapes=[pltpu.VMEM((tm, d), BF16)],
        compiler_params=_cp(("parallel", "parallel", "arbitrary")), name=name,
    )(xs, norm_w.reshape(1, d), mod, mod, w, bias.reshape(1, n))


def _out_proj_body(y_ref, w_ref, bias_ref, res_ref, gate_ref, nw_ref, sh_ref, sc_ref, rw_ref, o_ref, h_ref, lg_ref,
                   *, n_lat, n_b, tm):
    b, i = pl.program_id(0), pl.program_id(1)
    y = jnp.dot(y_ref[0].astype(BF16), w_ref[...], preferred_element_type=F32) + bias_ref[...]
    x = res_ref[0] + _row_gate(gate_ref, b, i * tm, tm, n_lat, n_b) * y
    o_ref[0] = x
    h = _norm_mod(x, nw_ref[...], sh_ref, sc_ref, b, i * tm, n_lat, n_b)
    hi = h.astype(BF16)
    lo = (h - hi.astype(F32)).astype(BF16)
    h_ref[0] = hi
    r = jnp.dot(jnp.concatenate([hi, lo], axis=0), rw_ref[...], preferred_element_type=F32)
    lg_ref[0] = (r[0:tm, 0:LANES] + r[0:tm, LANES:]) + (r[tm:, 0:LANES] + r[tm:, LANES:])


def _out_proj(y, w, bias, res, mod, n_lat, norm2_w, router_p, name):
    bsz, t, dy = y.shape
    d = w.shape[1]
    tm = t // 3 if t % 3 == 0 and (t // 3) % BF16_ROWS == 0 else t
    body = functools.partial(_out_proj_body, n_lat=n_lat, n_b=bsz, tm=tm)
    tok = lambda width: pl.BlockSpec((1, tm, width), lambda b, i: (b, i, 0))
    modc = lambda idx: pl.BlockSpec((MOD_ROWS, d), lambda b, i: (0, idx))
    return pl.pallas_call(
        body, out_shape=(S((bsz, t, d), F32), S((bsz, t, d), BF16), S((bsz, t, LANES), F32)), grid=(bsz, t // tm),
        in_specs=[tok(dy), pl.BlockSpec((dy, d), lambda b, i: (0, 0)), pl.BlockSpec((1, d), lambda b, i: (0, 0)),
                  tok(d), modc(2), pl.BlockSpec((1, d), lambda b, i: (0, 0)), modc(3), modc(4),
                  pl.BlockSpec((d, 2 * LANES), lambda b, i: (0, 0))],
        out_specs=(tok(d), tok(d), tok(LANES)),
        compiler_params=_cp(("parallel", "parallel")), name=name,
    )(y, w, bias.reshape(1, d), res, mod, norm2_w.reshape(1, d), mod, mod, router_p)


def _dft_tables(n):
    half = n // 2
    k = np.arange(half, dtype=np.int64)[:, None]
    m = np.arange(half, dtype=np.int64)[None, :]
    ang_e = ((k * 2 * m) % (2 * n)).astype(np.float64) * (math.pi / n)
    ang_o = ((k * (2 * m + 1)) % (2 * n)).astype(np.float64) * (math.pi / n)
    ce, se, co, so = np.cos(ang_e), np.sin(ang_e), np.cos(ang_o), np.sin(ang_o)
    return jnp.asarray(np.stack([ce, se, co, so, co.T, so.T]), dtype=BF16)


def _hy_positional(n):
    t = np.linspace(0.0, 1.0, n)[:, None]
    bands = (HY_EMB - 1) // 2
    w = (2.0 * math.pi * np.arange(n) / n)[:, None]
    f = np.linspace(1e-4, bands - 1, bands)[None, :]
    z = np.concatenate([t, np.cos(f * w), -np.sin(f * w)], axis=-1)
    zp = np.zeros((n, LANES), np.float32)
    zp[:, :HY_EMB] = z
    order = np.concatenate([np.arange(0, n, 2), np.arange(1, n, 2)])
    return jnp.asarray(zp[order]), jnp.asarray(t.astype(np.float32)[order])


def _hy_mlp_body(z_ref, w1_ref, b1_ref, w2_ref, b2_ref, fr_ref, h_ref):
    h = jnp.sin(fr_ref[0:1, :] * (_mm_f32(z_ref[...], w1_ref[...]) + b1_ref[...]))
    h_ref[...] = jnp.sin(fr_ref[1:2, :] * (_mm_f32(h, w2_ref[...]) + b2_ref[...]))


def _alt_sign(rows):
    return (1 - 2 * (_iota((rows, 1), 0) & 1)).astype(F32)


def _hy_filter_body(h_ref, w3f_ref, w3b_ref, t_ref, dl_ref, tab_ref, pk_ref, pm_ref, *, n):
    half = n // 2
    win = jnp.exp(-t_ref[...] * dl_ref[...]) + HY_SHIFT
    hf = _mm_f32(h_ref[...], w3f_ref[...]) * win
    hb = _mm_f32(h_ref[...], w3b_ref[...]) * win
    hb = jnp.where(_iota((n, 1), 0) == 0, 0.0, hb)
    hs, hd = hf + hb, hf - hb
    alt = _alt_sign(half)
    pm_ref[0] = jnp.concatenate([jnp.sum(alt * hs[0:half], axis=0, keepdims=True),
                                 -jnp.sum(alt * hd[half:], axis=0, keepdims=True),
                                 jnp.zeros((6, hs.shape[1]), F32)], axis=0) * (1.0 / n)
    wgt = jnp.where(_iota((half, 1), 0) == 0, 0.5 / n, 1.0 / n)
    ce, co = _mm(tab_ref[0], hs[0:half]), _mm(tab_ref[2], hs[half:])
    se, so = _mm(tab_ref[1], hd[0:half]), _mm(tab_ref[3], hd[half:])
    pk_ref[0, 0] = (ce + co) * wgt
    pk_ref[0, 1] = (ce - co) * wgt
    pk_ref[0, 2] = -(se + so) * wgt
    pk_ref[0, 3] = (se - so) * wgt


def _hy_filter_spectra(n, w1p, b1, w2, b2, freq, w3, tabs, d, td):
    zpad, tcol = _hy_positional(n)
    max_decay = math.log(HY_TARGET) / HY_FAST_DECAY
    min_decay = math.log(HY_TARGET) / HY_SLOW_DECAY
    deltas = jnp.asarray(np.abs(np.linspace(min_decay, max_decay, d)), dtype=F32)[None, :]
    nd = d // td
    ff = w1p.shape[1]
    half = n // 2
    h = pl.pallas_call(_hy_mlp_body, out_shape=S((n, ff), F32), name=f"hy_filter_mlp_{n}")(
        zpad, w1p, b1.reshape(1, ff), w2, b2.reshape(1, ff), freq)
    body = functools.partial(_hy_filter_body, n=n)
    const = lambda o, j: (0, 0)
    return pl.pallas_call(
        body, out_shape=(S((HY_ORDER, 4, half, d), F32), S((HY_ORDER, 8, d), F32)), grid=(HY_ORDER, nd),
        in_specs=[pl.BlockSpec((n, ff), const),
                  pl.BlockSpec((ff, td), lambda o, j: (0, (2 * o) * nd + j)),
                  pl.BlockSpec((ff, td), lambda o, j: (0, (2 * o + 1) * nd + j)),
                  pl.BlockSpec((n, 1), const), pl.BlockSpec((1, td), lambda o, j: (0, j)),
                  pl.BlockSpec((6, half, half), lambda o, j: (0, 0, 0), pipeline_mode=pl.Buffered(1))],
        out_specs=(pl.BlockSpec((1, 4, half, td), lambda o, j: (o, 0, 0, j)),
                   pl.BlockSpec((1, 8, td), lambda o, j: (o, 0, j))),
        compiler_params=_cp(("parallel", "parallel")), name=f"hy_filter_{n}",
    )(h, w3, w3, tcol, deltas, tabs)


def _short_conv(x, w_ref, b_ref):
    n = x.shape[0]
    row = _iota((n, 1), 0)
    xp = jnp.where(row == 0, 0.0, pltpu.roll(x, 1, 0))
    xn = jnp.where(row == n - 1, 0.0, pltpu.roll(x, n - 1, 0))
    return w_ref[0:1, :] * xp + w_ref[1:2, :] * x + w_ref[2:3, :] * xn + b_ref[...]


def _hy_conv_body(a_ref, g_ref, wa_ref, ba_ref, wg_ref, bg_ref, tab_ref, pk_ref, pm_ref, fb_ref, o_ref,
                  a_s, y_s, *, n, conv_a):
    half = n // 2
    nt = a_s.shape[0]
    a = a_ref[0].astype(F32)
    if conv_a:
        a = _short_conv(a, wa_ref, ba_ref)
    for c in range(nt):
        a_s[c] = a[:, c * LANES:(c + 1) * LANES]
    ve = jnp.concatenate([a_s[c, pl.ds(0, half, stride=2), :] for c in range(nt)], axis=1)
    vo = jnp.concatenate([a_s[c, pl.ds(1, half, stride=2), :] for c in range(nt)], axis=1)
    ec, es = _mm(tab_ref[0], ve), _mm(tab_ref[1], ve)
    oc, os_ = _mm(tab_ref[2], vo), _mm(tab_ref[3], vo)
    alt = _alt_sign(half)
    xr_mid = jnp.sum(alt * ve, axis=0, keepdims=True)
    xi_mid = -jnp.sum(alt * vo, axis=0, keepdims=True)
    kr_lo, kr_hi, ki_lo, ki_hi = pk_ref[0, 0], pk_ref[0, 1], pk_ref[0, 2], pk_ref[0, 3]
    xr_lo, xr_hi, xi_lo, xi_hi = ec + oc, ec - oc, -(es + os_), es - os_
    yr_lo = xr_lo * kr_lo - xi_lo * ki_lo
    yi_lo = xr_lo * ki_lo + xi_lo * kr_lo
    yr_hi = xr_hi * kr_hi - xi_hi * ki_hi
    yi_hi = xr_hi * ki_hi + xi_hi * kr_hi
    kr_mid, ki_mid = pm_ref[0, 0:1, :], pm_ref[0, 1:2, :]
    yr_mid = xr_mid * kr_mid - xi_mid * ki_mid
    yi_mid = xr_mid * ki_mid + xi_mid * kr_mid
    y_even = _mm(tab_ref[0], yr_lo + yr_hi) - _mm(tab_ref[1], yi_lo - yi_hi) + alt * yr_mid
    y_odd = _mm(tab_ref[4], yr_lo - yr_hi) - _mm(tab_ref[5], yi_lo + yi_hi) - alt * yi_mid
    for c in range(nt):
        y_s[c, pl.ds(0, half, stride=2), :] = y_even[:, c * LANES:(c + 1) * LANES]
        y_s[c, pl.ds(1, half, stride=2), :] = y_odd[:, c * LANES:(c + 1) * LANES]
    y = jnp.concatenate([y_s[c] for c in range(nt)], axis=1)
    g = _short_conv(g_ref[0].astype(F32), wg_ref, bg_ref)
    o_ref[0] = g * (y + a * fb_ref[0])


def _hy_conv(a, a_col0, a_row, g, g_col0, g_row, n, conv_w, conv_b, tabs, pk, pm, fbias, order, conv_a, d, td, name):
    bsz = a.shape[0]
    nd = d // td
    half = n // 2
    body = functools.partial(_hy_conv_body, n=n, conv_a=conv_a)
    cw = conv_w
    cb = conv_b.reshape(1, -1)
    return pl.pallas_call(
        body, out_shape=S((bsz, n, d), F32), grid=(nd, bsz),
        in_specs=[pl.BlockSpec((1, n, td), lambda j, b: (b, a_row, a_col0 * nd + j)),
                  pl.BlockSpec((1, n, td), lambda j, b: (b, g_row, g_col0 * nd + j)),
                  pl.BlockSpec((3, td), lambda j, b: (0, a_col0 * nd + j if conv_a else j)),
                  pl.BlockSpec((1, td), lambda j, b: (0, a_col0 * nd + j if conv_a else j)),
                  pl.BlockSpec((3, td), lambda j, b: (0, g_col0 * nd + j)),
                  pl.BlockSpec((1, td), lambda j, b: (0, g_col0 * nd + j)),
                  pl.BlockSpec((6, half, half), lambda j, b: (0, 0, 0), pipeline_mode=pl.Buffered(1)),
                  pl.BlockSpec((1, 4, half, td), lambda j, b: (order, 0, 0, j), pipeline_mode=pl.Buffered(1)),
                  pl.BlockSpec((1, 8, td), lambda j, b: (order, 0, j)),
                  pl.BlockSpec((1, 1, td), lambda j, b: (order, 0, j))],
        out_specs=pl.BlockSpec((1, n, td), lambda j, b: (b, 0, j)),
        scratch_shapes=[pltpu.VMEM((td // LANES, n, LANES), F32), pltpu.VMEM((td // LANES, n, LANES), F32)],
        compiler_params=_cp(("parallel", "parallel"), 56), name=name,
    )(a, g, cw, cb, cw, cb, tabs, pk, pm, fbias.reshape(HY_ORDER, 1, d))


def _hyena_mixer(u, n_lat, n_ctx, short_w, short_b, w1, b1, w2, b2, w3, freq, fbias):
    d = u.shape[2] // 3
    ff = w1.shape[1]
    w1p = jnp.zeros((LANES, ff), F32).at[:w1.shape[0]].set(w1)
    outs = []
    for n, row in ((n_lat, 0), (n_ctx, n_lat // n_ctx)):
        td = 256 if n > 512 else 512
        tabs = _dft_tables(n)
        pk, pm = _hy_filter_spectra(n, w1p, b1, w2, b2, freq, w3, tabs, d, td)
        z1 = _hy_conv(u, 0, row, u, 1, row, n, short_w, short_b, tabs, pk, pm, fbias, 0, True, d, td, f"hy_conv1_{n}")
        z2 = _hy_conv(z1, 0, 0, u, 2, row, n, short_w, short_b, tabs, pk, pm, fbias, 1, False, d, td, f"hy_conv2_{n}")
        outs.append(z2)
    return jnp.concatenate(outs, axis=1)


def _rope_tables(n_lat, width, rot_heads):
    hd = SW_HD
    rows = n_lat // GRID_W
    row = np.repeat(np.arange(rows, dtype=np.float64), GRID_W)
    col = np.tile(np.arange(GRID_W, dtype=np.float64), rows)
    nf = hd // 4
    inv = ROPE_BASE ** (-np.arange(nf, dtype=np.float64) / nf)
    ang = np.concatenate([row[:, None] * inv, col[:, None] * inv], axis=-1)
    cos, sin = np.cos(ang), np.sin(ang)
    zero = np.zeros_like(sin)
    c = np.ones((n_lat, width), np.float32)
    sa = np.zeros((n_lat, width), np.float32)
    sb = np.zeros((n_lat, width), np.float32)
    for h in range(rot_heads):
        c[:, h * hd:(h + 1) * hd] = np.concatenate([cos, cos], axis=-1)
        sa[:, h * hd:(h + 1) * hd] = np.concatenate([-sin, zero], axis=-1)
        sb[:, h * hd:(h + 1) * hd] = np.concatenate([zero, sin], axis=-1)
    return jnp.asarray(c), jnp.asarray(sa), jnp.asarray(sb)


def _rope(x, c, sa, sb):
    w = x.shape[1]
    half = SW_HD // 2
    return x * c + pltpu.roll(x, w - half, 1) * sa + pltpu.roll(x, half, 1) * sb


def _sink_attend(q, kvs, sink):
    ss = []
    m = None
    for k, _, mask in kvs:
        s = _mm_nt(q, k)
        if mask is not None:
            s = jnp.where(mask, s, NEG_INF)
        ss.append(s)
        sm = jnp.max(s, axis=-1, keepdims=True)
        m = sm if m is None else jnp.maximum(m, sm)
    m = jnp.maximum(m, sink)
    den = jnp.exp(sink - m)
    o = None
    for s, (_, v, _) in zip(ss, kvs):
        p = jnp.exp(s - m)
        den = den + jnp.sum(p, axis=-1, keepdims=True)
        pv = _mm(p, v)
        o = pv if o is None else o + pv
    return o / den


def _swa_body(u_ref, cq_ref, saq_ref, sbq_ref, ck_ref, sak_ref, sbk_ref, sink_ref, o_ref, kv_scr, s_a, s_b,
              *, n_lat, n_ctx, blk):
    g = pl.program_id(1)
    hd, rep = SW_HD, SW_REP
    qw = rep * hd
    scale = hd ** -0.5
    span = 3 * blk
    nb = n_lat // blk
    kv_scr[0:n_lat, :] = _rope(u_ref[0, 0:n_lat, qw:qw + 2 * hd].astype(F32), ck_ref[...], sak_ref[...],
                               sbk_ref[...]).astype(BF16)
    kv_scr[n_lat:, :] = u_ref[0, n_lat:, qw:qw + 2 * hd]
    kc = kv_scr[n_lat:, 0:hd]
    vc = kv_scr[n_lat:, hd:2 * hd]

    def key_start(i):
        return pl.multiple_of(jnp.clip((i - 1) * blk, 0, n_lat - span), blk)

    def scores(i, s_ref):
        rows = pl.ds(pl.multiple_of(i * blk, blk), blk)
        q = _rope(u_ref[0, rows, 0:qw].astype(F32), cq_ref[rows, :], saq_ref[rows, :], sbq_ref[rows, :]) * scale
        q = q.astype(BF16)
        kl = kv_scr[pl.ds(key_start(i), span), 0:hd]
        for r in range(rep):
            s_ref[r, :, 0:n_ctx] = _mm_nt(q[:, r * hd:(r + 1) * hd], kc)
            s_ref[r, :, n_ctx:] = _mm_nt(q[:, r * hd:(r + 1) * hd], kl)

    def attend(i, s_ref):
        r0 = pl.multiple_of(i * blk, blk)
        ks = key_start(i)
        vl = kv_scr[pl.ds(ks, span), hd:2 * hd]
        mask = jnp.abs((r0 + _iota((blk, span), 0)) - (ks + _iota((blk, span), 1))) <= SW_WINDOW
        for r in range(rep):
            sink = sink_ref[g * rep + r]
            sc = s_ref[r, :, 0:n_ctx]
            sl = jnp.where(mask, s_ref[r, :, n_ctx:], NEG_INF)
            m = jnp.maximum(jnp.maximum(jnp.max(sc, axis=-1, keepdims=True), jnp.max(sl, axis=-1, keepdims=True)), sink)
            pc = jnp.exp(sc - m)
            pl_ = jnp.exp(sl - m)
            den = jnp.exp(sink - m) + jnp.sum(pc, axis=-1, keepdims=True) + jnp.sum(pl_, axis=-1, keepdims=True)
            o_ref[0, pl.ds(r0, blk), r * hd:(r + 1) * hd] = (_mm(pc, vc) + _mm(pl_, vl)) / den

    scores(0, s_a)

    def pair(j, carry):
        scores(2 * j + 1, s_b)
        attend(2 * j, s_a)
        scores(jnp.minimum(2 * j + 2, nb - 1), s_a)
        attend(2 * j + 1, s_b)
        return carry

    lax.fori_loop(0, nb // 2, pair, 0)
    qc = (u_ref[0, n_lat:, 0:qw].astype(F32) * scale).astype(BF16)
    for r in range(rep):
        o_ref[0, n_lat:, r * hd:(r + 1) * hd] = _sink_attend(qc[:, r * hd:(r + 1) * hd], [(kc, vc, None)], sink_ref[g * rep + r])


def _swa_mixer(u, n_lat, n_ctx, sink):
    bsz, t, _ = u.shape
    hd, rep = SW_HD, SW_REP
    gw = rep * hd + 2 * hd
    blk = SW_WINDOW
    assert n_lat % (2 * blk) == 0 and n_lat >= 3 * blk and n_lat % GRID_W == 0
    cq, saq, sbq = _rope_tables(n_lat, rep * hd, rep)
    ck, sak, sbk = _rope_tables(n_lat, 2 * hd, 1)
    body = functools.partial(_swa_body, n_lat=n_lat, n_ctx=n_ctx, blk=blk)
    tab = lambda w: pl.BlockSpec((n_lat, w), lambda b, g: (0, 0))
    return pl.pallas_call(
        body, out_shape=S((bsz, t, SW_HQ * hd), F32), grid=(bsz, SW_HKV),
        in_specs=[pl.BlockSpec((1, t, gw), lambda b, g: (b, 0, g)),
                  tab(rep * hd), tab(rep * hd), tab(rep * hd), tab(2 * hd), tab(2 * hd), tab(2 * hd),
                  pl.BlockSpec(memory_space=pltpu.SMEM)],
        out_specs=pl.BlockSpec((1, t, rep * hd), lambda b, g: (b, 0, g)),
        scratch_shapes=[pltpu.VMEM((t, 2 * hd), BF16), pltpu.VMEM((rep, blk, n_ctx + 3 * blk), F32),
                        pltpu.VMEM((rep, blk, n_ctx + 3 * blk), F32)],
        compiler_params=_cp(("parallel", "parallel")), name="swa_attention",
    )(u, cq, saq, sbq, ck, sak, sbk, sink)


def _seq_conv(x, w_ref, n_lat):
    t = x.shape[0]
    row = _iota((t, 1), 0)
    first = (row == 0) | (row == n_lat)
    last = (row == n_lat - 1) | (row == t - 1)
    xp = jnp.where(first, 0.0, pltpu.roll(x, 1, 0))
    xn = jnp.where(last, 0.0, pltpu.roll(x, t - 1, 0))
    return w_ref[0:1, :] * xp + w_ref[1:2, :] * x + w_ref[2:3, :] * xn


def _chunk_scan(x, reverse):
    t = x.shape[0]
    pos = _iota((t, 1), 0) & (CHUNK - 1)
    s = 1
    while s < CHUNK:
        if reverse:
            x = x + jnp.where(pos < CHUNK - s, pltpu.roll(x, t - s, 0), 0.0)
        else:
            x = x + jnp.where(pos >= s, pltpu.roll(x, s, 0), 0.0)
        s *= 2
    return x


def _chunk_scan_lanes(x, reverse):
    t = x.shape[1]
    pos = _iota((1, t), 1) & (CHUNK - 1)
    s = 1
    while s < CHUNK:
        if reverse:
            x = x + jnp.where(pos < CHUNK - s, pltpu.roll(x, t - s, 1), 0.0)
        else:
            x = x + jnp.where(pos >= s, pltpu.roll(x, s, 1), 0.0)
        s *= 2
    return x


def _chunk_order(s, n_lat_chunks, n_ctx_chunks, direction):
    if direction == 0:
        return jnp.where(s < n_ctx_chunks, n_lat_chunks + s, s - n_ctx_chunks)
    return n_lat_chunks + n_ctx_chunks - 1 - s


def _intra_unroll(n_super, sup=SUPER):
    return next(u for u in range(INTRA_ROWS // sup, 0, -1) if n_super % u == 0)


def _row_sumsq(x):
    return jnp.dot((x * x).astype(BF16), jnp.ones((LANES, LANES), BF16), preferred_element_type=F32)


def _gated_rms(o, nw, z):
    return o * lax.rsqrt(_row_sumsq(o) * (1.0 / LANES) + EPS) * nw * _silu(z)


def _bmm(a, b):
    return jnp.einsum('bij,bjk->bik', a.astype(BF16), b.astype(BF16), preferred_element_type=F32)


def _bmm_nt(a, b):
    return jnp.einsum('bik,bjk->bij', a.astype(BF16), b.astype(BF16), preferred_element_type=F32)


def _unit_tri_inverse(a, ii, jj):
    eye = (ii == jj).astype(F32)
    a8 = jnp.where((ii >> 3) == (jj >> 3), a, 0.0)
    a8_2 = _bmm(a8, a8)
    a8_4 = _bmm(a8_2, a8_2)
    x = _bmm(_bmm(eye - a8, eye + a8_2), eye + a8_4)
    sh = 3
    while (1 << sh) < CHUNK:
        e = jnp.where(((ii >> (sh + 1)) == (jj >> (sh + 1))) & ((ii >> sh) != (jj >> sh)), a, 0.0)
        x = x - _bmm(_bmm(x, e), x)
        sh += 1
    return x


def _gdn_body(q_ref, k_ref, v_ref, z_ref, ba_ref, cwq_ref, cwk_ref, cwv_ref, par_ref, nw_ref, o_ref,
              kn_s, qn_s, qe_s, kk_s, egl_s, kb_s, kbe_s, vb_s, gc_s, gt_s, qp_s, k2_s, n_s, oacc,
              *, n_lat, n_ctx, hp):
    t = n_lat + n_ctx
    for hh in range(hp):
        _gdn_head_local(hh, pl.program_id(1) * hp + hh, q_ref, k_ref, v_ref, ba_ref, cwq_ref, cwk_ref, cwv_ref, par_ref,
                        kn_s, qn_s, qe_s, kk_s, egl_s, kb_s, kbe_s, vb_s, gc_s, gt_s, qp_s, k2_s, n_s, oacc,
                        n_lat=n_lat, t=t)

    nl, nc = n_lat // CHUNK, n_ctx // CHUNK

    def step(s, states):
        new = []
        for hh in range(hp):
            for d in range(2):
                st = states[2 * hh + d]
                sb = st.astype(BF16)
                c = _chunk_order(s, nl, nc, d)
                rows = pl.ds(pl.multiple_of(c * CHUNK, CHUNK), CHUNK)
                crow = pl.ds(pl.multiple_of(c * GD_DK, GD_DK), GD_DK)
                oacc[hh, rows, :] += jnp.dot(qp_s[hh, d, rows, :], sb, preferred_element_type=F32)
                new.append(st * egl_s[hh, d, pl.ds(c, 1), :] + n_s[hh, d, crow, :]
                           - jnp.dot(k2_s[hh, d, crow, :], sb, preferred_element_type=F32))
        return tuple(new)

    zero = jnp.zeros((GD_DK, LANES), F32)
    lax.fori_loop(0, nl + nc, step, (zero,) * (2 * hp))
    for hh in range(hp):
        cs = slice(hh * LANES, (hh + 1) * LANES)
        o_ref[0, :, cs] = _gated_rms(oacc[hh], nw_ref[...], z_ref[0, :, cs].astype(F32))


def _gdn_head_local(hh, h, q_ref, k_ref, v_ref, ba_ref, cwq_ref, cwk_ref, cwv_ref, par_ref,
                    kn_s, qn_s, qe_s, kk_s, egl_s, kb_s, kbe_s, vb_s, gc_s, gt_s, qp_s, k2_s, n_s, oacc, *, n_lat, t):
    cs = slice(hh * LANES, (hh + 1) * LANES)
    lane = _iota((1, LANES), 1)
    q = _silu(_seq_conv(q_ref[0, :, cs].astype(F32), cwq_ref[:, cs], n_lat))
    k = _silu(_seq_conv(k_ref[0, :, cs].astype(F32), cwk_ref[:, cs], n_lat))
    v = _silu(_seq_conv(v_ref[0, :, cs].astype(F32), cwv_ref[:, cs], n_lat))
    qn = q * lax.rsqrt(_row_sumsq(q) + EPS) * (GD_DK ** -0.5)
    kn = k * lax.rsqrt(_row_sumsq(k) + EPS)
    kn_s[...] = kn.astype(BF16)
    qn_s[...] = qn.astype(BF16)
    par = par_ref[...]

    def pick(x, c):
        return jnp.sum(jnp.where(lane == c, x, 0.0), axis=-1, keepdims=True)

    src = _iota((LANES, LANES), 0)
    dst = _iota((LANES, LANES), 1)
    want = jnp.where(dst < 2, 2 * GD_H + dst * GD_H + h, (dst - 2) * GD_H + h)
    onehot = jnp.where((src == want) & (dst < 4), 1.0, 0.0)
    raw_t = jnp.dot(ba_ref[0].astype(F32), onehot, preferred_element_type=F32).T[0:8, :]
    row8 = _iota((8, 1), 0)
    a_log = jnp.where(row8 == 0, pick(par[0:1, :], h), pick(par[1:2, :], h))
    dt_b = jnp.where(row8 == 0, pick(par[2:3, :], h), pick(par[3:4, :], h))
    g_t = -jnp.exp(a_log) * _softplus(raw_t + dt_b)
    beta_t = _sigmoid(raw_t)
    pre = _chunk_scan_lanes(g_t, False)
    suf = _chunk_scan_lanes(g_t, True)
    gcum_t = jnp.where(row8 == 1, suf, pre)
    glast_t = pre + suf - g_t
    gt_s[...] = gcum_t
    up2 = lambda x: pltpu.roll(x, 2, 0)
    packed = jnp.concatenate(
        [jnp.where(row8 < 2, gcum_t, up2(jnp.exp(gcum_t))),
         jnp.where(row8 < 2, jnp.exp(glast_t - gcum_t), jnp.where(row8 < 4, up2(jnp.exp(glast_t)), up2(beta_t))),
         jnp.zeros((LANES - 16, t), F32)], axis=0)
    cols = packed.T
    gc_s[...] = cols
    for d in range(2):
        beta = cols[:, 12 + d:13 + d]
        eg = jnp.broadcast_to(cols[:, 2 + d:3 + d], (t, LANES))
        kb = kn * beta
        qe_s[d] = (qn * eg).astype(BF16)
        kk_s[d] = (kn * cols[:, 8 + d:9 + d]).astype(BF16)
        ends = gc_s[pl.ds(0, t // CHUNK, stride=CHUNK), :]
        egl_s[hh, d] = jnp.broadcast_to(ends[:, 10 + d:11 + d], (t // CHUNK, LANES))
        kb_s[d] = kb.astype(BF16)
        kbe_s[d] = (kb * eg).astype(BF16)
        vb_s[d] = (v * beta).astype(BF16)

    sup = GD_SUPER
    ii = _iota((sup, sup), 0)
    jj = _iota((sup, sup), 1)
    same = (ii >> 6) == (jj >> 6)
    per = sup // CHUNK
    unroll = _intra_unroll(t // sup, sup)

    def intra(it, carry):
        r0s = [pl.multiple_of((it * unroll + kq) * sup, sup) for kq in range(unroll)]
        kcs = [kn_s[pl.ds(r0, sup), :] for r0 in r0s]
        pairs = [(kq, d) for kq in range(unroll) for d in range(2)]
        decs = []
        for kq, d in pairs:
            rows = pl.ds(r0s[kq], sup)
            incl = same & ((jj >= ii) if d else (jj <= ii))
            dif = jnp.broadcast_to(gc_s[rows, d:d + 1], (sup, sup)) - gt_s[d:d + 1, rows]
            decs.append(jnp.where(incl, jnp.exp(jnp.where(incl, dif, 0.0)), 0.0))
        dec = jnp.stack(decs)
        kc2 = jnp.stack([kcs[kq] for kq, _ in pairs])
        kb = jnp.stack([kb_s[d, pl.ds(r0s[kq], sup), :] for kq, d in pairs])
        rhs = jnp.stack([jnp.concatenate([vb_s[d, pl.ds(r0s[kq], sup), :], kbe_s[d, pl.ds(r0s[kq], sup), :]], axis=1)
                         for kq, d in pairs])
        a = jnp.where(ii == jj, 0.0, _bmm_nt(kb, kc2) * dec)
        uw = _bmm(_unit_tri_inverse(a, ii, jj), rhs).astype(BF16)
        qk = _bmm_nt(jnp.stack([qn_s[pl.ds(r0, sup), :] for r0 in r0s]), jnp.stack(kcs))
        auw = _bmm(jnp.stack([qk[kq] for kq, _ in pairs]) * dec, uw)
        for i, (kq, d) in enumerate(pairs):
            rows = pl.ds(r0s[kq], sup)
            qp_s[hh, d, rows, :] = (qe_s[d, rows, :].astype(F32) - auw[i, :, LANES:]).astype(BF16)
            if d == 1:
                oacc[hh, rows, :] = auw[i - 1, :, 0:LANES] + auw[i, :, 0:LANES]
            for c4 in range(per):
                kuw = _mm_tn(kk_s[d, pl.ds(r0s[kq] + c4 * CHUNK, CHUNK), :], uw[i, c4 * CHUNK:(c4 + 1) * CHUNK])
                crow = pl.ds(pl.multiple_of(((it * unroll + kq) * per + c4) * GD_DK, GD_DK), GD_DK)
                n_s[hh, d, crow, :] = kuw[:, 0:LANES]
                k2_s[hh, d, crow, :] = kuw[:, LANES:].astype(BF16)
        return carry

    lax.fori_loop(0, t // sup // unroll, intra, 0)


def _gdn_mixer(u, n_lat, n_ctx, conv_w, a_log, dt_bias, norm_w):
    bsz, t, _ = u.shape
    assert t % SUPER == 0 and n_lat % SUPER == 0
    par = jnp.zeros((8, LANES), F32).at[0:2, :GD_H].set(a_log).at[2:4, :GD_H].set(dt_bias)
    hp = GD_HEADS_PER_STEP
    body = functools.partial(_gdn_body, n_lat=n_lat, n_ctx=n_ctx, hp=hp)
    sec = lambda s: pl.BlockSpec((1, t, hp * LANES), lambda b, h: (b, 0, s * (GD_H // hp) + h))
    cw = lambda s: pl.BlockSpec((3, hp * LANES), lambda b, h: (0, s * (GD_H // hp) + h))
    both = lambda dt: pltpu.VMEM((2, t, LANES), dt)
    state_rows = (t // CHUNK) * GD_DK
    return pl.pallas_call(
        body, out_shape=S((bsz, t, GD_H * LANES), F32), grid=(bsz, GD_H // hp),
        in_specs=[sec(0), sec(1), sec(2), sec(3),
                  pl.BlockSpec((1, t, LANES), lambda b, h: (b, 0, 4 * GD_H)),
                  cw(0), cw(1), cw(2),
                  pl.BlockSpec((8, LANES), lambda b, h: (0, 0)),
                  pl.BlockSpec((1, LANES), lambda b, h: (0, 0))],
        out_specs=pl.BlockSpec((1, t, hp * LANES), lambda b, h: (b, 0, h)),
        scratch_shapes=[pltpu.VMEM((t, LANES), BF16), pltpu.VMEM((t, LANES), BF16),
                        both(BF16), both(BF16), pltpu.VMEM((hp, 2, t // CHUNK, LANES), F32),
                        both(BF16), both(BF16), both(BF16),
                        pltpu.VMEM((t, LANES), F32),
                        pltpu.VMEM((8, t), F32), pltpu.VMEM((hp, 2, t, LANES), BF16),
                        pltpu.VMEM((hp, 2, state_rows, LANES), BF16),
                        pltpu.VMEM((hp, 2, state_rows, LANES), F32), pltpu.VMEM((hp, t, LANES), F32)],
        compiler_params=_cp(("parallel", "parallel"), 56), name="gdn_mixer",
    )(u, u, u, u, u, conv_w, conv_w, conv_w, par, norm_w.reshape(1, LANES))


def _hgrn_body(q_ref, ff_ref, fb_ref, i_ref, g_ref, lbp_ref, nw_ref, o_ref,
               qe_s, ke_s, kk_s, egl_s, v_s, n_s, st_s, oacc, *, n_lat, n_ctx, layer):
    t = n_lat + n_ctx
    e = jnp.exp(lbp_ref[...] - jnp.max(lbp_ref[...], axis=0, keepdims=True))
    lb = jnp.sum(e[1:layer + 1, :], axis=0, keepdims=True) / jnp.sum(e, axis=0, keepdims=True)
    q = _silu(q_ref[0].astype(F32))
    v_s[...] = i_ref[0]
    for d, f_ref in enumerate((ff_ref, fb_ref)):
        f = f_ref[0].astype(F32)
        sig = _sigmoid(f)
        logf = jnp.log(lb + (1.0 - lb) * sig)
        kin = (1.0 - lb) * (1.0 - sig)
        gc = _chunk_scan(logf, bool(d))
        ends = gc.reshape(t // CHUNK, CHUNK, LANES)[:, 0:1, :] if d else gc.reshape(t // CHUNK, CHUNK, LANES)[:, CHUNK - 1:, :]
        glast = jnp.broadcast_to(ends, (t // CHUNK, CHUNK, LANES)).reshape(t, LANES)
        egl = jnp.exp(glast)
        ke = kin * jnp.exp(-gc)
        qe_s[d] = (q * jnp.exp(gc)).astype(BF16)
        ke_s[d] = ke.astype(BF16)
        kk_s[d] = (ke * egl).astype(BF16)
        egl_s[d] = egl

    ii = _iota((SUPER, SUPER), 0)
    jj = _iota((SUPER, SUPER), 1)
    same = (ii >> 6) == (jj >> 6)
    per = SUPER // CHUNK
    unroll = _intra_unroll(t // SUPER)

    def chunk_rows(sc, c4):
        rows = pl.ds(pl.multiple_of(sc * SUPER + c4 * CHUNK, CHUNK), CHUNK)
        crow = pl.ds(pl.multiple_of((sc * per + c4) * LANES, LANES), LANES)
        return rows, crow

    def intra(it, carry):
        pairs = [(kq, d) for kq in range(unroll) for d in range(2)]
        rows = [pl.ds(pl.multiple_of((it * unroll + kq) * SUPER, SUPER), SUPER) for kq in range(unroll)]
        incl = jnp.stack([same & ((jj >= ii) if d else (jj <= ii)) for _, d in pairs])
        at = jnp.where(incl, _bmm_nt(jnp.stack([qe_s[d, rows[kq], :] for kq, d in pairs]),
                                     jnp.stack([ke_s[d, rows[kq], :] for kq, d in pairs])), 0.0)
        part = _bmm(at, jnp.stack([v_s[rows[kq], :] for kq, _ in pairs]))
        for kq in range(unroll):
            oacc[rows[kq], :] = part[2 * kq] + part[2 * kq + 1]
            for d in range(2):
                for c4 in range(per):
                    r64, crow = chunk_rows(it * unroll + kq, c4)
                    n_s[d, crow, :] = _mm_tn(v_s[r64, :], kk_s[d, r64, :])
        return carry

    lax.fori_loop(0, t // SUPER // unroll, intra, 0)
    nl, nc = n_lat // CHUNK, n_ctx // CHUNK

    def scan(s, states):
        new = []
        for d in range(2):
            c = _chunk_order(s, nl, nc, d)
            crow = pl.ds(pl.multiple_of(c * LANES, LANES), LANES)
            st_s[d, crow, :] = states[d].astype(BF16)
            new.append(states[d] * egl_s[d, pl.ds(c * CHUNK, 1), :] + n_s[d, crow, :])
        return tuple(new)

    zero = jnp.zeros((LANES, HG_DK), F32)
    lax.fori_loop(0, nl + nc, scan, (zero, zero))

    def inter(it, carry):
        for kq in range(unroll):
            sc = it * unroll + kq
            for c4 in range(per):
                r64, crow = chunk_rows(sc, c4)
                oacc[r64, :] += (_mm_nt(qe_s[0, r64, :], st_s[0, crow, :]) + _mm_nt(qe_s[1, r64, :], st_s[1, crow, :]))
        return carry

    lax.fori_loop(0, t // SUPER // unroll, inter, 0)
    o_ref[0] = _gated_rms(oacc[...], nw_ref[...], g_ref[0].astype(F32))


def _hgrn_mixer(u, n_lat, n_ctx, hg_lb, norm_w, layer):
    bsz, t, n5 = u.shape
    d = n5 // 5
    nh = d // HG_DK
    depth = hg_lb.shape[0]
    assert t % SUPER == 0 and n_lat % SUPER == 0
    body = functools.partial(_hgrn_body, n_lat=n_lat, n_ctx=n_ctx, layer=layer)
    sec = lambda s: pl.BlockSpec((1, t, LANES), lambda b, h: (b, 0, s * nh + h))
    both = lambda dt: pltpu.VMEM((2, t, LANES), dt)
    state_rows = (t // CHUNK) * LANES
    return pl.pallas_call(
        body, out_shape=S((bsz, t, d), F32), grid=(bsz, nh),
        in_specs=[sec(0), sec(1), sec(2), sec(3), sec(4),
                  pl.BlockSpec((depth, LANES), lambda b, h: (0, h)),
                  pl.BlockSpec((1, LANES), lambda b, h: (0, 0))],
        out_specs=pl.BlockSpec((1, t, LANES), lambda b, h: (b, 0, h)),
        scratch_shapes=[both(BF16), both(BF16), both(BF16), both(F32), pltpu.VMEM((t, LANES), BF16),
                        pltpu.VMEM((2, state_rows, LANES), F32), pltpu.VMEM((2, state_rows, LANES), BF16),
                        pltpu.VMEM((t, LANES), F32)],
        compiler_params=_cp(("parallel", "parallel")), name="hgrn2_mixer",
    )(u, u, u, u, u, hg_lb, norm_w.reshape(1, LANES))


def _excl_count_lanes(x):
    n = x.shape[1]
    blk = min(n, 256)
    tri = jnp.where(_iota((blk, blk), 0) < _iota((blk, blk), 1), 1.0, 0.0).astype(BF16)
    run = jnp.zeros((x.shape[0], 1), F32)
    outs = []
    for r in range(n // blk):
        xb = x[:, r * blk:(r + 1) * blk]
        outs.append(jnp.dot(xb.astype(BF16), tri, preferred_element_type=F32) + run)
        run = run + jnp.sum(xb, axis=1, keepdims=True)
    return jnp.concatenate(outs, axis=1) if len(outs) > 1 else outs[0]


def _topcap_slots(groups):
    def count_ge(v, thr):
        return jnp.sum(jnp.where(v >= thr, 1.0, 0.0), axis=1, keepdims=True)

    def bisect(_, carry):
        new = []
        for (v, cap), (lo, hi) in zip(groups, carry):
            mid = jnp.sqrt(jnp.maximum(lo, 1e-37)) * jnp.sqrt(hi)
            ok = count_ge(v, mid) >= float(cap)
            new.append((jnp.where(ok, mid, lo), jnp.where(ok, hi, mid)))
        return tuple(new)

    init = tuple((jnp.zeros((v.shape[0], 1), F32), jnp.full((v.shape[0], 1), 2.0, F32)) for v, _ in groups)
    brackets = lax.fori_loop(0, 34, bisect, init)
    codes = []
    for (v, cap), (lo, hi) in zip(groups, brackets):
        thr, found, upper = lo, jnp.zeros_like(lo), hi
        for _ in range(4):
            m = jnp.max(jnp.where(v < upper, v, -1.0), axis=1, keepdims=True)
            ok = jnp.where(count_ge(v, m) >= float(cap), 1.0, 0.0) * (1.0 - found)
            thr = jnp.where(ok > 0, m, thr)
            found = jnp.maximum(found, ok)
            upper = jnp.where(found > 0, upper, m)
        gt = jnp.where(v > thr, 1.0, 0.0)
        eq = jnp.where(v == thr, 1.0, 0.0)
        need = float(cap) - jnp.sum(gt, axis=1, keepdims=True)
        sel = jnp.maximum(gt, eq * jnp.where(_excl_count_lanes(eq) < need, 1.0, 0.0))
        codes.append(jnp.where(sel > 0, _excl_count_lanes(sel), -1.0))
    return codes


def _route_body(lg_ref, aff_ref, code_ref, codet_ref, *, n_lat, cap_l, cap_c):
    t = lg_ref.shape[1]
    lane_ok = _iota((1, LANES), 1) < N_EXPERTS
    lg = jnp.where(lane_ok, lg_ref[0], NEG_INF)
    e = jnp.exp(lg - jnp.max(lg, axis=-1, keepdims=True))
    aff = e / jnp.sum(e, axis=-1, keepdims=True)
    aff_ref[0] = aff
    aff_t = aff.T[0:N_EXPERTS, :]
    code_l, code_c = _topcap_slots([(aff_t[:, 0:n_lat], cap_l), (aff_t[:, n_lat:], cap_c)])
    code_t = jnp.concatenate([code_l, code_c], axis=1)
    codet_ref[0] = code_t
    code_ref[0] = jnp.concatenate([code_t, jnp.full((LANES - N_EXPERTS, t), -1.0, F32)], axis=0).T


def _route(logits, n_lat, cap_l, cap_c):
    bsz, t, _ = logits.shape
    body = functools.partial(_route_body, n_lat=n_lat, cap_l=cap_l, cap_c=cap_c)
    blk = pl.BlockSpec((1, t, LANES), lambda b: (b, 0, 0))
    return pl.pallas_call(
        body, out_shape=(S((bsz, t, LANES), F32), S((bsz, t, LANES), F32), S((bsz, N_EXPERTS, t), F32)), grid=(bsz,),
        in_specs=[blk], out_specs=(blk, blk, pl.BlockSpec((1, N_EXPERTS, t), lambda b: (b, 0, 0))),
        compiler_params=_cp(("parallel",)), name="moe_route",
    )(logits)


def _gather_body(codet_ref, h_ref, xl_ref, xc_ref, *, n_lat, n_ctx, cap_l, cap_c, grp):
    hl = h_ref[0, 0:n_lat, :]
    hc = h_ref[0, n_lat:, :]
    il = _iota((cap_l, n_lat), 0).astype(F32)
    ic = _iota((cap_c, n_ctx), 0).astype(F32)
    for e0 in range(0, N_EXPERTS, grp):
        onehot = jnp.concatenate([jnp.where(il == codet_ref[0, e:e + 1, 0:n_lat], 1.0, 0.0).astype(BF16)
                                  for e in range(e0, e0 + grp)], axis=0)
        xe = jnp.dot(onehot, hl, preferred_element_type=F32).astype(BF16)
        for r in range(grp):
            xl_ref[0, e0 + r] = xe[r * cap_l:(r + 1) * cap_l]
    onehot = jnp.concatenate([jnp.where(ic == codet_ref[0, e:e + 1, n_lat:], 1.0, 0.0).astype(BF16)
                              for e in range(N_EXPERTS)], axis=0)
    xc_ref[0] = jnp.dot(onehot, hc, preferred_element_type=F32).astype(BF16)


def _gather(codet, h, n_lat, cap_l, cap_c):
    bsz, t, d = h.shape
    body = functools.partial(_gather_body, n_lat=n_lat, n_ctx=t - n_lat, cap_l=cap_l, cap_c=cap_c, grp=4)
    return pl.pallas_call(
        body, out_shape=(S((bsz, N_EXPERTS, cap_l, d), BF16), S((bsz, N_EXPERTS * cap_c, d), BF16)), grid=(bsz,),
        in_specs=[pl.BlockSpec((1, N_EXPERTS, t), lambda b: (b, 0, 0)), pl.BlockSpec((1, t, d), lambda b: (b, 0, 0))],
        out_specs=(pl.BlockSpec((1, N_EXPERTS, cap_l, d), lambda b: (b, 0, 0, 0)),
                   pl.BlockSpec((1, N_EXPERTS * cap_c, d), lambda b: (b, 0, 0))),
        compiler_params=_cp(("parallel",)), name="moe_gather",
    )(codet, h)


def _ffn_body(xl_ref, xc_ref, wg_ref, wu_ref, wd_ref, yl_ref, yc_ref, x_scr, acc, *, nb, cap_l, cap_c):
    f = pl.program_id(1)
    d = x_scr.shape[1]

    @pl.when(f == 0)
    def _():
        x_scr[0:nb * cap_l, :] = xl_ref[:, 0].reshape(nb * cap_l, d)
        x_scr[nb * cap_l:, :] = xc_ref[:, 0].reshape(nb * cap_c, d)
        acc[...] = jnp.zeros_like(acc)

    x = x_scr[...]
    a = jnp.dot(x, wg_ref[0, 0].astype(BF16), preferred_element_type=F32)
    u = jnp.dot(x, wu_ref[0, 0].astype(BF16), preferred_element_type=F32)
    acc[...] += _mm(_silu(a) * u, wd_ref[0, 0])

    @pl.when(f == pl.num_programs(1) - 1)
    def _():
        y = acc[...].astype(BF16)
        yl_ref[:, 0] = y[0:nb * cap_l].reshape(nb, cap_l, d)
        yc_ref[:, 0] = y[nb * cap_l:].reshape(nb, cap_c, d)


def _expert_ffn(xl, xc, w_gate, w_up, w_down, layer):
    bsz, ne, cap_l, d = xl.shape
    cap_c = xc.shape[2]
    ffd = w_gate.shape[-1]
    tf = 256
    rows = bsz * (cap_l + cap_c)
    body = functools.partial(_ffn_body, nb=bsz, cap_l=cap_l, cap_c=cap_c)
    return pl.pallas_call(
        body, out_shape=(S(xl.shape, BF16), S(xc.shape, BF16)), grid=(ne, ffd // tf),
        in_specs=[pl.BlockSpec((bsz, 1, cap_l, d), lambda e, f: (0, e, 0, 0)),
                  pl.BlockSpec((bsz, 1, cap_c, d), lambda e, f: (0, e, 0, 0)),
                  pl.BlockSpec((1, 1, d, tf), lambda e, f: (layer, e, 0, f)),
                  pl.BlockSpec((1, 1, d, tf), lambda e, f: (layer, e, 0, f)),
                  pl.BlockSpec((1, 1, tf, d), lambda e, f: (layer, e, f, 0))],
        out_specs=(pl.BlockSpec((bsz, 1, cap_l, d), lambda e, f: (0, e, 0, 0)),
                   pl.BlockSpec((bsz, 1, cap_c, d), lambda e, f: (0, e, 0, 0))),
        scratch_shapes=[pltpu.VMEM((rows, d), BF16), pltpu.VMEM((rows, d), F32)],
        compiler_params=_cp(("parallel", "arbitrary")), name="moe_expert_ffn",
    )(xl, xc, w_gate, w_up, w_down)


def _combine_body(code_ref, aff_ref, yl_ref, yc_ref, res_ref, gate_ref, fw_ref, o_ref,
                  *, n_lat, n_b, cap_l, cap_c, tm, final):
    b, i = pl.program_id(0), pl.program_id(1)
    code = code_ref[0]
    aff = aff_ref[0]

    def scatter(cap, y):
        slot = _iota((tm, cap), 1).astype(F32)
        q = jnp.concatenate([jnp.where(code[:, e:e + 1] == slot, aff[:, e:e + 1], 0.0) for e in range(N_EXPERTS)], axis=1)
        return jnp.dot(q.astype(BF16), y, preferred_element_type=F32)

    @pl.when(i * tm < n_lat)
    def _():
        x = res_ref[0] + gate_ref[pl.ds(b, 1), :] * scatter(cap_l, yl_ref[0])
        if final:
            x = x * lax.rsqrt(jnp.mean(x * x, axis=-1, keepdims=True) + EPS) * fw_ref[...]
        o_ref[0] = x

    @pl.when(i * tm >= n_lat)
    def _():
        o_ref[0] = res_ref[0] + gate_ref[n_b:n_b + 1, :] * scatter(cap_c, yc_ref[0])


def _combine(code, aff, yl, yc, res, mod, n_lat, final_w, final):
    bsz, t, d = res.shape
    cap_l, cap_c = yl.shape[2], yc.shape[2]
    tm = t - n_lat
    assert n_lat % tm == 0
    rows = n_lat if final else t
    body = functools.partial(_combine_body, n_lat=n_lat, n_b=bsz, cap_l=cap_l, cap_c=cap_c, tm=tm, final=final)
    tok = lambda w: pl.BlockSpec((1, tm, w), lambda b, i: (b, i, 0))
    return pl.pallas_call(
        body, out_shape=S((bsz, rows, d), F32), grid=(bsz, rows // tm),
        in_specs=[tok(LANES), tok(LANES),
                  pl.BlockSpec((1, N_EXPERTS * cap_l, d), lambda b, i: (b, 0, 0)),
                  pl.BlockSpec((1, N_EXPERTS * cap_c, d), lambda b, i: (b, 0, 0)),
                  tok(d), pl.BlockSpec((MOD_ROWS, d), lambda b, i: (0, 5)), pl.BlockSpec((1, d), lambda b, i: (0, 0))],
        out_specs=tok(d),
        compiler_params=_cp(("parallel", "parallel")), name="moe_combine",
    )(code, aff, yl.reshape(bsz, N_EXPERTS * cap_l, d), yc.reshape(bsz, N_EXPERTS * cap_c, d), res, mod,
      final_w.reshape(1, d))


def _moe_layer(xs, h, logits, mod, w_gate, w_up, w_down, layer, n_lat, final_w, final):
    bsz, t, d = xs.shape
    n_ctx = t - n_lat
    cap_l = EC_CAPACITY * n_lat // N_EXPERTS
    cap_c = EC_CAPACITY * n_ctx // N_EXPERTS
    aff, code, codet = _route(logits, n_lat, cap_l, cap_c)
    xl, xc = _gather(codet, h, n_lat, cap_l, cap_c)
    yl, yc = _expert_ffn(xl, xc.reshape(bsz, N_EXPERTS, cap_c, d), w_gate, w_up, w_down, layer)
    return _combine(code, aff, yl, yc, xs, mod, n_lat, final_w, final)


def _swa_group_columns():
    hd, rep = SW_HD, SW_REP
    cols = []
    for g in range(SW_HKV):
        cols += list(range(g * rep * hd, (g + 1) * rep * hd))
        cols += list(range(SW_HQ * hd + g * hd, SW_HQ * hd + (g + 1) * hd))
        cols += list(range((SW_HQ + SW_HKV) * hd + g * hd, (SW_HQ + SW_HKV) * hd + (g + 1) * hd))
    return np.asarray(cols, np.int32)


def kernel(x, c, ctx, c_ctx, ada_w, ada_b, norm1_w, norm2_w, final_norm_w, hy_w_in, hy_b_in, hy_short_w, hy_short_b, hy_ffn_w1, hy_ffn_b1, hy_ffn_w2, hy_ffn_b2, hy_ffn_w3, hy_sin_freq, hy_filter_bias, hy_w_out, hy_b_out, sw_w_in, sw_sink, sw_w_out, gd_w_in, gd_conv_w, gd_a_log, gd_dt_bias, gd_norm_w, gd_w_out, hg_w_in, hg_lb, hg_norm_w, hg_w_out, moe_router, moe_w_gate, moe_w_up, moe_w_down):
    bsz, n_lat, d = x.shape
    n_ctx = ctx.shape[1]
    depth = ada_w.shape[0]
    assert bsz < MOD_ROWS and n_lat % n_ctx == 0
    xs = jnp.concatenate([x, ctx], axis=1)
    c16 = jnp.zeros((MOD_ROWS, d), F32).at[:bsz].set(c).at[bsz].set(c_ctx)
    mod = _modulation(c16, ada_w, ada_b)
    zero_bias = jnp.zeros((d,), F32)
    gd_pad = (-gd_w_in.shape[1]) % IN_PROJ_TN
    gd_w = jnp.pad(gd_w_in, ((0, 0), (0, gd_pad))).astype(BF16)
    sw_w = sw_w_in[:, _swa_group_columns()].astype(BF16)
    for layer in range(depth):
        m = mod[layer]
        nw = norm1_w[layer]
        kind = layer % 4
        if kind == 0:
            u = _in_proj(xs, nw, m, 0, 1, hy_w_in.astype(BF16), hy_b_in, n_lat, "hy_in_proj")
            y = _hyena_mixer(u, n_lat, n_ctx, hy_short_w, hy_short_b, hy_ffn_w1, hy_ffn_b1, hy_ffn_w2, hy_ffn_b2,
                             hy_ffn_w3, hy_sin_freq, hy_filter_bias)
            w_out, b_out, name = hy_w_out, hy_b_out, "hy_out_proj"
        elif kind == 1:
            u = _in_proj(xs, nw, m, 0, 1, sw_w, jnp.zeros((sw_w.shape[1],), F32), n_lat, "sw_in_proj")
            y = _swa_mixer(u, n_lat, n_ctx, sw_sink)
            w_out, b_out, name = sw_w_out, zero_bias, "sw_out_proj"
        elif kind == 2:
            u = _in_proj(xs, nw, m, 0, 1, gd_w, jnp.zeros((gd_w.shape[1],), F32), n_lat, "gd_in_proj")
            y = _gdn_mixer(u, n_lat, n_ctx, gd_conv_w, gd_a_log, gd_dt_bias, gd_norm_w)
            w_out, b_out, name = gd_w_out, zero_bias, "gd_out_proj"
        else:
            u = _in_proj(xs, nw, m, 0, 1, hg_w_in.astype(BF16), jnp.zeros((hg_w_in.shape[1],), F32), n_lat, "hg_in_proj")
            y = _hgrn_mixer(u, n_lat, n_ctx, hg_lb, hg_norm_w, layer)
            w_out, b_out, name = hg_w_out, zero_bias, "hg_out_proj"
        router_w = jnp.pad(moe_router[layer], ((0, 0), (0, LANES - N_EXPERTS)))
        router_hi = router_w.astype(BF16)
        router_p = jnp.concatenate([router_hi, (router_w - router_hi.astype(F32)).astype(BF16)], axis=1)
        xs, h, logits = _out_proj(y, w_out.astype(BF16), b_out, xs, m, n_lat, norm2_w[layer], router_p, name)
        xs = _moe_layer(xs, h, logits, m, moe_w_gate, moe_w_up, moe_w_down, layer, n_lat, final_norm_w,
                        layer == depth - 1)
    return xs
```
